```python
import math
import jax, jax.numpy as jnp
from jax import lax
import numpy as np


D_MODEL = 1024
BATCH = 16
SEQ = 2048
DEPTH = 1

HEAD_DIM = 64
FOX_HEADS = 12
DIL_HEADS = 12
MEM_HEADS = 4
MEM_HEAD_DIM = 128
MEM_LEN = 256
FOX_W = FOX_HEADS * HEAD_DIM
DIL_W = DIL_HEADS * HEAD_DIM
MEM_W = MEM_HEADS * MEM_HEAD_DIM
MIX_W = FOX_W + DIL_W + MEM_W
DILATIONS = ((128, 1), (512, 4), (2048, 16))
BLOCK = 128
ROPE_THETA = 500000.0
ROPE_DIM = HEAD_DIM // 4
RMS_EPS = 1e-6
NEG_INF = -1e30
IN_SIZES = [FOX_W] * 4 + [FOX_HEADS] + [DIL_W] * 4 + [MEM_W] * 2
IN_W = sum(IN_SIZES)

kernel_name = 'hymba_fox_dilated_memory_block'


def rmsnorm(x, g):
    xf = x.astype(jnp.float32)
    y = xf * lax.rsqrt(jnp.mean(xf * xf, axis=-1, keepdims=True) + RMS_EPS)
    return (y * g.astype(jnp.float32)).astype(x.dtype)


def rope_partial(t, pos):
    half = ROPE_DIM // 2
    inv_freq = 1.0 / (ROPE_THETA ** (jnp.arange(0, ROPE_DIM, 2, dtype=jnp.float32) / ROPE_DIM))
    ang = pos[:, None] * inv_freq[None, :]
    cos = jnp.cos(ang)[None, :, None, :]
    sin = jnp.sin(ang)[None, :, None, :]
    tr = t[..., :ROPE_DIM].astype(jnp.float32)
    t1, t2 = tr[..., :half], tr[..., half:]
    rot = jnp.concatenate([t1 * cos - t2 * sin, t2 * cos + t1 * sin], axis=-1)
    return jnp.concatenate([rot.astype(t.dtype), t[..., ROPE_DIM:]], axis=-1)


def forgetting_attention(q, k, v, logf):
    B, S, H, E = q.shape
    scale = 1.0 / math.sqrt(E)
    c = jnp.cumsum(logf, axis=1).transpose(0, 2, 1)
    vf = v.astype(jnp.float32)
    outs = []
    for i in range(S // BLOCK):
        q0, q1 = i * BLOCK, (i + 1) * BLOCK
        s = jnp.einsum('bqhe,bkhe->bhqk', q[:, q0:q1], k[:, :q1]).astype(jnp.float32) * scale
        s = s + c[:, :, q0:q1, None] - c[:, :, None, :q1]
        mask = (q0 + jnp.arange(BLOCK))[:, None] >= jnp.arange(q1)[None, :]
        s = jnp.where(mask[None, None], s, NEG_INF)
        p = jax.nn.softmax(s, axis=-1)
        outs.append(jnp.einsum('bhqk,bkhe->bqhe', p, vf[:, :q1]))
    return jnp.concatenate(outs, axis=1)


def dilated_pattern(q, k, v, dilation, n_steps):
    B, S, H, E = q.shape
    L = S // dilation
    nb = -(-L // BLOCK)
    Lp = nb * BLOCK
    scale = 1.0 / math.sqrt(E)

    def to_blocks(t):
        t = t.reshape(B, L, dilation, H, E)
        t = jnp.pad(t, ((0, 0), (0, Lp - L), (0, 0), (0, 0), (0, 0)))
        return t.reshape(B, nb, BLOCK, dilation, H, E)

    def with_prev(t):
        prev = jnp.pad(t, ((0, 0), (1, 0), (0, 0), (0, 0), (0, 0), (0, 0)))[:, :nb]
        return jnp.concatenate([prev, t], axis=2)

    qb = to_blocks(q)
    kc = with_prev(to_blocks(k))
    vc = with_prev(to_blocks(v)).astype(jnp.float32)
    s = jnp.einsum('bnqrhe,bnkrhe->bnrhqk', qb, kc).astype(jnp.float32) * scale
    lq = jnp.arange(nb)[:, None] * BLOCK + jnp.arange(BLOCK)[None, :]
    lk = (jnp.arange(nb)[:, None] - 1) * BLOCK + jnp.arange(2 * BLOCK)[None, :]
    delta = lq[:, :, None] - lk[:, None, :]
    mask = (delta >= 0) & (delta <= n_steps) & (lk[:, None, :] >= 0)
    s = jnp.where(mask[None, :, None, None], s, NEG_INF)
    m = jnp.max(s, axis=-1, keepdims=True)
    e = jnp.exp(s - m)
    den = jnp.sum(e, axis=-1)
    num = jnp.einsum('bnrhqk,bnkrhe->bnqrhe', e, vc)
    num = num.reshape(B, Lp, dilation, H, E)[:, :L].reshape(B, S, H, E)

    def rows(t):
        t = t.transpose(0, 1, 4, 2, 3).reshape(B, Lp, dilation, H)
        return t[:, :L].reshape(B, S, H)

    return num, rows(den), rows(m[..., 0])


def dilated_attention(q, k, v):
    parts = [dilated_pattern(q, k, v, d, w // d) for (w, d) in DILATIONS]
    m_all = parts[0][2]
    for p in parts[1:]:
        m_all = jnp.maximum(m_all, p[2])
    num_tot = 0.0
    den_tot = 0.0
    for num, den, m in parts:
        w = jnp.exp(m - m_all)
        num_tot = num_tot + num * w[..., None]
        den_tot = den_tot + den * w
    return num_tot / den_tot[..., None]


def memory_attention(q, mk, mv):
    scale = 1.0 / math.sqrt(q.shape[-1])
    s = jnp.einsum('bqhe,bkhe->bhqk', q, mk).astype(jnp.float32) * scale
    p = jax.nn.softmax(s, axis=-1)
    return jnp.einsum('bhqk,bkhe->bqhe', p, mv.astype(jnp.float32))


def setup_inputs(seed: int = 0) -> dict:
    key = jax.random.key(seed)
    ks = jax.random.split(key, 10)
    f32 = jnp.float32
    x = jax.random.normal(ks[0], (BATCH, SEQ, D_MODEL), f32)
    mem = jax.random.normal(ks[1], (BATCH, MEM_LEN, D_MODEL), f32)
    norm_g = 1.0 + 0.02 * jax.random.normal(ks[2], (DEPTH, D_MODEL), f32)
    w_in = jax.random.normal(ks[3], (DEPTH, D_MODEL, IN_W), f32) * D_MODEL ** -0.5
    b_forget = jax.random.uniform(ks[4], (DEPTH, FOX_HEADS), f32, 1.0, 4.0)
    mem_norm_g = 1.0 + 0.02 * jax.random.normal(ks[5], (DEPTH, D_MODEL), f32)
    w_mem_kv = jax.random.normal(ks[6], (DEPTH, D_MODEL, 2 * MEM_W), f32) * D_MODEL ** -0.5
    w_out = jax.random.normal(ks[7], (DEPTH, MIX_W, D_MODEL), f32) * MIX_W ** -0.5
    final_norm_g = 1.0 + 0.02 * jax.random.normal(ks[8], (D_MODEL,), f32)
    return {'x': x, 'mem': mem, 'norm_g': norm_g, 'w_in': w_in, 'b_forget': b_forget,
            'mem_norm_g': mem_norm_g, 'w_mem_kv': w_mem_kv, 'w_out': w_out,
            'final_norm_g': final_norm_g}


def reference(x, mem, norm_g, w_in, b_forget, mem_norm_g, w_mem_kv, w_out, final_norm_g):
    B, S, _ = x.shape
    pos = jnp.arange(S, dtype=jnp.float32)
    split_idx = np.cumsum(IN_SIZES)[:-1].tolist()
    for l in range(DEPTH):
        h = rmsnorm(x, norm_g[l])
        proj = h @ w_in[l]
        (fq, fk, fv, fg, flog, dq, dk, dv, dg, mq, mg) = jnp.split(proj, split_idx, axis=-1)

        logf = jax.nn.log_sigmoid((flog + b_forget[l]).astype(jnp.float32))
        hs = (B, S, FOX_HEADS, HEAD_DIM)
        fox = forgetting_attention(fq.reshape(hs), fk.reshape(hs), fv.reshape(hs), logf)
        fox = fox.reshape(B, S, FOX_W).astype(x.dtype)

        hs = (B, S, DIL_HEADS, HEAD_DIM)
        dqr = rope_partial(dq.reshape(hs), pos)
        dkr = rope_partial(dk.reshape(hs), pos)
        dil = dilated_attention(dqr, dkr, dv.reshape(hs)).reshape(B, S, DIL_W).astype(x.dtype)

        mh = rmsnorm(mem, mem_norm_g[l])
        mk, mv = jnp.split(mh @ w_mem_kv[l], 2, axis=-1)
        ms = (B, mem.shape[1], MEM_HEADS, MEM_HEAD_DIM)
        memo = memory_attention(mq.reshape(B, S, MEM_HEADS, MEM_HEAD_DIM), mk.reshape(ms), mv.reshape(ms))
        memo = memo.reshape(B, S, MEM_W).astype(x.dtype)

        y = jnp.concatenate([fox * jax.nn.silu(fg), dil * jax.nn.silu(dg), memo * jax.nn.silu(mg)], axis=-1)
        x = x + y @ w_out[l]
    return rmsnorm(x, final_norm_g)
```

```python
import functools
import math

import jax
import jax.numpy as jnp
from jax import lax
from jax.experimental import pallas as pl
from jax.experimental.pallas import tpu as pltpu

F32 = jnp.float32
BF16 = jnp.bfloat16

D_MODEL = 1024
HEAD_DIM = 64
FOX_HEADS = 12
DIL_HEADS = 12
MEM_HEADS = 4
MEM_HEAD_DIM = 128
FOX_W = FOX_HEADS * HEAD_DIM
DIL_W = DIL_HEADS * HEAD_DIM
MEM_W = MEM_HEADS * MEM_HEAD_DIM
MIX_W = FOX_W + DIL_W + MEM_W
DILATIONS = ((128, 1), (512, 4), (2048, 16))
BLOCK = 128
ROPE_THETA = 500000.0
ROPE_DIM = HEAD_DIM // 4
RMS_EPS = 1e-6
NEG_INF = -1e30
QK_SCALE = 1.0 / math.sqrt(HEAD_DIM)
MEM_SCALE = 1.0 / math.sqrt(MEM_HEAD_DIM)

LANES = 128
HEAD_PAIRS = FOX_HEADS // 2
C_ROWS = 16

_SEG = dict(fq=0, fk=768, fv=1536, fg=2304, dq=3072, dk=3840, dv=4608, dg=5376, mq=6144, mg=6656)
MAIN_W = 7168

VMEM_LIMIT = 56 * 1024 * 1024


def _silu(r):
    return r * (1.0 / (1.0 + jnp.exp(-r)))


def _rows(ref, start, size, stride):
    if stride == 1:
        return ref[pl.ds(start, size), :]
    return ref[pl.ds(start, size, stride=stride), :]


def _set_rows(ref, start, size, stride, val):
    if stride == 1:
        ref[pl.ds(start, size), :] = val
    else:
        ref[pl.ds(start, size, stride=stride), :] = val


def _in_proj_kernel(x_ref, g_ref, w_ref, wfl_ref, bf_ref, cos_ref, sina_ref, sinb_ref,
                    of_ref, od_ref, odg_ref, om_ref, ccol_ref, crow_ref, carry_ref,
                    *, tm, tiles_per_seq):
    i = pl.program_id(0)
    x = x_ref[...]
    ms = jnp.mean(x * x, axis=-1, keepdims=True)
    h = (x * lax.rsqrt(ms + RMS_EPS) * g_ref[...]).astype(BF16)

    def seg(name, width):
        off = _SEG[name]
        return jnp.dot(h, w_ref[:, off:off + width], preferred_element_type=F32)

    of_ref[:, 0:768] = (seg("fq", 768) * QK_SCALE).astype(BF16)
    of_ref[:, 768:1536] = seg("fk", 768).astype(BF16)
    of_ref[:, 1536:2304] = seg("fv", 768).astype(BF16)
    of_ref[:, 2304:3072] = _silu(seg("fg", 768)).astype(BF16)

    cos = cos_ref[...]
    sina = sina_ref[...]
    sinb = sinb_ref[...]

    def rope_store(r, dst_off, scale):
        for j in range(DIL_W // LANES):
            c = r[:, LANES * j:LANES * (j + 1)]
            rot = c * cos + pltpu.roll(c, ROPE_DIM // 2, 1) * sina \
                + pltpu.roll(c, LANES - ROPE_DIM // 2, 1) * sinb
            if scale != 1.0:
                rot = rot * scale
            od_ref[:, dst_off + LANES * j:dst_off + LANES * (j + 1)] = rot

    rope_store(seg("dq", 768), 0, QK_SCALE)
    rope_store(seg("dk", 768), 768, 1.0)
    od_ref[:, 1536:2304] = seg("dv", 768)
    odg_ref[...] = _silu(seg("dg", 768)).astype(BF16)

    om_ref[:, 0:512] = seg("mq", 512).astype(BF16)
    om_ref[:, 512:1024] = _silu(seg("mg", 512)).astype(BF16)

    fl = jnp.dot(h, wfl_ref[...], preferred_element_type=F32) + bf_ref[...]
    logf = jnp.minimum(fl, 0.0) - jnp.log1p(jnp.exp(-jnp.abs(fl)))
    a1 = logf.astype(BF16)
    r1 = logf - a1.astype(F32)
    a2 = r1.astype(BF16)
    a3 = (r1 - a2.astype(F32)).astype(BF16)
    row = lax.broadcasted_iota(jnp.int32, (tm, tm), 0)
    col = lax.broadcasted_iota(jnp.int32, (tm, tm), 1)
    tri = jnp.where(row >= col, 1.0, 0.0).astype(BF16)

    @pl.when(i % tiles_per_seq == 0)
    def _():
        carry_ref[...] = jnp.zeros_like(carry_ref)

    c = (jnp.dot(tri, a1, preferred_element_type=F32)
         + jnp.dot(tri, a2, preferred_element_type=F32)
         + jnp.dot(tri, a3, preferred_element_type=F32)) + carry_ref[...]
    carry_ref[...] = c[tm - 1:tm, :]
    ccol_ref[...] = c
    crow_ref[0] = c.T[0:C_ROWS, :]


def _in_proj(x2, g, w_main, w_fl, b_fl, cos_t, sina_t, sinb_t, *, seq, tm=256):
    n = x2.shape[0]
    tiles_per_seq = seq // tm
    batch = n // seq
    const = lambda i: (0, 0)
    row = lambda i: (i, 0)
    pos = lambda i: (i % tiles_per_seq, 0)
    kernel = functools.partial(_in_proj_kernel, tm=tm, tiles_per_seq=tiles_per_seq)
    return pl.pallas_call(
        kernel,
        grid=(n // tm,),
        in_specs=[
            pl.BlockSpec((tm, D_MODEL), row),
            pl.BlockSpec((1, D_MODEL), const),
            pl.BlockSpec((D_MODEL, MAIN_W), const),
            pl.BlockSpec((D_MODEL, LANES), const),
            pl.BlockSpec((1, LANES), const),
            pl.BlockSpec((tm, LANES), pos),
            pl.BlockSpec((tm, LANES), pos),
            pl.BlockSpec((tm, LANES), pos),
        ],
        out_specs=[
            pl.BlockSpec((tm, 3072), row),
            pl.BlockSpec((tm, 2304), row),
            pl.BlockSpec((tm, 768), row),
            pl.BlockSpec((tm, 1024), row),
            pl.BlockSpec((tm, LANES), row),
            pl.BlockSpec((1, C_ROWS, tm), lambda i: (i // tiles_per_seq, 0, i % tiles_per_seq)),
        ],
        out_shape=[
            jax.ShapeDtypeStruct((n, 3072), BF16),
            jax.ShapeDtypeStruct((n, 2304), F32),
            jax.ShapeDtypeStruct((n, 768), BF16),
            jax.ShapeDtypeStruct((n, 1024), BF16),
            jax.ShapeDtypeStruct((n, LANES), F32),
            jax.ShapeDtypeStruct((batch, C_ROWS, seq), F32),
        ],
        scratch_shapes=[pltpu.VMEM((1, LANES), F32)],
        compiler_params=pltpu.CompilerParams(
            dimension_semantics=("arbitrary",), vmem_limit_bytes=VMEM_LIMIT),
        name="in_proj",
    )(x2, g, w_main, w_fl, b_fl, cos_t, sina_t, sinb_t)


def _fox_kernel(q_ref, k_ref, v_ref, g_ref, ccol_ref, crow_ref, o_ref, *, tq):
    i = pl.program_id(1)
    tk = tq
    lane = lax.broadcasted_iota(jnp.int32, (tq, LANES), 1)
    lo = lane < HEAD_DIM
    row = lax.broadcasted_iota(jnp.int32, (tq, tk), 0)
    col = lax.broadcasted_iota(jnp.int32, (tq, tk), 1)
    causal = row >= col

    for hp in range(HEAD_PAIRS):
        ls = slice(LANES * hp, LANES * (hp + 1))
        q2 = q_ref[0, :, ls]
        outs = []
        for hh in range(2):
            h = 2 * hp + hh
            qh = jnp.where(lo if hh == 0 else jnp.logical_not(lo), q2, jnp.zeros_like(q2))
            cq = ccol_ref[0, :, h:h + 1]

            def step(j, carry, masked, qh=qh, cq=cq, h=h, ls=ls):
                m, l, acc = carry
                off = pl.multiple_of(j * tk, tk)
                kt = k_ref[0, pl.ds(off, tk), ls]
                vt = v_ref[0, pl.ds(off, tk), ls]
                s = lax.dot_general(qh, kt, (((1,), (1,)), ((), ())), preferred_element_type=F32)
                s = s + cq - crow_ref[0, h:h + 1, pl.ds(off, tk)]
                if masked:
                    s = jnp.where(causal, s, NEG_INF)
                m_new = jnp.maximum(m, jnp.max(s, axis=-1, keepdims=True))
                alpha = jnp.exp(m - m_new)
                p = jnp.exp(s - m_new)
                l = alpha * l + jnp.sum(p, axis=-1, keepdims=True)
                acc = alpha * acc + jnp.dot(p.astype(BF16), vt, preferred_element_type=F32)
                return m_new, l, acc

            init = (jnp.full((tq, 1), NEG_INF, F32), jnp.zeros((tq, 1), F32),
                    jnp.zeros((tq, LANES), F32))
            carry = lax.fori_loop(0, i, functools.partial(step, masked=False), init)
            _, l, acc = step(i, carry, True)
            outs.append(acc / l)
        o2 = jnp.where(lo, outs[0], outs[1])
        o_ref[0, :, ls] = (o2 * g_ref[0, :, ls].astype(F32)).astype(BF16)


def _fox(of3, ccol3, crow, *, tq=256):
    batch, seq, _ = of3.shape
    kernel = functools.partial(_fox_kernel, tq=tq)
    return pl.pallas_call(
        kernel,
        grid=(batch, seq // tq),
        in_specs=[
            pl.BlockSpec((1, tq, FOX_W), lambda b, i: (b, i, 0)),
            pl.BlockSpec((1, seq, FOX_W), lambda b, i: (b, 0, 1)),
            pl.BlockSpec((1, seq, FOX_W), lambda b, i: (b, 0, 2)),
            pl.BlockSpec((1, tq, FOX_W), lambda b, i: (b, i, 3)),
            pl.BlockSpec((1, tq, LANES), lambda b, i: (b, i, 0)),
            pl.BlockSpec((1, C_ROWS, seq), lambda b, i: (b, 0, 0)),
        ],
        out_specs=pl.BlockSpec((1, tq, FOX_W), lambda b, i: (b, i, 0)),
        out_shape=jax.ShapeDtypeStruct((batch, seq, FOX_W), BF16),
        compiler_params=pltpu.CompilerParams(
            dimension_semantics=("arbitrary", "arbitrary"), vmem_limit_bytes=VMEM_LIMIT),
        name="fox_attn",
    )(of3, of3, of3, of3, ccol3, crow)


def _dil_kernel(q_ref, k_ref, v_ref, g_ref, o_ref, qd, kd, vd, acc_ref, m_ref, l_ref, *, seq):
    lane = lax.broadcasted_iota(jnp.int32, (BLOCK, LANES), 1)
    lo = lane < HEAD_DIM
    row2 = lax.broadcasted_iota(jnp.int32, (BLOCK, 2 * BLOCK), 0)
    col2 = lax.broadcasted_iota(jnp.int32, (BLOCK, 2 * BLOCK), 1)
    band = jnp.logical_and(col2 >= row2, col2 <= row2 + BLOCK)
    row1 = lax.broadcasted_iota(jnp.int32, (BLOCK, BLOCK), 0)
    col1 = lax.broadcasted_iota(jnp.int32, (BLOCK, BLOCK), 1)
    causal = col1 <= row1

    def block(base, nat_start, stride, with_prev, first_pattern):
        q2 = qd[pl.ds(base, BLOCK), :]
        if with_prev:
            kk = kd[pl.ds(base - BLOCK, 2 * BLOCK), :]
            vv = vd[pl.ds(base - BLOCK, 2 * BLOCK), :]
            mask = band
        else:
            kk = kd[pl.ds(base, BLOCK), :]
            vv = vd[pl.ds(base, BLOCK), :]
            mask = causal
        if stride == 1:
            nat_start = base
        nums, mxs, dens = [], [], []
        for hh in range(2):
            qh = jnp.where(lo if hh == 0 else jnp.logical_not(lo), q2, jnp.zeros_like(q2))
            s = lax.dot_general(qh, kk, (((1,), (1,)), ((), ())), preferred_element_type=F32)
            s = jnp.where(mask, s, NEG_INF)
            mx = jnp.max(s, axis=-1, keepdims=True)
            e = jnp.exp(s - mx)
            dens.append(jnp.sum(e, axis=-1, keepdims=True))
            mxs.append(mx)
            nums.append(jnp.dot(e.astype(BF16), vv, preferred_element_type=F32))
        num = jnp.where(lo, nums[0], nums[1])
        mx = jnp.where(lo, mxs[0], mxs[1])
        den = jnp.where(lo, dens[0], dens[1])
        if first_pattern:
            _set_rows(acc_ref, nat_start, BLOCK, stride, num)
            _set_rows(m_ref, nat_start, BLOCK, stride, mx)
            _set_rows(l_ref, nat_start, BLOCK, stride, den)
        else:
            m_old = _rows(m_ref, nat_start, BLOCK, stride)
            m_new = jnp.maximum(m_old, mx)
            a = jnp.exp(m_old - m_new)
            b = jnp.exp(mx - m_new)
            _set_rows(acc_ref, nat_start, BLOCK, stride,
                      _rows(acc_ref, nat_start, BLOCK, stride) * a + num * b)
            _set_rows(l_ref, nat_start, BLOCK, stride,
                      _rows(l_ref, nat_start, BLOCK, stride) * a + den * b)
            _set_rows(m_ref, nat_start, BLOCK, stride, m_new)

    for pi, (window, d) in enumerate(DILATIONS):
        assert window // d == BLOCK
        length = seq // d
        nb = length // BLOCK
        first = pi == 0
        for r in range(d):
            dst = slice(r * length, (r + 1) * length)
            qd[dst, :] = _rows(q_ref.at[0], r, length, d).astype(BF16)
            kd[dst, :] = _rows(k_ref.at[0], r, length, d).astype(BF16)
            vd[dst, :] = _rows(v_ref.at[0], r, length, d).astype(BF16)

        if nb == 1:
            def body(r, _, d=d, length=length, first=first):
                block(pl.multiple_of(r * length, BLOCK), r, d, False, first)
                return 0
            lax.fori_loop(0, d, body, 0)
        else:
            for r in range(d):
                block(r * length, r, d, False, first)

                def body(n, _, r=r, d=d, length=length, first=first):
                    base = pl.multiple_of(r * length + n * BLOCK, BLOCK)
                    block(base, n * BLOCK * d + r, d, True, first)
                    return 0
                lax.fori_loop(1, nb, body, 0)

    o = acc_ref[...] / l_ref[...]
    o_ref[0] = (o * g_ref[0].astype(F32)).astype(BF16)


def _dilated(od3, odg3):
    batch, seq, _ = od3.shape
    kernel = functools.partial(_dil_kernel, seq=seq)
    return pl.pallas_call(
        kernel,
        grid=(batch, HEAD_PAIRS),
        in_specs=[
            pl.BlockSpec((1, seq, LANES), lambda b, p: (b, 0, p)),
            pl.BlockSpec((1, seq, LANES), lambda b, p: (b, 0, HEAD_PAIRS + p)),
            pl.BlockSpec((1, seq, LANES), lambda b, p: (b, 0, 2 * HEAD_PAIRS + p)),
            pl.BlockSpec((1, seq, LANES), lambda b, p: (b, 0, p)),
        ],
        out_specs=pl.BlockSpec((1, seq, LANES), lambda b, p: (b, 0, p)),
        out_shape=jax.ShapeDtypeStruct((batch, seq, DIL_W), BF16),
        scratch_shapes=[
            pltpu.VMEM((seq, LANES), BF16),
            pltpu.VMEM((seq, LANES), BF16),
            pltpu.VMEM((seq, LANES), BF16),
            pltpu.VMEM((seq, LANES), F32),
            pltpu.VMEM((seq, LANES), F32),
            pltpu.VMEM((seq, LANES), F32),
        ],
        compiler_params=pltpu.CompilerParams(
            dimension_semantics=("arbitrary", "arbitrary"), vmem_limit_bytes=VMEM_LIMIT),
        name="dilated_attn",
    )(od3, od3, od3, odg3)


def _mem_kv_kernel(mem_ref, g_ref, w_ref, o_ref):
    x = mem_ref[0]
    ms = jnp.mean(x * x, axis=-1, keepdims=True)
    h = (x * lax.rsqrt(ms + RMS_EPS) * g_ref[...]).astype(BF16)
    o_ref[0] = jnp.dot(h, w_ref[...], preferred_element_type=F32).astype(BF16)


def _mem_kv(mem, g, w):
    batch, mlen, _ = mem.shape
    return pl.pallas_call(
        _mem_kv_kernel,
        grid=(batch,),
        in_specs=[
            pl.BlockSpec((1, mlen, D_MODEL), lambda b: (b, 0, 0)),
            pl.BlockSpec((1, D_MODEL), lambda b: (0, 0)),
            pl.BlockSpec((D_MODEL, 2 * MEM_W), lambda b: (0, 0)),
        ],
        out_specs=pl.BlockSpec((1, mlen, 2 * MEM_W), lambda b: (b, 0, 0)),
        out_shape=jax.ShapeDtypeStruct((batch, mlen, 2 * MEM_W), BF16),
        compiler_params=pltpu.CompilerParams(
            dimension_semantics=("arbitrary",), vmem_limit_bytes=VMEM_LIMIT),
        name="mem_kv",
    )(mem, g, w)


def _mem_attn_kernel(q_ref, g_ref, mk_ref, mv_ref, o_ref):
    for h in range(MEM_HEADS):
        ls = slice(MEM_HEAD_DIM * h, MEM_HEAD_DIM * (h + 1))
        s = lax.dot_general(q_ref[0, :, ls], mk_ref[0, :, ls], (((1,), (1,)), ((), ())),
                            preferred_element_type=F32) * MEM_SCALE
        mx = jnp.max(s, axis=-1, keepdims=True)
        e = jnp.exp(s - mx)
        den = jnp.sum(e, axis=-1, keepdims=True)
        o = jnp.dot(e.astype(BF16), mv_ref[0, :, ls], preferred_element_type=F32) / den
        o_ref[0, :, ls] = (o * g_ref[0, :, ls].astype(F32)).astype(BF16)


def _mem_attn(om3, mkv, *, tq=512):
    batch, seq, _ = om3.shape
    mlen = mkv.shape[1]
    return pl.pallas_call(
        _mem_attn_kernel,
        grid=(batch, seq // tq),
        in_specs=[
            pl.BlockSpec((1, tq, MEM_W), lambda b, i: (b, i, 0)),
            pl.BlockSpec((1, tq, MEM_W), lambda b, i: (b, i, 1)),
            pl.BlockSpec((1, mlen, MEM_W), lambda b, i: (b, 0, 0)),
            pl.BlockSpec((1, mlen, MEM_W), lambda b, i: (b, 0, 1)),
        ],
        out_specs=pl.BlockSpec((1, tq, MEM_W), lambda b, i: (b, i, 0)),
        out_shape=jax.ShapeDtypeStruct((batch, seq, MEM_W), BF16),
        compiler_params=pltpu.CompilerParams(
            dimension_semantics=("arbitrary", "arbitrary"), vmem_limit_bytes=VMEM_LIMIT),
        name="mem_attn",
    )(om3, om3, mkv, mkv)


def _out_proj_kernel(x_ref, yf_ref, yd_ref, ym_ref, w_ref, g_ref, o_ref, *, final_norm):
    z = x_ref[...]
    z = z + jnp.dot(yf_ref[...], w_ref[0:FOX_W, :], preferred_element_type=F32)
    z = z + jnp.dot(yd_ref[...], w_ref[FOX_W:FOX_W + DIL_W, :], preferred_element_type=F32)
    z = z + jnp.dot(ym_ref[...], w_ref[FOX_W + DIL_W:MIX_W, :], preferred_element_type=F32)
    if final_norm:
        ms = jnp.mean(z * z, axis=-1, keepdims=True)
        z = z * lax.rsqrt(ms + RMS_EPS) * g_ref[...]
    o_ref[...] = z


def _out_proj(x2, yf, yd, ym, w, g, *, final_norm, tm=256):
    n = x2.shape[0]
    row = lambda i: (i, 0)
    const = lambda i: (0, 0)
    kernel = functools.partial(_out_proj_kernel, final_norm=final_norm)
    return pl.pallas_call(
        kernel,
        grid=(n // tm,),
        in_specs=[
            pl.BlockSpec((tm, D_MODEL), row),
            pl.BlockSpec((tm, FOX_W), row),
            pl.BlockSpec((tm, DIL_W), row),
            pl.BlockSpec((tm, MEM_W), row),
            pl.BlockSpec((MIX_W, D_MODEL), const),
            pl.BlockSpec((1, D_MODEL), const),
        ],
        out_specs=pl.BlockSpec((tm, D_MODEL), row),
        out_shape=jax.ShapeDtypeStruct((n, D_MODEL), F32),
        compiler_params=pltpu.CompilerParams(
            dimension_semantics=("arbitrary",), vmem_limit_bytes=VMEM_LIMIT),
        name="out_proj",
    )(x2, yf, yd, ym, w, g)


def _rope_tables(seq):
    half = ROPE_DIM // 2
    pos = jnp.arange(seq, dtype=F32)
    inv_freq = 1.0 / (ROPE_THETA ** (jnp.arange(0, ROPE_DIM, 2, dtype=F32) / ROPE_DIM))
    ang = pos[:, None] * inv_freq[None, :]
    cos = jnp.cos(ang)
    sin = jnp.sin(ang)
    ones = jnp.ones((seq, HEAD_DIM - ROPE_DIM), F32)
    zeros = jnp.zeros((seq, HEAD_DIM - ROPE_DIM), F32)
    zh = jnp.zeros((seq, half), F32)
    cos_h = jnp.concatenate([cos, cos, ones], axis=1)
    sina_h = jnp.concatenate([zh, sin, zeros], axis=1)
    sinb_h = jnp.concatenate([-sin, zh, zeros], axis=1)
    tile = lambda t: jnp.concatenate([t, t], axis=1)
    return tile(cos_h), tile(sina_h), tile(sinb_h)


def _split_w_in(w):
    sizes = [FOX_W] * 4 + [FOX_HEADS] + [DIL_W] * 4 + [MEM_W] * 2
    offs = [0]
    for s in sizes:
        offs.append(offs[-1] + s)
    cols = [w[:, offs[k]:offs[k + 1]] for k in range(len(sizes))]
    fq, fk, fv, fg, flog, dq, dk, dv, dg, mq, mg = cols
    w_main = jnp.concatenate([fq, fk, fv, fg, dq, dk, dv, dg, mq, mg], axis=1).astype(BF16)
    w_fl = jnp.pad(flog, ((0, 0), (0, LANES - FOX_HEADS))).astype(BF16)
    return w_main, w_fl


def kernel(x, mem, norm_g, w_in, b_forget, mem_norm_g, w_mem_kv, w_out, final_norm_g):
    batch, seq, _ = x.shape
    depth = norm_g.shape[0]
    cos_t, sina_t, sinb_t = _rope_tables(seq)
    x2 = x.reshape(batch * seq, D_MODEL)
    for l in range(depth):
        w_main, w_fl = _split_w_in(w_in[l])
        b_fl = jnp.pad(b_forget[l], (0, LANES - FOX_HEADS)).reshape(1, LANES)
        of, od, odg, om, ccol, crow = _in_proj(
            x2, norm_g[l].reshape(1, D_MODEL), w_main, w_fl, b_fl, cos_t, sina_t, sinb_t, seq=seq)
        yf = _fox(of.reshape(batch, seq, 3072), ccol.reshape(batch, seq, LANES), crow)
        yd = _dilated(od.reshape(batch, seq, 2304), odg.reshape(batch, seq, DIL_W))
        mkv = _mem_kv(mem, mem_norm_g[l].reshape(1, D_MODEL), w_mem_kv[l].astype(BF16))
        ym = _mem_attn(om.reshape(batch, seq, 1024), mkv)
        last = l == depth - 1
        x2 = _out_proj(x2, yf.reshape(batch * seq, FOX_W), yd.reshape(batch * seq, DIL_W),
                       ym.reshape(batch * seq, MEM_W), w_out[l].astype(BF16),
                       final_norm_g.reshape(1, D_MODEL), final_norm=last)
    return x2.reshape(batch, seq, D_MODEL)
```

```python
import functools
import math

import jax
import jax.numpy as jnp
from jax import lax
from jax.experimental import pallas as pl
from jax.experimental.pallas import tpu as pltpu

F32 = jnp.float32
BF16 = jnp.bfloat16

D_MODEL = 1024
HEAD_DIM = 64
FOX_HEADS = 12
DIL_HEADS = 12
MEM_HEADS = 4
MEM_HEAD_DIM = 128
FOX_W = FOX_HEADS * HEAD_DIM
DIL_W = DIL_HEADS * HEAD_DIM
MEM_W = MEM_HEADS * MEM_HEAD_DIM
MIX_W = FOX_W + DIL_W + MEM_W
DILATIONS = ((128, 1), (512, 4), (2048, 16))
BLOCK = 128
ROPE_THETA = 500000.0
ROPE_DIM = HEAD_DIM // 4
RMS_EPS = 1e-6
NEG_INF = -1e30
QK_SCALE = 1.0 / math.sqrt(HEAD_DIM)
MEM_SCALE = 1.0 / math.sqrt(MEM_HEAD_DIM)

LANES = 128
HEAD_PAIRS = FOX_HEADS // 2
C_ROWS = 16

_SEG = dict(fq=0, fk=768, fv=1536, fg=2304, dq=3072, dk=3840, dv=4608, dg=5376, mq=6144, mg=6656)
MAIN_W = 7168

VMEM_LIMIT = 56 * 1024 * 1024


def _silu(r):
    return r * (1.0 / (1.0 + jnp.exp(-r)))


def _rows(ref, start, size, stride):
    if stride == 1:
        return ref[pl.ds(start, size), :]
    return ref[pl.ds(start, size, stride=stride), :]


def _set_rows(ref, start, size, stride, val):
    if stride == 1:
        ref[pl.ds(start, size), :] = val
    else:
        ref[pl.ds(start, size, stride=stride), :] = val


def _in_proj_kernel(x_ref, g_ref, w_ref, wfl_ref, bf_ref, cos_ref, sina_ref, sinb_ref,
                    of_ref, od_ref, odg_ref, om_ref, ccol_ref, crow_ref, carry_ref,
                    *, tm, tiles_per_seq):
    i = pl.program_id(0)
    x = x_ref[...]
    ms = jnp.mean(x * x, axis=-1, keepdims=True)
    h = (x * lax.rsqrt(ms + RMS_EPS) * g_ref[...]).astype(BF16)

    def seg(name, width):
        off = _SEG[name]
        return jnp.dot(h, w_ref[:, off:off + width], preferred_element_type=F32)

    of_ref[:, 0:768] = (seg("fq", 768) * QK_SCALE).astype(BF16)
    of_ref[:, 768:1536] = seg("fk", 768).astype(BF16)
    of_ref[:, 1536:2304] = seg("fv", 768).astype(BF16)
    of_ref[:, 2304:3072] = _silu(seg("fg", 768)).astype(BF16)

    cos = cos_ref[...]
    sina = sina_ref[...]
    sinb = sinb_ref[...]

    def rope_store(r, dst_off, scale):
        for j in range(DIL_W // LANES):
            c = r[:, LANES * j:LANES * (j + 1)]
            rot = c * cos + pltpu.roll(c, ROPE_DIM // 2, 1) * sina \
                + pltpu.roll(c, LANES - ROPE_DIM // 2, 1) * sinb
            if scale != 1.0:
                rot = rot * scale
            od_ref[:, dst_off + LANES * j:dst_off + LANES * (j + 1)] = rot

    rope_store(seg("dq", 768), 0, QK_SCALE)
    rope_store(seg("dk", 768), 768, 1.0)
    od_ref[:, 1536:2304] = seg("dv", 768)
    odg_ref[...] = _silu(seg("dg", 768)).astype(BF16)

    om_ref[:, 0:512] = seg("mq", 512).astype(BF16)
    om_ref[:, 512:1024] = _silu(seg("mg", 512)).astype(BF16)

    fl = jnp.dot(h, wfl_ref[...], preferred_element_type=F32) + bf_ref[...]
    logf = jnp.minimum(fl, 0.0) - jnp.log1p(jnp.exp(-jnp.abs(fl)))
    a1 = logf.astype(BF16)
    r1 = logf - a1.astype(F32)
    a2 = r1.astype(BF16)
    a3 = (r1 - a2.astype(F32)).astype(BF16)
    row = lax.broadcasted_iota(jnp.int32, (tm, tm), 0)
    col = lax.broadcasted_iota(jnp.int32, (tm, tm), 1)
    tri = jnp.where(row >= col, 1.0, 0.0).astype(BF16)

    @pl.when(i % tiles_per_seq == 0)
    def _():
        carry_ref[...] = jnp.zeros_like(carry_ref)

    c = (jnp.dot(tri, a1, preferred_element_type=F32)
         + jnp.dot(tri, a2, preferred_element_type=F32)
         + jnp.dot(tri, a3, preferred_element_type=F32)) + carry_ref[...]
    carry_ref[...] = c[tm - 1:tm, :]
    ccol_ref[...] = c
    crow_ref[0] = c.T[0:C_ROWS, :]


def _in_proj(x2, g, w_main, w_fl, b_fl, cos_t, sina_t, sinb_t, *, seq, tm=256):
    n = x2.shape[0]
    tiles_per_seq = seq // tm
    batch = n // seq
    const = lambda i: (0, 0)
    row = lambda i: (i, 0)
    pos = lambda i: (i % tiles_per_seq, 0)
    kernel = functools.partial(_in_proj_kernel, tm=tm, tiles_per_seq=tiles_per_seq)
    return pl.pallas_call(
        kernel,
        grid=(n // tm,),
        in_specs=[
            pl.BlockSpec((tm, D_MODEL), row),
            pl.BlockSpec((1, D_MODEL), const),
            pl.BlockSpec((D_MODEL, MAIN_W), const),
            pl.BlockSpec((D_MODEL, LANES), const),
            pl.BlockSpec((1, LANES), const),
            pl.BlockSpec((tm, LANES), pos),
            pl.BlockSpec((tm, LANES), pos),
            pl.BlockSpec((tm, LANES), pos),
        ],
        out_specs=[
            pl.BlockSpec((tm, 3072), row),
            pl.BlockSpec((tm, 2304), row),
            pl.BlockSpec((tm, 768), row),
            pl.BlockSpec((tm, 1024), row),
            pl.BlockSpec((tm, LANES), row),
            pl.BlockSpec((1, C_ROWS, tm), lambda i: (i // tiles_per_seq, 0, i % tiles_per_seq)),
        ],
        out_shape=[
            jax.ShapeDtypeStruct((n, 3072), BF16),
            jax.ShapeDtypeStruct((n, 2304), F32),
            jax.ShapeDtypeStruct((n, 768), BF16),
            jax.ShapeDtypeStruct((n, 1024), BF16),
            jax.ShapeDtypeStruct((n, LANES), F32),
            jax.ShapeDtypeStruct((batch, C_ROWS, seq), F32),
        ],
        scratch_shapes=[pltpu.VMEM((1, LANES), F32)],
        compiler_params=pltpu.CompilerParams(
            dimension_semantics=("arbitrary",), vmem_limit_bytes=VMEM_LIMIT),
        name="in_proj",
    )(x2, g, w_main, w_fl, b_fl, cos_t, sina_t, sinb_t)


def _fox_kernel(q_ref, k_ref, v_ref, g_ref, ccol_ref, crow_ref, o_ref, *, tq):
    i = pl.program_id(1)
    tk = tq
    lane = lax.broadcasted_iota(jnp.int32, (tq, LANES), 1)
    lo = lane < HEAD_DIM
    row = lax.broadcasted_iota(jnp.int32, (tq, tk), 0)
    col = lax.broadcasted_iota(jnp.int32, (tq, tk), 1)
    causal = row >= col

    for hp in range(HEAD_PAIRS):
        ls = slice(LANES * hp, LANES * (hp + 1))
        q2 = q_ref[0, :, ls]
        outs = []
        for hh in range(2):
            h = 2 * hp + hh
            qh = jnp.where(lo if hh == 0 else jnp.logical_not(lo), q2, jnp.zeros_like(q2))
            cq = ccol_ref[0, :, h:h + 1]

            def step(j, carry, masked, qh=qh, cq=cq, h=h, ls=ls):
                m, l, acc = carry
                off = pl.multiple_of(j * tk, tk)
                kt = k_ref[0, pl.ds(off, tk), ls]
                vt = v_ref[0, pl.ds(off, tk), ls]
                s = lax.dot_general(qh, kt, (((1,), (1,)), ((), ())), preferred_element_type=F32)
                s = s + cq - crow_ref[0, h:h + 1, pl.ds(off, tk)]
                if masked:
                    s = jnp.where(causal, s, NEG_INF)
                m_new = jnp.maximum(m, jnp.max(s, axis=-1, keepdims=True))
                alpha = jnp.exp(m - m_new)
                p = jnp.exp(s - m_new)
                l = alpha * l + jnp.sum(p, axis=-1, keepdims=True)
                acc = alpha * acc + jnp.dot(p.astype(BF16), vt, preferred_element_type=F32)
                return m_new, l, acc

            init = (jnp.full((tq, 1), NEG_INF, F32), jnp.zeros((tq, 1), F32),
                    jnp.zeros((tq, LANES), F32))
            carry = lax.fori_loop(0, i, functools.partial(step, masked=False), init)
            _, l, acc = step(i, carry, True)
            outs.append(acc / l)
        o2 = jnp.where(lo, outs[0], outs[1])
        o_ref[0, :, ls] = (o2 * g_ref[0, :, ls].astype(F32)).astype(BF16)


def _fox(of3, ccol3, crow, *, tq=256):
    batch, seq, _ = of3.shape
    kernel = functools.partial(_fox_kernel, tq=tq)
    return pl.pallas_call(
        kernel,
        grid=(batch, seq // tq),
        in_specs=[
            pl.BlockSpec((1, tq, FOX_W), lambda b, i: (b, i, 0)),
            pl.BlockSpec((1, seq, FOX_W), lambda b, i: (b, 0, 1)),
            pl.BlockSpec((1, seq, FOX_W), lambda b, i: (b, 0, 2)),
            pl.BlockSpec((1, tq, FOX_W), lambda b, i: (b, i, 3)),
            pl.BlockSpec((1, tq, LANES), lambda b, i: (b, i, 0)),
            pl.BlockSpec((1, C_ROWS, seq), lambda b, i: (b, 0, 0)),
        ],
        out_specs=pl.BlockSpec((1, tq, FOX_W), lambda b, i: (b, i, 0)),
        out_shape=jax.ShapeDtypeStruct((batch, seq, FOX_W), BF16),
        compiler_params=pltpu.CompilerParams(
            dimension_semantics=("arbitrary", "arbitrary"), vmem_limit_bytes=VMEM_LIMIT),
        name="fox_attn",
    )(of3, of3, of3, of3, ccol3, crow)


def _dil_kernel(q_ref, k_ref, v_ref, g_ref, o_ref, q0, q1, kd, v0, v1, nd_ref, m_ref, *, seq):
    n_blocks = seq // BLOCK
    lane = lax.broadcasted_iota(jnp.int32, (BLOCK, LANES), 1)
    lo = lane < HEAD_DIM
    row2 = lax.broadcasted_iota(jnp.int32, (BLOCK, 2 * BLOCK), 0)
    col2 = lax.broadcasted_iota(jnp.int32, (BLOCK, 2 * BLOCK), 1)
    band = jnp.logical_and(col2 >= row2, col2 <= row2 + BLOCK)
    bias_band = jnp.where(band, 0.0, NEG_INF)
    bias_first = jnp.where(jnp.logical_and(band, col2 >= BLOCK), 0.0, NEG_INF)

    kd[0:BLOCK, :] = jnp.zeros((BLOCK, LANES), BF16)
    v0[0:BLOCK, :] = jnp.zeros((BLOCK, LANES), BF16)
    v1[0:BLOCK, :] = jnp.zeros((BLOCK, LANES), BF16)

    for pi, (window, d) in enumerate(DILATIONS):
        assert window // d == BLOCK
        length = seq // d
        nb = length // BLOCK
        lo_l = lax.broadcasted_iota(jnp.int32, (length, LANES), 1) < HEAD_DIM
        for r in range(d):
            qdst = slice(r * length, (r + 1) * length)
            kdst = slice(BLOCK + r * length, BLOCK + (r + 1) * length)
            qf = _rows(q_ref.at[0], r, length, d).astype(BF16)
            q0[qdst, :] = jnp.where(lo_l, qf, jnp.zeros_like(qf))
            q1[qdst, :] = jnp.where(lo_l, jnp.zeros_like(qf), qf)
            kd[kdst, :] = _rows(k_ref.at[0], r, length, d).astype(BF16)
            vf = _rows(v_ref.at[0], r, length, d).astype(BF16)
            v0[kdst, :] = jnp.where(lo_l, vf, jnp.ones_like(vf))
            v1[kdst, :] = jnp.where(lo_l, jnp.ones_like(vf), vf)

        def body(t, _, pi=pi, d=d, nb=nb):
            base = pl.multiple_of(t * BLOCK, BLOCK)
            if nb == 1:
                bias = bias_first
                nat_start = t
            else:
                r = t // nb
                n = t - r * nb
                bias = jnp.where(n == 0, bias_first, bias_band)
                nat_start = base if d == 1 else n * (BLOCK * d) + r
            kk = kd[pl.ds(base, 2 * BLOCK), :]
            for hh, (qs, vs) in enumerate(((q0, v0), (q1, v1))):
                s = lax.dot_general(qs[pl.ds(base, BLOCK), :], kk, (((1,), (1,)), ((), ())),
                                    preferred_element_type=F32) + bias
                mx = jnp.max(s, axis=-1, keepdims=True)
                e = jnp.exp(s - mx)
                nd = jnp.dot(e.astype(BF16), vs[pl.ds(base, 2 * BLOCK), :],
                             preferred_element_type=F32)
                _set_rows(nd_ref.at[pi, hh], nat_start, BLOCK, d, nd)
                _set_rows(m_ref.at[pi, hh], nat_start, BLOCK, d, jnp.broadcast_to(mx, (BLOCK, LANES)))
            return 0

        lax.fori_loop(0, n_blocks, body, 0, unroll=4)

    def merge(c, _):
        rs = pl.ds(pl.multiple_of(c * BLOCK, BLOCK), BLOCK)
        outs = []
        for hh in range(2):
            ms = [m_ref[p, hh, rs, :] for p in range(len(DILATIONS))]
            m_all = jnp.maximum(jnp.maximum(ms[0], ms[1]), ms[2])
            tot = nd_ref[0, hh, rs, :] * jnp.exp(ms[0] - m_all)
            for p in range(1, len(DILATIONS)):
                tot = tot + nd_ref[p, hh, rs, :] * jnp.exp(ms[p] - m_all)
            outs.append(tot / pltpu.roll(tot, HEAD_DIM, 1))
        o = jnp.where(lo, outs[0], outs[1]) * g_ref[0, rs, :].astype(F32)
        o_ref[0, rs, :] = o.astype(BF16)
        return 0

    lax.fori_loop(0, n_blocks, merge, 0)


def _dilated(od3, odg3):
    batch, seq, _ = od3.shape
    kernel = functools.partial(_dil_kernel, seq=seq)
    n_pat = len(DILATIONS)
    return pl.pallas_call(
        kernel,
        grid=(batch, HEAD_PAIRS),
        in_specs=[
            pl.BlockSpec((1, seq, LANES), lambda b, p: (b, 0, p)),
            pl.BlockSpec((1, seq, LANES), lambda b, p: (b, 0, HEAD_PAIRS + p)),
            pl.BlockSpec((1, seq, LANES), lambda b, p: (b, 0, 2 * HEAD_PAIRS + p)),
            pl.BlockSpec((1, seq, LANES), lambda b, p: (b, 0, p)),
        ],
        out_specs=pl.BlockSpec((1, seq, LANES), lambda b, p: (b, 0, p)),
        out_shape=jax.ShapeDtypeStruct((batch, seq, DIL_W), BF16),
        scratch_shapes=[
            pltpu.VMEM((seq, LANES), BF16),
            pltpu.VMEM((seq, LANES), BF16),
            pltpu.VMEM((seq + BLOCK, LANES), BF16),
            pltpu.VMEM((seq + BLOCK, LANES), BF16),
            pltpu.VMEM((seq + BLOCK, LANES), BF16),
            pltpu.VMEM((n_pat, 2, seq, LANES), F32),
            pltpu.VMEM((n_pat, 2, seq, LANES), F32),
        ],
        compiler_params=pltpu.CompilerParams(
            dimension_semantics=("arbitrary", "arbitrary"), vmem_limit_bytes=VMEM_LIMIT),
        name="dilated_attn",
    )(od3, od3, od3, odg3)


def _mem_kv_kernel(mem_ref, g_ref, w_ref, o_ref):
    x = mem_ref[0]
    ms = jnp.mean(x * x, axis=-1, keepdims=True)
    h = (x * lax.rsqrt(ms + RMS_EPS) * g_ref[...]).astype(BF16)
    o_ref[0] = jnp.dot(h, w_ref[...], preferred_element_type=F32).astype(BF16)


def _mem_kv(mem, g, w):
    batch, mlen, _ = mem.shape
    return pl.pallas_call(
        _mem_kv_kernel,
        grid=(batch,),
        in_specs=[
            pl.BlockSpec((1, mlen, D_MODEL), lambda b: (b, 0, 0)),
            pl.BlockSpec((1, D_MODEL), lambda b: (0, 0)),
            pl.BlockSpec((D_MODEL, 2 * MEM_W), lambda b: (0, 0)),
        ],
        out_specs=pl.BlockSpec((1, mlen, 2 * MEM_W), lambda b: (b, 0, 0)),
        out_shape=jax.ShapeDtypeStruct((batch, mlen, 2 * MEM_W), BF16),
        compiler_params=pltpu.CompilerParams(
            dimension_semantics=("arbitrary",), vmem_limit_bytes=VMEM_LIMIT),
        name="mem_kv",
    )(mem, g, w)


def _mem_attn_kernel(q_ref, g_ref, mk_ref, mv_ref, o_ref):
    for h in range(MEM_HEADS):
        ls = slice(MEM_HEAD_DIM * h, MEM_HEAD_DIM * (h + 1))
        s = lax.dot_general(q_ref[0, :, ls], mk_ref[0, :, ls], (((1,), (1,)), ((), ())),
                            preferred_element_type=F32) * MEM_SCALE
        mx = jnp.max(s, axis=-1, keepdims=True)
        e = jnp.exp(s - mx)
        den = jnp.sum(e, axis=-1, keepdims=True)
        o = jnp.dot(e.astype(BF16), mv_ref[0, :, ls], preferred_element_type=F32) / den
        o_ref[0, :, ls] = (o * g_ref[0, :, ls].astype(F32)).astype(BF16)


def _mem_attn(om3, mkv, *, tq=512):
    batch, seq, _ = om3.shape
    mlen = mkv.shape[1]
    return pl.pallas_call(
        _mem_attn_kernel,
        grid=(batch, seq // tq),
        in_specs=[
            pl.BlockSpec((1, tq, MEM_W), lambda b, i: (b, i, 0)),
            pl.BlockSpec((1, tq, MEM_W), lambda b, i: (b, i, 1)),
            pl.BlockSpec((1, mlen, MEM_W), lambda b, i: (b, 0, 0)),
            pl.BlockSpec((1, mlen, MEM_W), lambda b, i: (b, 0, 1)),
        ],
        out_specs=pl.BlockSpec((1, tq, MEM_W), lambda b, i: (b, i, 0)),
        out_shape=jax.ShapeDtypeStruct((batch, seq, MEM_W), BF16),
        compiler_params=pltpu.CompilerParams(
            dimension_semantics=("arbitrary", "arbitrary"), vmem_limit_bytes=VMEM_LIMIT),
        name="mem_attn",
    )(om3, om3, mkv, mkv)


def _out_proj_kernel(x_ref, yf_ref, yd_ref, ym_ref, w_ref, g_ref, o_ref, *, final_norm):
    z = x_ref[...]
    z = z + jnp.dot(yf_ref[...], w_ref[0:FOX_W, :], preferred_element_type=F32)
    z = z + jnp.dot(yd_ref[...], w_ref[FOX_W:FOX_W + DIL_W, :], preferred_element_type=F32)
    z = z + jnp.dot(ym_ref[...], w_ref[FOX_W + DIL_W:MIX_W, :], preferred_element_type=F32)
    if final_norm:
        ms = jnp.mean(z * z, axis=-1, keepdims=True)
        z = z * lax.rsqrt(ms + RMS_EPS) * g_ref[...]
    o_ref[...] = z


def _out_proj(x2, yf, yd, ym, w, g, *, final_norm, tm=256):
    n = x2.shape[0]
    row = lambda i: (i, 0)
    const = lambda i: (0, 0)
    kernel = functools.partial(_out_proj_kernel, final_norm=final_norm)
    return pl.pallas_call(
        kernel,
        grid=(n // tm,),
        in_specs=[
            pl.BlockSpec((tm, D_MODEL), row),
            pl.BlockSpec((tm, FOX_W), row),
            pl.BlockSpec((tm, DIL_W), row),
            pl.BlockSpec((tm, MEM_W), row),
            pl.BlockSpec((MIX_W, D_MODEL), const),
            pl.BlockSpec((1, D_MODEL), const),
        ],
        out_specs=pl.BlockSpec((tm, D_MODEL), row),
        out_shape=jax.ShapeDtypeStruct((n, D_MODEL), F32),
        compiler_params=pltpu.CompilerParams(
            dimension_semantics=("arbitrary",), vmem_limit_bytes=VMEM_LIMIT),
        name="out_proj",
    )(x2, yf, yd, ym, w, g)


def _rope_tables(seq):
    half = ROPE_DIM // 2
    pos = jnp.arange(seq, dtype=F32)
    inv_freq = 1.0 / (ROPE_THETA ** (jnp.arange(0, ROPE_DIM, 2, dtype=F32) / ROPE_DIM))
    ang = pos[:, None] * inv_freq[None, :]
    cos = jnp.cos(ang)
    sin = jnp.sin(ang)
    ones = jnp.ones((seq, HEAD_DIM - ROPE_DIM), F32)
    zeros = jnp.zeros((seq, HEAD_DIM - ROPE_DIM), F32)
    zh = jnp.zeros((seq, half), F32)
    cos_h = jnp.concatenate([cos, cos, ones], axis=1)
    sina_h = jnp.concatenate([zh, sin, zeros], axis=1)
    sinb_h = jnp.concatenate([-sin, zh, zeros], axis=1)
    tile = lambda t: jnp.concatenate([t, t], axis=1)
    return tile(cos_h), tile(sina_h), tile(sinb_h)


def _split_w_in(w):
    sizes = [FOX_W] * 4 + [FOX_HEADS] + [DIL_W] * 4 + [MEM_W] * 2
    offs = [0]
    for s in sizes:
        offs.append(offs[-1] + s)
    cols = [w[:, offs[k]:offs[k + 1]] for k in range(len(sizes))]
    fq, fk, fv, fg, flog, dq, dk, dv, dg, mq, mg = cols
    w_main = jnp.concatenate([fq, fk, fv, fg, dq, dk, dv, dg, mq, mg], axis=1).astype(BF16)
    w_fl = jnp.pad(flog, ((0, 0), (0, LANES - FOX_HEADS))).astype(BF16)
    return w_main, w_fl


def kernel(x, mem, norm_g, w_in, b_forget, mem_norm_g, w_mem_kv, w_out, final_norm_g):
    batch, seq, _ = x.shape
    depth = norm_g.shape[0]
    cos_t, sina_t, sinb_t = _rope_tables(seq)
    x2 = x.reshape(batch * seq, D_MODEL)
    for l in range(depth):
        w_main, w_fl = _split_w_in(w_in[l])
        b_fl = jnp.pad(b_forget[l], (0, LANES - FOX_HEADS)).reshape(1, LANES)
        of, od, odg, om, ccol, crow = _in_proj(
            x2, norm_g[l].reshape(1, D_MODEL), w_main, w_fl, b_fl, cos_t, sina_t, sinb_t, seq=seq)
        yf = _fox(of.reshape(batch, seq, 3072), ccol.reshape(batch, seq, LANES), crow)
        yd = _dilated(od.reshape(batch, seq, 2304), odg.reshape(batch, seq, DIL_W))
        mkv = _mem_kv(mem, mem_norm_g[l].reshape(1, D_MODEL), w_mem_kv[l].astype(BF16))
        ym = _mem_attn(om.reshape(batch, seq, 1024), mkv)
        last = l == depth - 1
        x2 = _out_proj(x2, yf.reshape(batch * seq, FOX_W), yd.reshape(batch * seq, DIL_W),
                       ym.reshape(batch * seq, MEM_W), w_out[l].astype(BF16),
                       final_norm_g.reshape(1, D_MODEL), final_norm=last)
    return x2.reshape(batch, seq, D_MODEL)
```

```python
import functools
import math

import jax
import jax.numpy as jnp
from jax import lax
from jax.experimental import pallas as pl
from jax.experimental.pallas import tpu as pltpu

F32 = jnp.float32
BF16 = jnp.bfloat16

D_MODEL = 1024
HEAD_DIM = 64
FOX_HEADS = 12
DIL_HEADS = 12
MEM_HEADS = 4
MEM_HEAD_DIM = 128
FOX_W = FOX_HEADS * HEAD_DIM
DIL_W = DIL_HEADS * HEAD_DIM
MEM_W = MEM_HEADS * MEM_HEAD_DIM
MIX_W = FOX_W + DIL_W + MEM_W
DILATIONS = ((128, 1), (512, 4), (2048, 16))
BLOCK = 128
ROPE_THETA = 500000.0
ROPE_DIM = HEAD_DIM // 4
RMS_EPS = 1e-6
NEG_INF = -1e30
QK_SCALE = 1.0 / math.sqrt(HEAD_DIM)
MEM_SCALE = 1.0 / math.sqrt(MEM_HEAD_DIM)

LANES = 128
HEAD_PAIRS = FOX_HEADS // 2
FOX_AUG_W = FOX_HEADS * LANES
FOX_OUT_W = 3 * FOX_AUG_W + FOX_W
FOX_ROWS = 128
FOX_KEYS = 256
FOX_LOOKAHEAD = 2
DIL_GROUP = 16
DIL_LOOKAHEAD = 2

_SEG = dict(fq=0, fk=768, fv=1536, fg=2304, dq=3072, dk=3840, dv=4608, dg=5376, mq=6144, mg=6656)
MAIN_W = 7168

VMEM_LIMIT = 56 * 1024 * 1024


def _silu(r):
    return r * (1.0 / (1.0 + jnp.exp(-r)))


def _rows(ref, start, size, stride):
    if stride == 1:
        return ref[pl.ds(start, size), :]
    return ref[pl.ds(start, size, stride=stride), :]


def _set_rows(ref, start, size, stride, val):
    if stride == 1:
        ref[pl.ds(start, size), :] = val
    else:
        ref[pl.ds(start, size, stride=stride), :] = val


def _in_proj_kernel(x_ref, g_ref, w_ref, wfl_ref, bf_ref, cos_ref, sina_ref, sinb_ref,
                    of_ref, od_ref, odg_ref, om_ref, carry_ref,
                    *, tm, tiles_per_seq):
    i = pl.program_id(0)
    x = x_ref[...]
    ms = jnp.mean(x * x, axis=-1, keepdims=True)
    h = (x * lax.rsqrt(ms + RMS_EPS) * g_ref[...]).astype(BF16)

    def seg(name, width):
        off = _SEG[name]
        return jnp.dot(h, w_ref[:, off:off + width], preferred_element_type=F32)

    fl = jnp.dot(h, wfl_ref[...], preferred_element_type=F32) + bf_ref[...]
    logf = jnp.minimum(fl, 0.0) - jnp.log1p(jnp.exp(-jnp.abs(fl)))
    a1 = logf.astype(BF16)
    r1 = logf - a1.astype(F32)
    a2 = r1.astype(BF16)
    a3 = (r1 - a2.astype(F32)).astype(BF16)
    row = lax.broadcasted_iota(jnp.int32, (tm, tm), 0)
    col = lax.broadcasted_iota(jnp.int32, (tm, tm), 1)
    tri = jnp.where(row >= col, 1.0, 0.0).astype(BF16)

    @pl.when(i % tiles_per_seq == 0)
    def _():
        carry_ref[...] = jnp.zeros_like(carry_ref)

    c = (jnp.dot(tri, a1, preferred_element_type=F32)
         + jnp.dot(tri, a2, preferred_element_type=F32)
         + jnp.dot(tri, a3, preferred_element_type=F32)) + carry_ref[...]
    carry_ref[...] = c[tm - 1:tm, :]

    c1 = c.astype(BF16).astype(F32)
    c2 = (c - c1).astype(BF16).astype(F32)
    c3 = (c - c1 - c2).astype(BF16).astype(F32)
    fq = seg("fq", 768) * QK_SCALE
    fk = seg("fk", 768)
    fv = seg("fv", 768)
    lane = lax.broadcasted_iota(jnp.int32, (tm, LANES), 1)
    for hd in range(FOX_HEADS):
        pair = slice(LANES * (hd // 2), LANES * (hd // 2 + 1))
        own = (lane < HEAD_DIM) if hd % 2 == 0 else (lane >= HEAD_DIM)
        base = HEAD_DIM if hd % 2 == 0 else 0
        b1, b2, b3 = (jnp.broadcast_to(t[:, hd:hd + 1], (tm, LANES)) for t in (c1, c2, c3))
        ones_q = jnp.where(jnp.logical_and(lane >= base, lane < base + 3), 1.0, 0.0)
        ones_k = jnp.where(jnp.logical_and(lane >= base + 3, lane < base + 6), 1.0, 0.0)
        q_bias = jnp.where(lane == base + 3, b1, jnp.where(lane == base + 4, b2,
                           jnp.where(lane == base + 5, b3, ones_q)))
        k_bias = jnp.where(lane == base, -b1, jnp.where(lane == base + 1, -b2,
                           jnp.where(lane == base + 2, -b3, ones_k)))
        dst = LANES * hd
        of_ref[:, dst:dst + LANES] = jnp.where(own, fq[:, pair], q_bias).astype(BF16)
        of_ref[:, FOX_AUG_W + dst:FOX_AUG_W + dst + LANES] = \
            jnp.where(own, fk[:, pair], k_bias).astype(BF16)
        of_ref[:, 2 * FOX_AUG_W + dst:2 * FOX_AUG_W + dst + LANES] = \
            jnp.where(own, fv[:, pair], 1.0).astype(BF16)
    of_ref[:, 3 * FOX_AUG_W:3 * FOX_AUG_W + FOX_W] = _silu(seg("fg", 768)).astype(BF16)

    cos = cos_ref[...]
    sina = sina_ref[...]
    sinb = sinb_ref[...]

    def rope_store(r, dst_off, scale):
        for j in range(DIL_W // LANES):
            c = r[:, LANES * j:LANES * (j + 1)]
            rot = c * cos + pltpu.roll(c, ROPE_DIM // 2, 1) * sina \
                + pltpu.roll(c, LANES - ROPE_DIM // 2, 1) * sinb
            if scale != 1.0:
                rot = rot * scale
            od_ref[:, dst_off + LANES * j:dst_off + LANES * (j + 1)] = rot

    rope_store(seg("dq", 768), 0, QK_SCALE)
    rope_store(seg("dk", 768), 768, 1.0)
    od_ref[:, 1536:2304] = seg("dv", 768)
    odg_ref[...] = _silu(seg("dg", 768)).astype(BF16)

    om_ref[:, 0:512] = seg("mq", 512).astype(BF16)
    om_ref[:, 512:1024] = _silu(seg("mg", 512)).astype(BF16)


def _in_proj(x2, g, w_main, w_fl, b_fl, cos_t, sina_t, sinb_t, *, seq, tm=256):
    n = x2.shape[0]
    tiles_per_seq = seq // tm
    const = lambda i: (0, 0)
    row = lambda i: (i, 0)
    pos = lambda i: (i % tiles_per_seq, 0)
    kernel = functools.partial(_in_proj_kernel, tm=tm, tiles_per_seq=tiles_per_seq)
    return pl.pallas_call(
        kernel,
        grid=(n // tm,),
        in_specs=[
            pl.BlockSpec((tm, D_MODEL), row),
            pl.BlockSpec((1, D_MODEL), const),
            pl.BlockSpec((D_MODEL, MAIN_W), const),
            pl.BlockSpec((D_MODEL, LANES), const),
            pl.BlockSpec((1, LANES), const),
            pl.BlockSpec((tm, LANES), pos),
            pl.BlockSpec((tm, LANES), pos),
            pl.BlockSpec((tm, LANES), pos),
        ],
        out_specs=[
            pl.BlockSpec((tm, FOX_OUT_W), row),
            pl.BlockSpec((tm, 2304), row),
            pl.BlockSpec((tm, 768), row),
            pl.BlockSpec((tm, 1024), row),
        ],
        out_shape=[
            jax.ShapeDtypeStruct((n, FOX_OUT_W), BF16),
            jax.ShapeDtypeStruct((n, 2304), F32),
            jax.ShapeDtypeStruct((n, 768), BF16),
            jax.ShapeDtypeStruct((n, 1024), BF16),
        ],
        scratch_shapes=[pltpu.VMEM((1, LANES), F32)],
        compiler_params=pltpu.CompilerParams(
            dimension_semantics=("arbitrary",), vmem_limit_bytes=VMEM_LIMIT),
        name="in_proj",
    )(x2, g, w_main, w_fl, b_fl, cos_t, sina_t, sinb_t)


def _fox_kernel(q_ref, k_ref, v_ref, g_ref, o_ref, acc_ref, m_ref, *, tq):
    i = pl.program_id(1)
    tk = FOX_KEYS
    row_chunks = tq // FOX_ROWS
    lane = lax.broadcasted_iota(jnp.int32, (FOX_ROWS, LANES), 1)
    lo = lane < HEAD_DIM
    row = lax.broadcasted_iota(jnp.int32, (FOX_ROWS, tk), 0)
    col = lax.broadcasted_iota(jnp.int32, (FOX_ROWS, tk), 1)
    bias_lo = jnp.where(row >= col, 0.0, NEG_INF)
    bias_hi = jnp.where(row + FOX_ROWS >= col, 0.0, NEG_INF)

    m_ref[...] = jnp.full(m_ref.shape, NEG_INF, F32)
    acc_ref[...] = jnp.zeros(acc_ref.shape, F32)

    def run_chains(chains):
        def scores(c):
            hd, rc, koff, klen, _ = chains[c]
            ls = slice(LANES * hd, LANES * (hd + 1))
            rs = slice(FOX_ROWS * rc, FOX_ROWS * (rc + 1))
            return lax.dot_general(q_ref[0, rs, ls], k_ref[0, pl.ds(koff, klen), ls],
                                   (((1,), (1,)), ((), ())), preferred_element_type=F32)

        pending = [scores(c) for c in range(FOX_LOOKAHEAD)]
        for c, (hd, rc, koff, klen, bias) in enumerate(chains):
            if c + FOX_LOOKAHEAD < len(chains):
                pending.append(scores(c + FOX_LOOKAHEAD))
            s = pending.pop(0)
            ls = slice(LANES * hd, LANES * (hd + 1))
            rs = slice(FOX_ROWS * rc, FOX_ROWS * (rc + 1))
            if bias is not None:
                s = s + bias
            m_old = m_ref[hd, rs, :]
            m_new = jnp.maximum(
                m_old, jnp.broadcast_to(jnp.max(s, axis=-1, keepdims=True), (FOX_ROWS, LANES)))
            alpha = jnp.exp(m_old - m_new)
            p = jnp.concatenate(
                [jnp.exp(s[:, LANES * t:LANES * (t + 1)] - m_new) for t in range(klen // LANES)],
                axis=1)
            acc_ref[hd, rs, :] = alpha * acc_ref[hd, rs, :] + jnp.dot(
                p.astype(BF16), v_ref[0, pl.ds(koff, klen), ls], preferred_element_type=F32)
            m_ref[hd, rs, :] = m_new

    def body(j, _):
        koff = pl.multiple_of(j * tk, tk)
        run_chains([(hd, rc, koff, tk, None)
                    for hd in range(FOX_HEADS) for rc in range(row_chunks)])
        return 0

    def diag_body(t, _, kt, masked):
        koff = pl.multiple_of(t * tq + kt * tk, tk)
        if masked:
            chains = [(hd, rc, koff, tk, bias_lo if rc % 2 == 0 else bias_hi)
                      for hd in range(FOX_HEADS) for rc in (2 * kt, 2 * kt + 1)]
        else:
            chains = [(hd, rc, koff, tk, None)
                      for hd in range(FOX_HEADS) for rc in range(2 * kt + 2, row_chunks)]
        run_chains(chains)
        return 0

    lax.fori_loop(0, i * (tq // tk), body, 0)
    for kt in range(tq // tk):
        lax.fori_loop(i, i + 1, functools.partial(diag_body, kt=kt, masked=True), 0)
        if 2 * kt + 2 < row_chunks:
            lax.fori_loop(i, i + 1, functools.partial(diag_body, kt=kt, masked=False), 0)

    for hp in range(HEAD_PAIRS):
        ls = slice(LANES * hp, LANES * (hp + 1))
        for rc in range(row_chunks):
            rs = slice(FOX_ROWS * rc, FOX_ROWS * (rc + 1))
            a0 = acc_ref[2 * hp, rs, :]
            a1 = acc_ref[2 * hp + 1, rs, :]
            o2 = jnp.where(lo, a0 / pltpu.roll(a0, HEAD_DIM, 1), a1 / pltpu.roll(a1, HEAD_DIM, 1))
            o_ref[0, rs, ls] = (o2 * g_ref[0, rs, ls].astype(F32)).astype(BF16)


def _fox(of3, *, tq=512):
    batch, seq, _ = of3.shape
    kernel = functools.partial(_fox_kernel, tq=tq)
    gate_blk = 3 * FOX_AUG_W // FOX_W
    return pl.pallas_call(
        kernel,
        grid=(batch, seq // tq),
        in_specs=[
            pl.BlockSpec((1, tq, FOX_AUG_W), lambda b, i: (b, i, 0)),
            pl.BlockSpec((1, seq, FOX_AUG_W), lambda b, i: (b, 0, 1)),
            pl.BlockSpec((1, seq, FOX_AUG_W), lambda b, i: (b, 0, 2)),
            pl.BlockSpec((1, tq, FOX_W), lambda b, i: (b, i, gate_blk)),
        ],
        out_specs=pl.BlockSpec((1, tq, FOX_W), lambda b, i: (b, i, 0)),
        out_shape=jax.ShapeDtypeStruct((batch, seq, FOX_W), BF16),
        scratch_shapes=[
            pltpu.VMEM((FOX_HEADS, tq, LANES), F32),
            pltpu.VMEM((FOX_HEADS, tq, LANES), F32),
        ],
        compiler_params=pltpu.CompilerParams(
            dimension_semantics=("arbitrary", "arbitrary"), vmem_limit_bytes=VMEM_LIMIT),
        name="fox_attn",
    )(of3, of3, of3, of3)


def _dil_kernel(q_ref, k_ref, v_ref, g_ref, o_ref, q0, q1, kd, v0, v1, nd_ref, m_ref, *, seq):
    n_blocks = seq // BLOCK
    lane = lax.broadcasted_iota(jnp.int32, (BLOCK, LANES), 1)
    lo = lane < HEAD_DIM
    row2 = lax.broadcasted_iota(jnp.int32, (BLOCK, 2 * BLOCK), 0)
    col2 = lax.broadcasted_iota(jnp.int32, (BLOCK, 2 * BLOCK), 1)
    band = jnp.logical_and(col2 >= row2, col2 <= row2 + BLOCK)
    bias_band = jnp.where(band, 0.0, NEG_INF)
    bias_first = jnp.where(jnp.logical_and(band, col2 >= BLOCK), 0.0, NEG_INF)

    kd[0:BLOCK, :] = jnp.zeros((BLOCK, LANES), BF16)
    v0[0:BLOCK, :] = jnp.zeros((BLOCK, LANES), BF16)
    v1[0:BLOCK, :] = jnp.zeros((BLOCK, LANES), BF16)

    for slot, (window, d) in enumerate(DILATIONS):
        assert window // d == BLOCK
        length = seq // d
        nb = length // BLOCK
        lo_l = lax.broadcasted_iota(jnp.int32, (length, LANES), 1) < HEAD_DIM
        for r in range(d):
            qdst = slice(r * length, (r + 1) * length)
            kdst = slice(BLOCK + r * length, BLOCK + (r + 1) * length)
            qf = _rows(q_ref.at[0], r, length, d).astype(BF16)
            q0[qdst, :] = jnp.where(lo_l, qf, jnp.zeros_like(qf))
            q1[qdst, :] = jnp.where(lo_l, jnp.zeros_like(qf), qf)
            kd[kdst, :] = _rows(k_ref.at[0], r, length, d).astype(BF16)
            vf = _rows(v_ref.at[0], r, length, d).astype(BF16)
            v0[kdst, :] = jnp.where(lo_l, vf, jnp.ones_like(vf))
            v1[kdst, :] = jnp.where(lo_l, jnp.ones_like(vf), vf)

        def body(g, _, slot=slot, d=d, nb=nb):
            chains = [(u, hh) for u in range(DIL_GROUP) for hh in range(2)]
            qv = ((q0, v0), (q1, v1))

            def base_of(u):
                return pl.multiple_of((g * DIL_GROUP + u) * BLOCK, BLOCK)

            def scores(c):
                u, hh = chains[c]
                base = base_of(u)
                return lax.dot_general(qv[hh][0][pl.ds(base, BLOCK), :],
                                       kd[pl.ds(base, 2 * BLOCK), :], (((1,), (1,)), ((), ())),
                                       preferred_element_type=F32)

            pending = [scores(c) for c in range(DIL_LOOKAHEAD)]
            for c, (u, hh) in enumerate(chains):
                if c + DIL_LOOKAHEAD < len(chains):
                    pending.append(scores(c + DIL_LOOKAHEAD))
                t = g * DIL_GROUP + u
                base = base_of(u)
                if nb == 1:
                    bias = bias_first
                    nat_start = t
                else:
                    r = t // nb
                    n = t - r * nb
                    bias = jnp.where(n == 0, bias_first, bias_band)
                    nat_start = base if d == 1 else n * (BLOCK * d) + r
                s = pending.pop(0) + bias
                mx = jnp.broadcast_to(jnp.max(s, axis=-1, keepdims=True), (BLOCK, LANES))
                e = jnp.concatenate(
                    [jnp.exp(s[:, LANES * t2:LANES * (t2 + 1)] - mx) for t2 in range(2)], axis=1)
                nd = jnp.dot(e.astype(BF16), qv[hh][1][pl.ds(base, 2 * BLOCK), :],
                             preferred_element_type=F32)
                _set_rows(nd_ref.at[slot, hh], nat_start, BLOCK, d, nd)
                _set_rows(m_ref.at[slot, hh], nat_start, BLOCK, d, mx)
            return 0

        lax.fori_loop(0, n_blocks // DIL_GROUP, body, 0)

    def merge(c, _):
        rs = pl.ds(pl.multiple_of(c * BLOCK, BLOCK), BLOCK)
        outs = []
        for hh in range(2):
            ms = [m_ref[p, hh, rs, :] for p in range(len(DILATIONS))]
            m_all = jnp.maximum(jnp.maximum(ms[0], ms[1]), ms[2])
            tot = nd_ref[0, hh, rs, :] * jnp.exp(ms[0] - m_all)
            for p in range(1, len(DILATIONS)):
                tot = tot + nd_ref[p, hh, rs, :] * jnp.exp(ms[p] - m_all)
            outs.append(tot / pltpu.roll(tot, HEAD_DIM, 1))
        o = jnp.where(lo, outs[0], outs[1]) * g_ref[0, rs, :].astype(F32)
        o_ref[0, rs, :] = o.astype(BF16)
        return 0

    lax.fori_loop(0, n_blocks, merge, 0, unroll=2)


def _dilated(od3, odg3):
    batch, seq, _ = od3.shape
    kernel = functools.partial(_dil_kernel, seq=seq)
    n_pat = len(DILATIONS)
    return pl.pallas_call(
        kernel,
        grid=(batch, HEAD_PAIRS),
        in_specs=[
            pl.BlockSpec((1, seq, LANES), lambda b, p: (b, 0, p)),
            pl.BlockSpec((1, seq, LANES), lambda b, p: (b, 0, HEAD_PAIRS + p)),
            pl.BlockSpec((1, seq, LANES), lambda b, p: (b, 0, 2 * HEAD_PAIRS + p)),
            pl.BlockSpec((1, seq, LANES), lambda b, p: (b, 0, p)),
        ],
        out_specs=pl.BlockSpec((1, seq, LANES), lambda b, p: (b, 0, p)),
        out_shape=jax.ShapeDtypeStruct((batch, seq, DIL_W), BF16),
        scratch_shapes=[
            pltpu.VMEM((seq, LANES), BF16),
            pltpu.VMEM((seq, LANES), BF16),
            pltpu.VMEM((seq + BLOCK, LANES), BF16),
            pltpu.VMEM((seq + BLOCK, LANES), BF16),
            pltpu.VMEM((seq + BLOCK, LANES), BF16),
            pltpu.VMEM((n_pat, 2, seq, LANES), F32),
            pltpu.VMEM((n_pat, 2, seq, LANES), F32),
        ],
        compiler_params=pltpu.CompilerParams(
            dimension_semantics=("arbitrary", "arbitrary"), vmem_limit_bytes=VMEM_LIMIT),
        name="dilated_attn",
    )(od3, od3, od3, odg3)


def _mem_kv_kernel(mem_ref, g_ref, w_ref, o_ref):
    x = mem_ref[0]
    ms = jnp.mean(x * x, axis=-1, keepdims=True)
    h = (x * lax.rsqrt(ms + RMS_EPS) * g_ref[...]).astype(BF16)
    o_ref[0] = jnp.dot(h, w_ref[...], preferred_element_type=F32).astype(BF16)


def _mem_kv(mem, g, w):
    batch, mlen, _ = mem.shape
    return pl.pallas_call(
        _mem_kv_kernel,
        grid=(batch,),
        in_specs=[
            pl.BlockSpec((1, mlen, D_MODEL), lambda b: (b, 0, 0)),
            pl.BlockSpec((1, D_MODEL), lambda b: (0, 0)),
            pl.BlockSpec((D_MODEL, 2 * MEM_W), lambda b: (0, 0)),
        ],
        out_specs=pl.BlockSpec((1, mlen, 2 * MEM_W), lambda b: (b, 0, 0)),
        out_shape=jax.ShapeDtypeStruct((batch, mlen, 2 * MEM_W), BF16),
        compiler_params=pltpu.CompilerParams(
            dimension_semantics=("arbitrary",), vmem_limit_bytes=VMEM_LIMIT),
        name="mem_kv",
    )(mem, g, w)


def _mem_attn_kernel(q_ref, g_ref, mk_ref, mv_ref, o_ref):
    tq = q_ref.shape[1]
    chains = [(h, rc) for h in range(MEM_HEADS) for rc in range(tq // BLOCK)]

    def scores(c):
        h, rc = chains[c]
        ls = slice(MEM_HEAD_DIM * h, MEM_HEAD_DIM * (h + 1))
        return lax.dot_general(q_ref[0, BLOCK * rc:BLOCK * (rc + 1), ls], mk_ref[0, :, ls],
                               (((1,), (1,)), ((), ())), preferred_element_type=F32)

    pending = [scores(c) for c in range(FOX_LOOKAHEAD)]
    for c, (h, rc) in enumerate(chains):
        if c + FOX_LOOKAHEAD < len(chains):
            pending.append(scores(c + FOX_LOOKAHEAD))
        ls = slice(MEM_HEAD_DIM * h, MEM_HEAD_DIM * (h + 1))
        rs = slice(BLOCK * rc, BLOCK * (rc + 1))
        s = pending.pop(0) * MEM_SCALE
        mx = jnp.max(s, axis=-1, keepdims=True)
        e = jnp.exp(s - mx)
        den = jnp.sum(e, axis=-1, keepdims=True)
        o = jnp.dot(e.astype(BF16), mv_ref[0, :, ls], preferred_element_type=F32) / den
        o_ref[0, rs, ls] = (o * g_ref[0, rs, ls].astype(F32)).astype(BF16)


def _mem_attn(om3, mkv, *, tq=512):
    batch, seq, _ = om3.shape
    mlen = mkv.shape[1]
    return pl.pallas_call(
        _mem_attn_kernel,
        grid=(batch, seq // tq),
        in_specs=[
            pl.BlockSpec((1, tq, MEM_W), lambda b, i: (b, i, 0)),
            pl.BlockSpec((1, tq, MEM_W), lambda b, i: (b, i, 1)),
            pl.BlockSpec((1, mlen, MEM_W), lambda b, i: (b, 0, 0)),
            pl.BlockSpec((1, mlen, MEM_W), lambda b, i: (b, 0, 1)),
        ],
        out_specs=pl.BlockSpec((1, tq, MEM_W), lambda b, i: (b, i, 0)),
        out_shape=jax.ShapeDtypeStruct((batch, seq, MEM_W), BF16),
        compiler_params=pltpu.CompilerParams(
            dimension_semantics=("arbitrary", "arbitrary"), vmem_limit_bytes=VMEM_LIMIT),
        name="mem_attn",
    )(om3, om3, mkv, mkv)


def _out_proj_kernel(x_ref, yf_ref, yd_ref, ym_ref, w_ref, g_ref, o_ref, *, final_norm):
    z = x_ref[...]
    z = z + jnp.dot(yf_ref[...], w_ref[0:FOX_W, :], preferred_element_type=F32)
    z = z + jnp.dot(yd_ref[...], w_ref[FOX_W:FOX_W + DIL_W, :], preferred_element_type=F32)
    z = z + jnp.dot(ym_ref[...], w_ref[FOX_W + DIL_W:MIX_W, :], preferred_element_type=F32)
    if final_norm:
        ms = jnp.mean(z * z, axis=-1, keepdims=True)
        z = z * lax.rsqrt(ms + RMS_EPS) * g_ref[...]
    o_ref[...] = z


def _out_proj(x2, yf, yd, ym, w, g, *, final_norm, tm=512):
    n = x2.shape[0]
    row = lambda i: (i, 0)
    const = lambda i: (0, 0)
    kernel = functools.partial(_out_proj_kernel, final_norm=final_norm)
    return pl.pallas_call(
        kernel,
        grid=(n // tm,),
        in_specs=[
            pl.BlockSpec((tm, D_MODEL), row),
            pl.BlockSpec((tm, FOX_W), row),
            pl.BlockSpec((tm, DIL_W), row),
            pl.BlockSpec((tm, MEM_W), row),
            pl.BlockSpec((MIX_W, D_MODEL), const),
            pl.BlockSpec((1, D_MODEL), const),
        ],
        out_specs=pl.BlockSpec((tm, D_MODEL), row),
        out_shape=jax.ShapeDtypeStruct((n, D_MODEL), F32),
        compiler_params=pltpu.CompilerParams(
            dimension_semantics=("arbitrary",), vmem_limit_bytes=VMEM_LIMIT),
        name="out_proj",
    )(x2, yf, yd, ym, w, g)


def _rope_tables(seq):
    half = ROPE_DIM // 2
    pos = jnp.arange(seq, dtype=F32)
    inv_freq = 1.0 / (ROPE_THETA ** (jnp.arange(0, ROPE_DIM, 2, dtype=F32) / ROPE_DIM))
    ang = pos[:, None] * inv_freq[None, :]
    cos = jnp.cos(ang)
    sin = jnp.sin(ang)
    ones = jnp.ones((seq, HEAD_DIM - ROPE_DIM), F32)
    zeros = jnp.zeros((seq, HEAD_DIM - ROPE_DIM), F32)
    zh = jnp.zeros((seq, half), F32)
    cos_h = jnp.concatenate([cos, cos, ones], axis=1)
    sina_h = jnp.concatenate([zh, sin, zeros], axis=1)
    sinb_h = jnp.concatenate([-sin, zh, zeros], axis=1)
    tile = lambda t: jnp.concatenate([t, t], axis=1)
    return tile(cos_h), tile(sina_h), tile(sinb_h)


def _split_w_in(w):
    sizes = [FOX_W] * 4 + [FOX_HEADS] + [DIL_W] * 4 + [MEM_W] * 2
    offs = [0]
    for s in sizes:
        offs.append(offs[-1] + s)
    cols = [w[:, offs[k]:offs[k + 1]] for k in range(len(sizes))]
    fq, fk, fv, fg, flog, dq, dk, dv, dg, mq, mg = cols
    w_main = jnp.concatenate([fq, fk, fv, fg, dq, dk, dv, dg, mq, mg], axis=1).astype(BF16)
    w_fl = jnp.pad(flog, ((0, 0), (0, LANES - FOX_HEADS))).astype(BF16)
    return w_main, w_fl


def kernel(x, mem, norm_g, w_in, b_forget, mem_norm_g, w_mem_kv, w_out, final_norm_g):
    batch, seq, _ = x.shape
    depth = norm_g.shape[0]
    cos_t, sina_t, sinb_t = _rope_tables(seq)
    x2 = x.reshape(batch * seq, D_MODEL)
    for l in range(depth):
        w_main, w_fl = _split_w_in(w_in[l])
        b_fl = jnp.pad(b_forget[l], (0, LANES - FOX_HEADS)).reshape(1, LANES)
        of, od, odg, om = _in_proj(
            x2, norm_g[l].reshape(1, D_MODEL), w_main, w_fl, b_fl, cos_t, sina_t, sinb_t, seq=seq)
        yf = _fox(of.reshape(batch, seq, FOX_OUT_W))
        yd = _dilated(od.reshape(batch, seq, 2304), odg.reshape(batch, seq, DIL_W))
        mkv = _mem_kv(mem, mem_norm_g[l].reshape(1, D_MODEL), w_mem_kv[l].astype(BF16))
        ym = _mem_attn(om.reshape(batch, seq, 1024), mkv)
        last = l == depth - 1
        x2 = _out_proj(x2, yf.reshape(batch * seq, FOX_W), yd.reshape(batch * seq, DIL_W),
                       ym.reshape(batch * seq, MEM_W), w_out[l].astype(BF16),
                       final_norm_g.reshape(1, D_MODEL), final_norm=last)
    return x2.reshape(batch, seq, D_MODEL)
```

```python
import functools
import math

import jax
import jax.numpy as jnp
from jax import lax
from jax.experimental import pallas as pl
from jax.experimental.pallas import tpu as pltpu

F32 = jnp.float32
BF16 = jnp.bfloat16

D_MODEL = 1024
HEAD_DIM = 64
FOX_HEADS = 12
DIL_HEADS = 12
MEM_HEADS = 4
MEM_HEAD_DIM = 128
FOX_W = FOX_HEADS * HEAD_DIM
DIL_W = DIL_HEADS * HEAD_DIM
MEM_W = MEM_HEADS * MEM_HEAD_DIM
MIX_W = FOX_W + DIL_W + MEM_W
DILATIONS = ((128, 1), (512, 4), (2048, 16))
BLOCK = 128
ROPE_THETA = 500000.0
ROPE_DIM = HEAD_DIM // 4
RMS_EPS = 1e-6
NEG_INF = -1e30
QK_SCALE = 1.0 / math.sqrt(HEAD_DIM)
MEM_SCALE = 1.0 / math.sqrt(MEM_HEAD_DIM)

LANES = 128
HEAD_PAIRS = FOX_HEADS // 2
FOX_AUG_W = FOX_HEADS * LANES
FOX_OUT_W = 3 * FOX_AUG_W + FOX_W
FOX_ROWS = 128
FOX_KEYS = 256
FOX_LOOKAHEAD = 2
DIL_GROUP = 16
DIL_LOOKAHEAD = 2

_SEG = dict(fq=0, fk=768, fv=1536, fg=2304, dq=3072, dk=3840, dv=4608, dg=5376, mq=6144, mg=6656)
MAIN_W = 7168

VMEM_LIMIT = 56 * 1024 * 1024


def _silu(r):
    return r * (1.0 / (1.0 + jnp.exp(-r)))


def _rows(ref, start, size, stride):
    if stride == 1:
        return ref[pl.ds(start, size), :]
    return ref[pl.ds(start, size, stride=stride), :]


def _set_rows(ref, start, size, stride, val):
    if stride == 1:
        ref[pl.ds(start, size), :] = val
    else:
        ref[pl.ds(start, size, stride=stride), :] = val


def _in_proj_kernel(x_ref, g_ref, w_ref, wfl_ref, bf_ref, cos_ref, sina_ref, sinb_ref,
                    of_ref, od_ref, odg_ref, om_ref, carry_ref,
                    *, tm, tiles_per_seq):
    i = pl.program_id(0)
    x = x_ref[...]
    ms = jnp.mean(x * x, axis=-1, keepdims=True)
    h = (x * lax.rsqrt(ms + RMS_EPS) * g_ref[...]).astype(BF16)

    def seg(name, width):
        off = _SEG[name]
        return jnp.dot(h, w_ref[:, off:off + width], preferred_element_type=F32)

    fl = jnp.dot(h, wfl_ref[...], preferred_element_type=F32) + bf_ref[...]
    logf = jnp.minimum(fl, 0.0) - jnp.log1p(jnp.exp(-jnp.abs(fl)))
    a1 = logf.astype(BF16)
    r1 = logf - a1.astype(F32)
    a2 = r1.astype(BF16)
    a3 = (r1 - a2.astype(F32)).astype(BF16)
    row = lax.broadcasted_iota(jnp.int32, (tm, tm), 0)
    col = lax.broadcasted_iota(jnp.int32, (tm, tm), 1)
    tri = jnp.where(row >= col, 1.0, 0.0).astype(BF16)

    @pl.when(i % tiles_per_seq == 0)
    def _():
        carry_ref[...] = jnp.zeros_like(carry_ref)

    c = (jnp.dot(tri, a1, preferred_element_type=F32)
         + jnp.dot(tri, a2, preferred_element_type=F32)
         + jnp.dot(tri, a3, preferred_element_type=F32)) + carry_ref[...]
    carry_ref[...] = c[tm - 1:tm, :]

    c1 = c.astype(BF16).astype(F32)
    c2 = (c - c1).astype(BF16).astype(F32)
    c3 = (c - c1 - c2).astype(BF16).astype(F32)
    fq = seg("fq", 768) * QK_SCALE
    fk = seg("fk", 768)
    fv = seg("fv", 768)
    lane = lax.broadcasted_iota(jnp.int32, (tm, LANES), 1)
    for hd in range(FOX_HEADS):
        pair = slice(LANES * (hd // 2), LANES * (hd // 2 + 1))
        own = (lane < HEAD_DIM) if hd % 2 == 0 else (lane >= HEAD_DIM)
        base = HEAD_DIM if hd % 2 == 0 else 0
        b1, b2, b3 = (jnp.broadcast_to(t[:, hd:hd + 1], (tm, LANES)) for t in (c1, c2, c3))
        ones_q = jnp.where(jnp.logical_and(lane >= base, lane < base + 3), 1.0, 0.0)
        ones_k = jnp.where(jnp.logical_and(lane >= base + 3, lane < base + 6), 1.0, 0.0)
        q_bias = jnp.where(lane == base + 3, b1, jnp.where(lane == base + 4, b2,
                           jnp.where(lane == base + 5, b3, ones_q)))
        k_bias = jnp.where(lane == base, -b1, jnp.where(lane == base + 1, -b2,
                           jnp.where(lane == base + 2, -b3, ones_k)))
        dst = LANES * hd
        of_ref[:, dst:dst + LANES] = jnp.where(own, fq[:, pair], q_bias).astype(BF16)
        of_ref[:, FOX_AUG_W + dst:FOX_AUG_W + dst + LANES] = \
            jnp.where(own, fk[:, pair], k_bias).astype(BF16)
        of_ref[:, 2 * FOX_AUG_W + dst:2 * FOX_AUG_W + dst + LANES] = \
            jnp.where(own, fv[:, pair], 1.0).astype(BF16)
    of_ref[:, 3 * FOX_AUG_W:3 * FOX_AUG_W + FOX_W] = _silu(seg("fg", 768)).astype(BF16)

    cos = cos_ref[...]
    sina = sina_ref[...]
    sinb = sinb_ref[...]

    def rope_store(r, dst_off, scale):
        for j in range(DIL_W // LANES):
            c = r[:, LANES * j:LANES * (j + 1)]
            rot = c * cos + pltpu.roll(c, ROPE_DIM // 2, 1) * sina \
                + pltpu.roll(c, LANES - ROPE_DIM // 2, 1) * sinb
            if scale != 1.0:
                rot = rot * scale
            od_ref[:, dst_off + LANES * j:dst_off + LANES * (j + 1)] = rot

    rope_store(seg("dq", 768), 0, QK_SCALE)
    rope_store(seg("dk", 768), 768, 1.0)
    od_ref[:, 1536:2304] = seg("dv", 768)
    odg_ref[...] = _silu(seg("dg", 768)).astype(BF16)

    om_ref[:, 0:512] = seg("mq", 512).astype(BF16)
    om_ref[:, 512:1024] = _silu(seg("mg", 512)).astype(BF16)


def _in_proj(x2, g, w_main, w_fl, b_fl, cos_t, sina_t, sinb_t, *, seq, tm=256):
    n = x2.shape[0]
    tiles_per_seq = seq // tm
    const = lambda i: (0, 0)
    row = lambda i: (i, 0)
    pos = lambda i: (i % tiles_per_seq, 0)
    kernel = functools.partial(_in_proj_kernel, tm=tm, tiles_per_seq=tiles_per_seq)
    return pl.pallas_call(
        kernel,
        grid=(n // tm,),
        in_specs=[
            pl.BlockSpec((tm, D_MODEL), row),
            pl.BlockSpec((1, D_MODEL), const),
            pl.BlockSpec((D_MODEL, MAIN_W), const),
            pl.BlockSpec((D_MODEL, LANES), const),
            pl.BlockSpec((1, LANES), const),
            pl.BlockSpec((tm, LANES), pos),
            pl.BlockSpec((tm, LANES), pos),
            pl.BlockSpec((tm, LANES), pos),
        ],
        out_specs=[
            pl.BlockSpec((tm, FOX_OUT_W), row),
            pl.BlockSpec((tm, 2304), row),
            pl.BlockSpec((tm, 768), row),
            pl.BlockSpec((tm, 1024), row),
        ],
        out_shape=[
            jax.ShapeDtypeStruct((n, FOX_OUT_W), BF16),
            jax.ShapeDtypeStruct((n, 2304), F32),
            jax.ShapeDtypeStruct((n, 768), BF16),
            jax.ShapeDtypeStruct((n, 1024), BF16),
        ],
        scratch_shapes=[pltpu.VMEM((1, LANES), F32)],
        compiler_params=pltpu.CompilerParams(
            dimension_semantics=("arbitrary",), vmem_limit_bytes=VMEM_LIMIT),
        name="in_proj",
    )(x2, g, w_main, w_fl, b_fl, cos_t, sina_t, sinb_t)


def _fox_kernel(q_ref, k_ref, v_ref, g_ref, o_ref, acc_ref, m_ref, *, tq):
    i = pl.program_id(1)
    tk = FOX_KEYS
    row_chunks = tq // FOX_ROWS
    lane = lax.broadcasted_iota(jnp.int32, (FOX_ROWS, LANES), 1)
    lo = lane < HEAD_DIM
    row = lax.broadcasted_iota(jnp.int32, (FOX_ROWS, tk), 0)
    col = lax.broadcasted_iota(jnp.int32, (FOX_ROWS, tk), 1)
    bias_lo = jnp.where(row >= col, 0.0, NEG_INF)
    bias_hi = jnp.where(row + FOX_ROWS >= col, 0.0, NEG_INF)

    m_ref[...] = jnp.full(m_ref.shape, NEG_INF, F32)
    acc_ref[...] = jnp.zeros(acc_ref.shape, F32)

    def run_chains(chains):
        def scores(c):
            hd, rc, koff, klen, _ = chains[c]
            ls = slice(LANES * hd, LANES * (hd + 1))
            rs = slice(FOX_ROWS * rc, FOX_ROWS * (rc + 1))
            return lax.dot_general(q_ref[0, rs, ls], k_ref[0, pl.ds(koff, klen), ls],
                                   (((1,), (1,)), ((), ())), preferred_element_type=F32)

        pending = [scores(c) for c in range(FOX_LOOKAHEAD)]
        for c, (hd, rc, koff, klen, bias) in enumerate(chains):
            if c + FOX_LOOKAHEAD < len(chains):
                pending.append(scores(c + FOX_LOOKAHEAD))
            s = pending.pop(0)
            ls = slice(LANES * hd, LANES * (hd + 1))
            rs = slice(FOX_ROWS * rc, FOX_ROWS * (rc + 1))
            if bias is not None:
                s = s + bias
            m_old = m_ref[hd, rs, :]
            m_new = jnp.maximum(
                m_old, jnp.broadcast_to(jnp.max(s, axis=-1, keepdims=True), (FOX_ROWS, LANES)))
            alpha = jnp.exp(m_old - m_new)
            p = jnp.concatenate(
                [jnp.exp(s[:, LANES * t:LANES * (t + 1)] - m_new) for t in range(klen // LANES)],
                axis=1)
            acc_ref[hd, rs, :] = alpha * acc_ref[hd, rs, :] + jnp.dot(
                p.astype(BF16), v_ref[0, pl.ds(koff, klen), ls], preferred_element_type=F32)
            m_ref[hd, rs, :] = m_new

    def body(j, _):
        first = pl.multiple_of(j * tq, tq)
        run_chains([(hd, rc, first + kt * tk, tk, None)
                    for kt in range(tq // tk) for hd in range(FOX_HEADS) for rc in range(row_chunks)])
        return 0

    def diag_body(t, _, masked):
        first = pl.multiple_of(t * tq, tq)
        chains = []
        for kt in range(tq // tk):
            koff = first + kt * tk
            for hd in range(FOX_HEADS):
                if masked:
                    chains += [(hd, rc, koff, tk, bias_lo if rc % 2 == 0 else bias_hi)
                               for rc in (2 * kt, 2 * kt + 1)]
                else:
                    chains += [(hd, rc, koff, tk, None) for rc in range(2 * kt + 2, row_chunks)]
        run_chains(chains)
        return 0

    lax.fori_loop(0, i, body, 0)
    lax.fori_loop(i, i + 1, functools.partial(diag_body, masked=True), 0)
    lax.fori_loop(i, i + 1, functools.partial(diag_body, masked=False), 0)

    for hp in range(HEAD_PAIRS):
        ls = slice(LANES * hp, LANES * (hp + 1))
        for rc in range(row_chunks):
            rs = slice(FOX_ROWS * rc, FOX_ROWS * (rc + 1))
            a0 = acc_ref[2 * hp, rs, :]
            a1 = acc_ref[2 * hp + 1, rs, :]
            o2 = jnp.where(lo, a0 / pltpu.roll(a0, HEAD_DIM, 1), a1 / pltpu.roll(a1, HEAD_DIM, 1))
            o_ref[0, rs, ls] = (o2 * g_ref[0, rs, ls].astype(F32)).astype(BF16)


def _fox(of3, *, tq=512):
    batch, seq, _ = of3.shape
    kernel = functools.partial(_fox_kernel, tq=tq)
    gate_blk = 3 * FOX_AUG_W // FOX_W
    return pl.pallas_call(
        kernel,
        grid=(batch, seq // tq),
        in_specs=[
            pl.BlockSpec((1, tq, FOX_AUG_W), lambda b, i: (b, i, 0)),
            pl.BlockSpec((1, seq, FOX_AUG_W), lambda b, i: (b, 0, 1)),
            pl.BlockSpec((1, seq, FOX_AUG_W), lambda b, i: (b, 0, 2)),
            pl.BlockSpec((1, tq, FOX_W), lambda b, i: (b, i, gate_blk)),
        ],
        out_specs=pl.BlockSpec((1, tq, FOX_W), lambda b, i: (b, i, 0)),
        out_shape=jax.ShapeDtypeStruct((batch, seq, FOX_W), BF16),
        scratch_shapes=[
            pltpu.VMEM((FOX_HEADS, tq, LANES), F32),
            pltpu.VMEM((FOX_HEADS, tq, LANES), F32),
        ],
        compiler_params=pltpu.CompilerParams(
            dimension_semantics=("arbitrary", "arbitrary"), vmem_limit_bytes=VMEM_LIMIT),
        name="fox_attn",
    )(of3, of3, of3, of3)


def _dil_kernel(q_ref, k_ref, v_ref, g_ref, o_ref, q0, q1, kd, v0, v1, nd_ref, m_ref, *, seq):
    n_blocks = seq // BLOCK
    lane = lax.broadcasted_iota(jnp.int32, (BLOCK, LANES), 1)
    lo = lane < HEAD_DIM
    row2 = lax.broadcasted_iota(jnp.int32, (BLOCK, 2 * BLOCK), 0)
    col2 = lax.broadcasted_iota(jnp.int32, (BLOCK, 2 * BLOCK), 1)
    band = jnp.logical_and(col2 >= row2, col2 <= row2 + BLOCK)
    bias_band = jnp.where(band, 0.0, NEG_INF)
    bias_first = jnp.where(jnp.logical_and(band, col2 >= BLOCK), 0.0, NEG_INF)

    kd[0:BLOCK, :] = jnp.zeros((BLOCK, LANES), BF16)
    v0[0:BLOCK, :] = jnp.zeros((BLOCK, LANES), BF16)
    v1[0:BLOCK, :] = jnp.zeros((BLOCK, LANES), BF16)

    for slot, (window, d) in enumerate(DILATIONS):
        assert window // d == BLOCK
        length = seq // d
        nb = length // BLOCK
        lo_l = lax.broadcasted_iota(jnp.int32, (length, LANES), 1) < HEAD_DIM
        for r in range(d):
            qdst = slice(r * length, (r + 1) * length)
            kdst = slice(BLOCK + r * length, BLOCK + (r + 1) * length)
            qf = _rows(q_ref.at[0], r, length, d).astype(BF16)
            q0[qdst, :] = jnp.where(lo_l, qf, jnp.zeros_like(qf))
            q1[qdst, :] = jnp.where(lo_l, jnp.zeros_like(qf), qf)
            kd[kdst, :] = _rows(k_ref.at[0], r, length, d).astype(BF16)
            vf = _rows(v_ref.at[0], r, length, d).astype(BF16)
            v0[kdst, :] = jnp.where(lo_l, vf, jnp.ones_like(vf))
            v1[kdst, :] = jnp.where(lo_l, jnp.ones_like(vf), vf)

        def body(g, _, slot=slot, d=d, nb=nb):
            chains = [(u, hh) for u in range(DIL_GROUP) for hh in range(2)]
            qv = ((q0, v0), (q1, v1))

            def base_of(u):
                return pl.multiple_of((g * DIL_GROUP + u) * BLOCK, BLOCK)

            def scores(c):
                u, hh = chains[c]
                base = base_of(u)
                return lax.dot_general(qv[hh][0][pl.ds(base, BLOCK), :],
                                       kd[pl.ds(base, 2 * BLOCK), :], (((1,), (1,)), ((), ())),
                                       preferred_element_type=F32)

            pending = [scores(c) for c in range(DIL_LOOKAHEAD)]
            for c, (u, hh) in enumerate(chains):
                if c + DIL_LOOKAHEAD < len(chains):
                    pending.append(scores(c + DIL_LOOKAHEAD))
                t = g * DIL_GROUP + u
                base = base_of(u)
                if nb == 1:
                    bias = bias_first
                    nat_start = t
                else:
                    r = t // nb
                    n = t - r * nb
                    bias = jnp.where(n == 0, bias_first, bias_band)
                    nat_start = base if d == 1 else n * (BLOCK * d) + r
                s = pending.pop(0) + bias
                mx = jnp.broadcast_to(jnp.max(s, axis=-1, keepdims=True), (BLOCK, LANES))
                e = jnp.concatenate(
                    [jnp.exp(s[:, LANES * t2:LANES * (t2 + 1)] - mx) for t2 in range(2)], axis=1)
                nd = jnp.dot(e.astype(BF16), qv[hh][1][pl.ds(base, 2 * BLOCK), :],
                             preferred_element_type=F32)
                _set_rows(nd_ref.at[slot, hh], nat_start, BLOCK, d, nd)
                _set_rows(m_ref.at[slot, hh], nat_start, BLOCK, d, mx)
            return 0

        lax.fori_loop(0, n_blocks // DIL_GROUP, body, 0)

    def merge(c, _):
        rs = pl.ds(pl.multiple_of(c * BLOCK, BLOCK), BLOCK)
        outs = []
        for hh in range(2):
            ms = [m_ref[p, hh, rs, :] for p in range(len(DILATIONS))]
            m_all = jnp.maximum(jnp.maximum(ms[0], ms[1]), ms[2])
            tot = nd_ref[0, hh, rs, :] * jnp.exp(ms[0] - m_all)
            for p in range(1, len(DILATIONS)):
                tot = tot + nd_ref[p, hh, rs, :] * jnp.exp(ms[p] - m_all)
            outs.append(tot / pltpu.roll(tot, HEAD_DIM, 1))
        o = jnp.where(lo, outs[0], outs[1]) * g_ref[0, rs, :].astype(F32)
        o_ref[0, rs, :] = o.astype(BF16)
        return 0

    lax.fori_loop(0, n_blocks, merge, 0, unroll=2)


def _dilated(od3, odg3):
    batch, seq, _ = od3.shape
    kernel = functools.partial(_dil_kernel, seq=seq)
    n_pat = len(DILATIONS)
    return pl.pallas_call(
        kernel,
        grid=(batch, HEAD_PAIRS),
        in_specs=[
            pl.BlockSpec((1, seq, LANES), lambda b, p: (b, 0, p)),
            pl.BlockSpec((1, seq, LANES), lambda b, p: (b, 0, HEAD_PAIRS + p)),
            pl.BlockSpec((1, seq, LANES), lambda b, p: (b, 0, 2 * HEAD_PAIRS + p)),
            pl.BlockSpec((1, seq, LANES), lambda b, p: (b, 0, p)),
        ],
        out_specs=pl.BlockSpec((1, seq, LANES), lambda b, p: (b, 0, p)),
        out_shape=jax.ShapeDtypeStruct((batch, seq, DIL_W), BF16),
        scratch_shapes=[
            pltpu.VMEM((seq, LANES), BF16),
            pltpu.VMEM((seq, LANES), BF16),
            pltpu.VMEM((seq + BLOCK, LANES), BF16),
            pltpu.VMEM((seq + BLOCK, LANES), BF16),
            pltpu.VMEM((seq + BLOCK, LANES), BF16),
            pltpu.VMEM((n_pat, 2, seq, LANES), F32),
            pltpu.VMEM((n_pat, 2, seq, LANES), F32),
        ],
        compiler_params=pltpu.CompilerParams(
            dimension_semantics=("arbitrary", "arbitrary"), vmem_limit_bytes=VMEM_LIMIT),
        name="dilated_attn",
    )(od3, od3, od3, odg3)


def _mem_kv_kernel(mem_ref, g_ref, w_ref, o_ref):
    x = mem_ref[0]
    ms = jnp.mean(x * x, axis=-1, keepdims=True)
    h = (x * lax.rsqrt(ms + RMS_EPS) * g_ref[...]).astype(BF16)
    o_ref[0] = jnp.dot(h, w_ref[...], preferred_element_type=F32).astype(BF16)


def _mem_kv(mem, g, w):
    batch, mlen, _ = mem.shape
    return pl.pallas_call(
        _mem_kv_kernel,
        grid=(batch,),
        in_specs=[
            pl.BlockSpec((1, mlen, D_MODEL), lambda b: (b, 0, 0)),
            pl.BlockSpec((1, D_MODEL), lambda b: (0, 0)),
            pl.BlockSpec((D_MODEL, 2 * MEM_W), lambda b: (0, 0)),
        ],
        out_specs=pl.BlockSpec((1, mlen, 2 * MEM_W), lambda b: (b, 0, 0)),
        out_shape=jax.ShapeDtypeStruct((batch, mlen, 2 * MEM_W), BF16),
        compiler_params=pltpu.CompilerParams(
            dimension_semantics=("arbitrary",), vmem_limit_bytes=VMEM_LIMIT),
        name="mem_kv",
    )(mem, g, w)


def _mem_attn_kernel(q_ref, g_ref, mk_ref, mv_ref, o_ref):
    tq = q_ref.shape[1]
    chains = [(h, rc) for h in range(MEM_HEADS) for rc in range(tq // BLOCK)]

    def scores(c):
        h, rc = chains[c]
        ls = slice(MEM_HEAD_DIM * h, MEM_HEAD_DIM * (h + 1))
        return lax.dot_general(q_ref[0, BLOCK * rc:BLOCK * (rc + 1), ls], mk_ref[0, :, ls],
                               (((1,), (1,)), ((), ())), preferred_element_type=F32)

    pending = [scores(c) for c in range(FOX_LOOKAHEAD)]
    for c, (h, rc) in enumerate(chains):
        if c + FOX_LOOKAHEAD < len(chains):
            pending.append(scores(c + FOX_LOOKAHEAD))
        ls = slice(MEM_HEAD_DIM * h, MEM_HEAD_DIM * (h + 1))
        rs = slice(BLOCK * rc, BLOCK * (rc + 1))
        s = pending.pop(0) * MEM_SCALE
        mx = jnp.max(s, axis=-1, keepdims=True)
        e = jnp.exp(s - mx)
        den = jnp.sum(e, axis=-1, keepdims=True)
        o = jnp.dot(e.astype(BF16), mv_ref[0, :, ls], preferred_element_type=F32) / den
        o_ref[0, rs, ls] = (o * g_ref[0, rs, ls].astype(F32)).astype(BF16)


def _mem_attn(om3, mkv, *, tq=2048):
    batch, seq, _ = om3.shape
    mlen = mkv.shape[1]
    return pl.pallas_call(
        _mem_attn_kernel,
        grid=(batch, seq // tq),
        in_specs=[
            pl.BlockSpec((1, tq, MEM_W), lambda b, i: (b, i, 0)),
            pl.BlockSpec((1, tq, MEM_W), lambda b, i: (b, i, 1)),
            pl.BlockSpec((1, mlen, MEM_W), lambda b, i: (b, 0, 0)),
            pl.BlockSpec((1, mlen, MEM_W), lambda b, i: (b, 0, 1)),
        ],
        out_specs=pl.BlockSpec((1, tq, MEM_W), lambda b, i: (b, i, 0)),
        out_shape=jax.ShapeDtypeStruct((batch, seq, MEM_W), BF16),
        compiler_params=pltpu.CompilerParams(
            dimension_semantics=("arbitrary", "arbitrary"), vmem_limit_bytes=VMEM_LIMIT),
        name="mem_attn",
    )(om3, om3, mkv, mkv)


def _out_proj_kernel(x_ref, yf_ref, yd_ref, ym_ref, w_ref, g_ref, o_ref, *, final_norm):
    z = x_ref[...]
    z = z + jnp.dot(yf_ref[...], w_ref[0:FOX_W, :], preferred_element_type=F32)
    z = z + jnp.dot(yd_ref[...], w_ref[FOX_W:FOX_W + DIL_W, :], preferred_element_type=F32)
    z = z + jnp.dot(ym_ref[...], w_ref[FOX_W + DIL_W:MIX_W, :], preferred_element_type=F32)
    if final_norm:
        ms = jnp.mean(z * z, axis=-1, keepdims=True)
        z = z * lax.rsqrt(ms + RMS_EPS) * g_ref[...]
    o_ref[...] = z


def _out_proj(x2, yf, yd, ym, w, g, *, final_norm, tm=512):
    n = x2.shape[0]
    row = lambda i: (i, 0)
    const = lambda i: (0, 0)
    kernel = functools.partial(_out_proj_kernel, final_norm=final_norm)
    return pl.pallas_call(
        kernel,
        grid=(n // tm,),
        in_specs=[
            pl.BlockSpec((tm, D_MODEL), row),
            pl.BlockSpec((tm, FOX_W), row),
            pl.BlockSpec((tm, DIL_W), row),
            pl.BlockSpec((tm, MEM_W), row),
            pl.BlockSpec((MIX_W, D_MODEL), const),
            pl.BlockSpec((1, D_MODEL), const),
        ],
        out_specs=pl.BlockSpec((tm, D_MODEL), row),
        out_shape=jax.ShapeDtypeStruct((n, D_MODEL), F32),
        compiler_params=pltpu.CompilerParams(
            dimension_semantics=("arbitrary",), vmem_limit_bytes=VMEM_LIMIT),
        name="out_proj",
    )(x2, yf, yd, ym, w, g)


def _rope_tables(seq):
    half = ROPE_DIM // 2
    pos = jnp.arange(seq, dtype=F32)
    inv_freq = 1.0 / (ROPE_THETA ** (jnp.arange(0, ROPE_DIM, 2, dtype=F32) / ROPE_DIM))
    ang = pos[:, None] * inv_freq[None, :]
    cos = jnp.cos(ang)
    sin = jnp.sin(ang)
    ones = jnp.ones((seq, HEAD_DIM - ROPE_DIM), F32)
    zeros = jnp.zeros((seq, HEAD_DIM - ROPE_DIM), F32)
    zh = jnp.zeros((seq, half), F32)
    cos_h = jnp.concatenate([cos, cos, ones], axis=1)
    sina_h = jnp.concatenate([zh, sin, zeros], axis=1)
    sinb_h = jnp.concatenate([-sin, zh, zeros], axis=1)
    tile = lambda t: jnp.concatenate([t, t], axis=1)
    return tile(cos_h), tile(sina_h), tile(sinb_h)


def _split_w_in(w):
    sizes = [FOX_W] * 4 + [FOX_HEADS] + [DIL_W] * 4 + [MEM_W] * 2
    offs = [0]
    for s in sizes:
        offs.append(offs[-1] + s)
    cols = [w[:, offs[k]:offs[k + 1]] for k in range(len(sizes))]
    fq, fk, fv, fg, flog, dq, dk, dv, dg, mq, mg = cols
    w_main = jnp.concatenate([fq, fk, fv, fg, dq, dk, dv, dg, mq, mg], axis=1).astype(BF16)
    w_fl = jnp.pad(flog, ((0, 0), (0, LANES - FOX_HEADS))).astype(BF16)
    return w_main, w_fl


def kernel(x, mem, norm_g, w_in, b_forget, mem_norm_g, w_mem_kv, w_out, final_norm_g):
    batch, seq, _ = x.shape
    depth = norm_g.shape[0]
    cos_t, sina_t, sinb_t = _rope_tables(seq)
    x2 = x.reshape(batch * seq, D_MODEL)
    for l in range(depth):
        w_main, w_fl = _split_w_in(w_in[l])
        b_fl = jnp.pad(b_forget[l], (0, LANES - FOX_HEADS)).reshape(1, LANES)
        of, od, odg, om = _in_proj(
            x2, norm_g[l].reshape(1, D_MODEL), w_main, w_fl, b_fl, cos_t, sina_t, sinb_t, seq=seq)
        yf = _fox(of.reshape(batch, seq, FOX_OUT_W))
        yd = _dilated(od.reshape(batch, seq, 2304), odg.reshape(batch, seq, DIL_W))
        mkv = _mem_kv(mem, mem_norm_g[l].reshape(1, D_MODEL), w_mem_kv[l].astype(BF16))
        ym = _mem_attn(om.reshape(batch, seq, 1024), mkv)
        last = l == depth - 1
        x2 = _out_proj(x2, yf.reshape(batch * seq, FOX_W), yd.reshape(batch * seq, DIL_W),
                       ym.reshape(batch * seq, MEM_W), w_out[l].astype(BF16),
                       final_norm_g.reshape(1, D_MODEL), final_norm=last)
    return x2.reshape(batch, seq, D_MODEL)
```

```python
import functools
import math

import jax
import jax.numpy as jnp
from jax import lax
from jax.experimental import pallas as pl
from jax.experimental.pallas import tpu as pltpu

F32 = jnp.float32
BF16 = jnp.bfloat16

D_MODEL = 1024
HEAD_DIM = 64
FOX_HEADS = 12
DIL_HEADS = 12
MEM_HEADS = 4
MEM_HEAD_DIM = 128
FOX_W = FOX_HEADS * HEAD_DIM
DIL_W = DIL_HEADS * HEAD_DIM
MEM_W = MEM_HEADS * MEM_HEAD_DIM
MIX_W = FOX_W + DIL_W + MEM_W
DILATIONS = ((128, 1), (512, 4), (2048, 16))
BLOCK = 128
ROPE_THETA = 500000.0
ROPE_DIM = HEAD_DIM // 4
RMS_EPS = 1e-6
NEG_INF = -1e30
QK_SCALE = 1.0 / math.sqrt(HEAD_DIM)
MEM_SCALE = 1.0 / math.sqrt(MEM_HEAD_DIM)

LANES = 128
HEAD_PAIRS = FOX_HEADS // 2
FOX_AUG_W = FOX_HEADS * LANES
FOX_OUT_W = 3 * FOX_AUG_W + FOX_W
FOX_ROWS = 128
FOX_KEYS = 256
FOX_LOOKAHEAD = 2
DIL_GROUP = 16
DIL_STAGE = 4

_SEG = dict(fq=0, fk=768, fv=1536, fg=2304, dq=3072, dk=3840, dv=4608, dg=5376, mq=6144, mg=6656)
MAIN_W = 7168

VMEM_LIMIT = 56 * 1024 * 1024


def _silu(r):
    return r * (1.0 / (1.0 + jnp.exp(-r)))


def _rows(ref, start, size, stride):
    if stride == 1:
        return ref[pl.ds(start, size), :]
    return ref[pl.ds(start, size, stride=stride), :]


def _set_rows(ref, start, size, stride, val):
    if stride == 1:
        ref[pl.ds(start, size), :] = val
    else:
        ref[pl.ds(start, size, stride=stride), :] = val


def _in_proj_kernel(x_ref, g_ref, w_ref, wfl_ref, bf_ref, cos_ref, sina_ref, sinb_ref,
                    of_ref, od_ref, odg_ref, om_ref, carry_ref,
                    *, tm, tiles_per_seq):
    i = pl.program_id(0)
    x = x_ref[...]
    ms = jnp.mean(x * x, axis=-1, keepdims=True)
    h = (x * lax.rsqrt(ms + RMS_EPS) * g_ref[...]).astype(BF16)

    def seg(name, width):
        off = _SEG[name]
        return jnp.dot(h, w_ref[:, off:off + width], preferred_element_type=F32)

    fl = jnp.dot(h, wfl_ref[...], preferred_element_type=F32) + bf_ref[...]
    logf = jnp.minimum(fl, 0.0) - jnp.log1p(jnp.exp(-jnp.abs(fl)))
    a1 = logf.astype(BF16)
    r1 = logf - a1.astype(F32)
    a2 = r1.astype(BF16)
    a3 = (r1 - a2.astype(F32)).astype(BF16)
    row = lax.broadcasted_iota(jnp.int32, (tm, tm), 0)
    col = lax.broadcasted_iota(jnp.int32, (tm, tm), 1)
    tri = jnp.where(row >= col, 1.0, 0.0).astype(BF16)

    @pl.when(i % tiles_per_seq == 0)
    def _():
        carry_ref[...] = jnp.zeros_like(carry_ref)

    c = (jnp.dot(tri, a1, preferred_element_type=F32)
         + jnp.dot(tri, a2, preferred_element_type=F32)
         + jnp.dot(tri, a3, preferred_element_type=F32)) + carry_ref[...]
    carry_ref[...] = c[tm - 1:tm, :]

    c1 = c.astype(BF16).astype(F32)
    c2 = (c - c1).astype(BF16).astype(F32)
    c3 = (c - c1 - c2).astype(BF16).astype(F32)
    fq = seg("fq", 768) * QK_SCALE
    fk = seg("fk", 768)
    fv = seg("fv", 768)
    lane = lax.broadcasted_iota(jnp.int32, (tm, LANES), 1)
    for hd in range(FOX_HEADS):
        pair = slice(LANES * (hd // 2), LANES * (hd // 2 + 1))
        own = (lane < HEAD_DIM) if hd % 2 == 0 else (lane >= HEAD_DIM)
        base = HEAD_DIM if hd % 2 == 0 else 0
        b1, b2, b3 = (jnp.broadcast_to(t[:, hd:hd + 1], (tm, LANES)) for t in (c1, c2, c3))
        ones_q = jnp.where(jnp.logical_and(lane >= base, lane < base + 3), 1.0, 0.0)
        ones_k = jnp.where(jnp.logical_and(lane >= base + 3, lane < base + 6), 1.0, 0.0)
        q_bias = jnp.where(lane == base + 3, b1, jnp.where(lane == base + 4, b2,
                           jnp.where(lane == base + 5, b3, ones_q)))
        k_bias = jnp.where(lane == base, -b1, jnp.where(lane == base + 1, -b2,
                           jnp.where(lane == base + 2, -b3, ones_k)))
        dst = LANES * hd
        of_ref[:, dst:dst + LANES] = jnp.where(own, fq[:, pair], q_bias).astype(BF16)
        of_ref[:, FOX_AUG_W + dst:FOX_AUG_W + dst + LANES] = \
            jnp.where(own, fk[:, pair], k_bias).astype(BF16)
        of_ref[:, 2 * FOX_AUG_W + dst:2 * FOX_AUG_W + dst + LANES] = \
            jnp.where(own, fv[:, pair], 1.0).astype(BF16)
    of_ref[:, 3 * FOX_AUG_W:3 * FOX_AUG_W + FOX_W] = _silu(seg("fg", 768)).astype(BF16)

    cos = cos_ref[...]
    sina = sina_ref[...]
    sinb = sinb_ref[...]

    def rope_store(r, dst_off, scale):
        for j in range(DIL_W // LANES):
            c = r[:, LANES * j:LANES * (j + 1)]
            rot = c * cos + pltpu.roll(c, ROPE_DIM // 2, 1) * sina \
                + pltpu.roll(c, LANES - ROPE_DIM // 2, 1) * sinb
            if scale != 1.0:
                rot = rot * scale
            od_ref[:, dst_off + LANES * j:dst_off + LANES * (j + 1)] = rot

    rope_store(seg("dq", 768), 0, QK_SCALE)
    rope_store(seg("dk", 768), 768, 1.0)
    od_ref[:, 1536:2304] = seg("dv", 768)
    odg_ref[...] = _silu(seg("dg", 768)).astype(BF16)

    om_ref[:, 0:512] = seg("mq", 512).astype(BF16)
    om_ref[:, 512:1024] = _silu(seg("mg", 512)).astype(BF16)


def _in_proj(x2, g, w_main, w_fl, b_fl, cos_t, sina_t, sinb_t, *, seq, tm=256):
    n = x2.shape[0]
    tiles_per_seq = seq // tm
    const = lambda i: (0, 0)
    row = lambda i: (i, 0)
    pos = lambda i: (i % tiles_per_seq, 0)
    kernel = functools.partial(_in_proj_kernel, tm=tm, tiles_per_seq=tiles_per_seq)
    return pl.pallas_call(
        kernel,
        grid=(n // tm,),
        in_specs=[
            pl.BlockSpec((tm, D_MODEL), row),
            pl.BlockSpec((1, D_MODEL), const),
            pl.BlockSpec((D_MODEL, MAIN_W), const),
            pl.BlockSpec((D_MODEL, LANES), const),
            pl.BlockSpec((1, LANES), const),
            pl.BlockSpec((tm, LANES), pos),
            pl.BlockSpec((tm, LANES), pos),
            pl.BlockSpec((tm, LANES), pos),
        ],
        out_specs=[
            pl.BlockSpec((tm, FOX_OUT_W), row),
            pl.BlockSpec((tm, 2304), row),
            pl.BlockSpec((tm, 768), row),
            pl.BlockSpec((tm, 1024), row),
        ],
        out_shape=[
            jax.ShapeDtypeStruct((n, FOX_OUT_W), BF16),
            jax.ShapeDtypeStruct((n, 2304), F32),
            jax.ShapeDtypeStruct((n, 768), BF16),
            jax.ShapeDtypeStruct((n, 1024), BF16),
        ],
        scratch_shapes=[pltpu.VMEM((1, LANES), F32)],
        compiler_params=pltpu.CompilerParams(
            dimension_semantics=("arbitrary",), vmem_limit_bytes=VMEM_LIMIT),
        name="in_proj",
    )(x2, g, w_main, w_fl, b_fl, cos_t, sina_t, sinb_t)


def _fox_kernel(q_ref, k_ref, v_ref, g_ref, o_ref, acc_ref, m_ref, *, tq):
    i = pl.program_id(1)
    tk = FOX_KEYS
    row_chunks = tq // FOX_ROWS
    lane = lax.broadcasted_iota(jnp.int32, (FOX_ROWS, LANES), 1)
    lo = lane < HEAD_DIM
    row = lax.broadcasted_iota(jnp.int32, (FOX_ROWS, tk), 0)
    col = lax.broadcasted_iota(jnp.int32, (FOX_ROWS, tk), 1)
    bias_lo = jnp.where(row >= col, 0.0, NEG_INF)
    bias_hi = jnp.where(row + FOX_ROWS >= col, 0.0, NEG_INF)

    m_ref[...] = jnp.full(m_ref.shape, NEG_INF, F32)
    acc_ref[...] = jnp.zeros(acc_ref.shape, F32)

    def run_chains(chains):
        def scores(c):
            hd, rc, koff, klen, _ = chains[c]
            ls = slice(LANES * hd, LANES * (hd + 1))
            rs = slice(FOX_ROWS * rc, FOX_ROWS * (rc + 1))
            return lax.dot_general(q_ref[0, rs, ls], k_ref[0, pl.ds(koff, klen), ls],
                                   (((1,), (1,)), ((), ())), preferred_element_type=F32)

        pending = [scores(c) for c in range(FOX_LOOKAHEAD)]
        for c, (hd, rc, koff, klen, bias) in enumerate(chains):
            if c + FOX_LOOKAHEAD < len(chains):
                pending.append(scores(c + FOX_LOOKAHEAD))
            s = pending.pop(0)
            ls = slice(LANES * hd, LANES * (hd + 1))
            rs = slice(FOX_ROWS * rc, FOX_ROWS * (rc + 1))
            if bias is not None:
                s = s + bias
            m_old = m_ref[hd, rs, :]
            m_new = jnp.maximum(
                m_old, jnp.broadcast_to(jnp.max(s, axis=-1, keepdims=True), (FOX_ROWS, LANES)))
            alpha = jnp.exp(m_old - m_new)
            p = jnp.concatenate(
                [jnp.exp(s[:, LANES * t:LANES * (t + 1)] - m_new) for t in range(klen // LANES)],
                axis=1)
            acc_ref[hd, rs, :] = alpha * acc_ref[hd, rs, :] + jnp.dot(
                p.astype(BF16), v_ref[0, pl.ds(koff, klen), ls], preferred_element_type=F32)
            m_ref[hd, rs, :] = m_new

    def body(j, _):
        first = pl.multiple_of(j * tq, tq)
        run_chains([(hd, rc, first + kt * tk, tk, None)
                    for kt in range(tq // tk) for hd in range(FOX_HEADS) for rc in range(row_chunks)])
        return 0

    def diag_body(t, _, masked):
        first = pl.multiple_of(t * tq, tq)
        chains = []
        for kt in range(tq // tk):
            koff = first + kt * tk
            for hd in range(FOX_HEADS):
                if masked:
                    chains += [(hd, rc, koff, tk, bias_lo if rc % 2 == 0 else bias_hi)
                               for rc in (2 * kt, 2 * kt + 1)]
                else:
                    chains += [(hd, rc, koff, tk, None) for rc in range(2 * kt + 2, row_chunks)]
        run_chains(chains)
        return 0

    lax.fori_loop(0, i, body, 0)
    lax.fori_loop(i, i + 1, functools.partial(diag_body, masked=True), 0)
    lax.fori_loop(i, i + 1, functools.partial(diag_body, masked=False), 0)

    for hp in range(HEAD_PAIRS):
        ls = slice(LANES * hp, LANES * (hp + 1))
        for rc in range(row_chunks):
            rs = slice(FOX_ROWS * rc, FOX_ROWS * (rc + 1))
            a0 = acc_ref[2 * hp, rs, :]
            a1 = acc_ref[2 * hp + 1, rs, :]
            o2 = jnp.where(lo, a0 / pltpu.roll(a0, HEAD_DIM, 1), a1 / pltpu.roll(a1, HEAD_DIM, 1))
            o_ref[0, rs, ls] = (o2 * g_ref[0, rs, ls].astype(F32)).astype(BF16)


def _fox(of3, *, tq=512):
    batch, seq, _ = of3.shape
    kernel = functools.partial(_fox_kernel, tq=tq)
    gate_blk = 3 * FOX_AUG_W // FOX_W
    return pl.pallas_call(
        kernel,
        grid=(batch, seq // tq),
        in_specs=[
            pl.BlockSpec((1, tq, FOX_AUG_W), lambda b, i: (b, i, 0)),
            pl.BlockSpec((1, seq, FOX_AUG_W), lambda b, i: (b, 0, 1)),
            pl.BlockSpec((1, seq, FOX_AUG_W), lambda b, i: (b, 0, 2)),
            pl.BlockSpec((1, tq, FOX_W), lambda b, i: (b, i, gate_blk)),
        ],
        out_specs=pl.BlockSpec((1, tq, FOX_W), lambda b, i: (b, i, 0)),
        out_shape=jax.ShapeDtypeStruct((batch, seq, FOX_W), BF16),
        scratch_shapes=[
            pltpu.VMEM((FOX_HEADS, tq, LANES), F32),
            pltpu.VMEM((FOX_HEADS, tq, LANES), F32),
        ],
        compiler_params=pltpu.CompilerParams(
            dimension_semantics=("arbitrary", "arbitrary"), vmem_limit_bytes=VMEM_LIMIT),
        name="fox_attn",
    )(of3, of3, of3, of3)


def _dil_kernel(q_ref, k_ref, v_ref, g_ref, o_ref, q0, q1, kd, v0, v1, nd_ref, m_ref,
                tq_ref, tk_ref, tv_ref, und_ref, um_ref, *, seq):
    n_blocks = seq // BLOCK
    lane = lax.broadcasted_iota(jnp.int32, (BLOCK, LANES), 1)
    lo = lane < HEAD_DIM
    row2 = lax.broadcasted_iota(jnp.int32, (BLOCK, 2 * BLOCK), 0)
    col2 = lax.broadcasted_iota(jnp.int32, (BLOCK, 2 * BLOCK), 1)
    band = jnp.logical_and(col2 >= row2, col2 <= row2 + BLOCK)
    bias_band = jnp.where(band, 0.0, NEG_INF)
    bias_first = jnp.where(jnp.logical_and(band, col2 >= BLOCK), 0.0, NEG_INF)

    kd[0:BLOCK, :] = jnp.zeros((BLOCK, LANES), BF16)
    v0[0:BLOCK, :] = jnp.zeros((BLOCK, LANES), BF16)
    v1[0:BLOCK, :] = jnp.zeros((BLOCK, LANES), BF16)

    order = sorted(range(len(DILATIONS)), key=lambda p: (DILATIONS[p][1] == 1, DILATIONS[p][1]))
    for slot in order:
        window, d = DILATIONS[slot]
        assert window // d == BLOCK
        length = seq // d
        nb = length // BLOCK
        lo_l = lax.broadcasted_iota(jnp.int32, (length, LANES), 1) < HEAD_DIM
        for r in range(d):
            qdst = slice(r * length, (r + 1) * length)
            kdst = slice(BLOCK + r * length, BLOCK + (r + 1) * length)
            if d == DIL_STAGE * DIL_STAGE:
                r4, j = r % DIL_STAGE, r // DIL_STAGE
                start = r4 * (seq // DIL_STAGE) + j
                qx, kx, vx = (_rows(t, start, length, DIL_STAGE) for t in (tq_ref, tk_ref, tv_ref))
            else:
                qx, kx, vx = (_rows(t.at[0], r, length, d) for t in (q_ref, k_ref, v_ref))
            if d == DIL_STAGE:
                tq_ref[qdst, :] = qx
                tk_ref[qdst, :] = kx
                tv_ref[qdst, :] = vx
            qf = qx.astype(BF16)
            q0[qdst, :] = jnp.where(lo_l, qf, jnp.zeros_like(qf))
            q1[qdst, :] = jnp.where(lo_l, jnp.zeros_like(qf), qf)
            kd[kdst, :] = kx.astype(BF16)
            vf = vx.astype(BF16)
            v0[kdst, :] = jnp.where(lo_l, vf, jnp.ones_like(vf))
            v1[kdst, :] = jnp.where(lo_l, jnp.ones_like(vf), vf)

        def body(g, _, slot=slot, d=d, nb=nb):
            qv = ((q0, v0), (q1, v1))
            for u, hh in [(u, hh) for u in range(DIL_GROUP) for hh in range(2)]:
                t = g * DIL_GROUP + u
                base = pl.multiple_of(t * BLOCK, BLOCK)
                if nb == 1:
                    bias = bias_first
                    nat_start = t
                else:
                    r = t // nb
                    n = t - r * nb
                    bias = jnp.where(n == 0, bias_first, bias_band)
                    nat_start = base if d == 1 else n * (BLOCK * d) + r
                s = lax.dot_general(qv[hh][0][pl.ds(base, BLOCK), :], kd[pl.ds(base, 2 * BLOCK), :],
                                    (((1,), (1,)), ((), ())), preferred_element_type=F32) + bias
                mx = jnp.broadcast_to(jnp.max(s, axis=-1, keepdims=True), (BLOCK, LANES))
                e = jnp.concatenate(
                    [jnp.exp(s[:, LANES * t2:LANES * (t2 + 1)] - mx) for t2 in range(2)], axis=1)
                nd = jnp.dot(e.astype(BF16), qv[hh][1][pl.ds(base, 2 * BLOCK), :],
                             preferred_element_type=F32)
                if d == DIL_STAGE * DIL_STAGE:
                    stage = (t % DIL_STAGE) * (seq // DIL_STAGE) + t // DIL_STAGE
                    _set_rows(und_ref.at[hh], stage, BLOCK, DIL_STAGE, nd)
                    _set_rows(um_ref.at[hh], stage, BLOCK, DIL_STAGE, mx)
                else:
                    _set_rows(nd_ref.at[slot, hh], nat_start, BLOCK, d, nd)
                    _set_rows(m_ref.at[slot, hh], nat_start, BLOCK, d, mx)
            return 0

        lax.fori_loop(0, n_blocks // DIL_GROUP, body, 0)
        if d == DIL_STAGE * DIL_STAGE:
            for hh in range(2):
                for r4 in range(DIL_STAGE):
                    src = slice(r4 * (seq // DIL_STAGE), (r4 + 1) * (seq // DIL_STAGE))
                    _set_rows(nd_ref.at[slot, hh], r4, seq // DIL_STAGE, DIL_STAGE, und_ref[hh, src, :])
                    _set_rows(m_ref.at[slot, hh], r4, seq // DIL_STAGE, DIL_STAGE, um_ref[hh, src, :])

    def merge(c, _):
        rs = pl.ds(pl.multiple_of(c * BLOCK, BLOCK), BLOCK)
        outs = []
        for hh in range(2):
            ms = [m_ref[p, hh, rs, :] for p in range(len(DILATIONS))]
            m_all = jnp.maximum(jnp.maximum(ms[0], ms[1]), ms[2])
            tot = nd_ref[0, hh, rs, :] * jnp.exp(ms[0] - m_all)
            for p in range(1, len(DILATIONS)):
                tot = tot + nd_ref[p, hh, rs, :] * jnp.exp(ms[p] - m_all)
            outs.append(tot / pltpu.roll(tot, HEAD_DIM, 1))
        o = jnp.where(lo, outs[0], outs[1]) * g_ref[0, rs, :].astype(F32)
        o_ref[0, rs, :] = o.astype(BF16)
        return 0

    lax.fori_loop(0, n_blocks, merge, 0, unroll=4)


def _dilated(od3, odg3):
    batch, seq, _ = od3.shape
    kernel = functools.partial(_dil_kernel, seq=seq)
    n_pat = len(DILATIONS)
    return pl.pallas_call(
        kernel,
        grid=(batch, HEAD_PAIRS),
        in_specs=[
            pl.BlockSpec((1, seq, LANES), lambda b, p: (b, 0, p)),
            pl.BlockSpec((1, seq, LANES), lambda b, p: (b, 0, HEAD_PAIRS + p)),
            pl.BlockSpec((1, seq, LANES), lambda b, p: (b, 0, 2 * HEAD_PAIRS + p)),
            pl.BlockSpec((1, seq, LANES), lambda b, p: (b, 0, p)),
        ],
        out_specs=pl.BlockSpec((1, seq, LANES), lambda b, p: (b, 0, p)),
        out_shape=jax.ShapeDtypeStruct((batch, seq, DIL_W), BF16),
        scratch_shapes=[
            pltpu.VMEM((seq, LANES), BF16),
            pltpu.VMEM((seq, LANES), BF16),
            pltpu.VMEM((seq + BLOCK, LANES), BF16),
            pltpu.VMEM((seq + BLOCK, LANES), BF16),
            pltpu.VMEM((seq + BLOCK, LANES), BF16),
            pltpu.VMEM((n_pat, 2, seq, LANES), F32),
            pltpu.VMEM((n_pat, 2, seq, LANES), F32),
            pltpu.VMEM((seq, LANES), F32),
            pltpu.VMEM((seq, LANES), F32),
            pltpu.VMEM((seq, LANES), F32),
            pltpu.VMEM((2, seq, LANES), F32),
            pltpu.VMEM((2, seq, LANES), F32),
        ],
        compiler_params=pltpu.CompilerParams(
            dimension_semantics=("arbitrary", "arbitrary"), vmem_limit_bytes=VMEM_LIMIT),
        name="dilated_attn",
    )(od3, od3, od3, odg3)


def _mem_kv_kernel(mem_ref, g_ref, w_ref, o_ref):
    x = mem_ref[0]
    ms = jnp.mean(x * x, axis=-1, keepdims=True)
    h = (x * lax.rsqrt(ms + RMS_EPS) * g_ref[...]).astype(BF16)
    o_ref[0] = jnp.dot(h, w_ref[...], preferred_element_type=F32).astype(BF16)


def _mem_kv(mem, g, w):
    batch, mlen, _ = mem.shape
    return pl.pallas_call(
        _mem_kv_kernel,
        grid=(batch,),
        in_specs=[
            pl.BlockSpec((1, mlen, D_MODEL), lambda b: (b, 0, 0)),
            pl.BlockSpec((1, D_MODEL), lambda b: (0, 0)),
            pl.BlockSpec((D_MODEL, 2 * MEM_W), lambda b: (0, 0)),
        ],
        out_specs=pl.BlockSpec((1, mlen, 2 * MEM_W), lambda b: (b, 0, 0)),
        out_shape=jax.ShapeDtypeStruct((batch, mlen, 2 * MEM_W), BF16),
        compiler_params=pltpu.CompilerParams(
            dimension_semantics=("arbitrary",), vmem_limit_bytes=VMEM_LIMIT),
        name="mem_kv",
    )(mem, g, w)


def _mem_attn_kernel(q_ref, g_ref, mk_ref, mv_ref, o_ref):
    tq = q_ref.shape[1]
    chains = [(h, rc) for h in range(MEM_HEADS) for rc in range(tq // BLOCK)]

    def scores(c):
        h, rc = chains[c]
        ls = slice(MEM_HEAD_DIM * h, MEM_HEAD_DIM * (h + 1))
        return lax.dot_general(q_ref[0, BLOCK * rc:BLOCK * (rc + 1), ls], mk_ref[0, :, ls],
                               (((1,), (1,)), ((), ())), preferred_element_type=F32)

    pending = [scores(c) for c in range(FOX_LOOKAHEAD)]
    for c, (h, rc) in enumerate(chains):
        if c + FOX_LOOKAHEAD < len(chains):
            pending.append(scores(c + FOX_LOOKAHEAD))
        ls = slice(MEM_HEAD_DIM * h, MEM_HEAD_DIM * (h + 1))
        rs = slice(BLOCK * rc, BLOCK * (rc + 1))
        s = pending.pop(0) * MEM_SCALE
        mx = jnp.max(s, axis=-1, keepdims=True)
        e = jnp.exp(s - mx)
        den = jnp.sum(e, axis=-1, keepdims=True)
        o = jnp.dot(e.astype(BF16), mv_ref[0, :, ls], preferred_element_type=F32) / den
        o_ref[0, rs, ls] = (o * g_ref[0, rs, ls].astype(F32)).astype(BF16)


def _mem_attn(om3, mkv, *, tq=2048):
    batch, seq, _ = om3.shape
    mlen = mkv.shape[1]
    return pl.pallas_call(
        _mem_attn_kernel,
        grid=(batch, seq // tq),
        in_specs=[
            pl.BlockSpec((1, tq, MEM_W), lambda b, i: (b, i, 0)),
            pl.BlockSpec((1, tq, MEM_W), lambda b, i: (b, i, 1)),
            pl.BlockSpec((1, mlen, MEM_W), lambda b, i: (b, 0, 0)),
            pl.BlockSpec((1, mlen, MEM_W), lambda b, i: (b, 0, 1)),
        ],
        out_specs=pl.BlockSpec((1, tq, MEM_W), lambda b, i: (b, i, 0)),
        out_shape=jax.ShapeDtypeStruct((batch, seq, MEM_W), BF16),
        compiler_params=pltpu.CompilerParams(
            dimension_semantics=("arbitrary", "arbitrary"), vmem_limit_bytes=VMEM_LIMIT),
        name="mem_attn",
    )(om3, om3, mkv, mkv)


def _out_proj_kernel(x_ref, yf_ref, yd_ref, ym_ref, w_ref, g_ref, o_ref, *, final_norm):
    z = x_ref[...]
    z = z + jnp.dot(yf_ref[...], w_ref[0:FOX_W, :], preferred_element_type=F32)
    z = z + jnp.dot(yd_ref[...], w_ref[FOX_W:FOX_W + DIL_W, :], preferred_element_type=F32)
    z = z + jnp.dot(ym_ref[...], w_ref[FOX_W + DIL_W:MIX_W, :], preferred_element_type=F32)
    if final_norm:
        ms = jnp.mean(z * z, axis=-1, keepdims=True)
        z = z * lax.rsqrt(ms + RMS_EPS) * g_ref[...]
    o_ref[...] = z


def _out_proj(x2, yf, yd, ym, w, g, *, final_norm, tm=512):
    n = x2.shape[0]
    row = lambda i: (i, 0)
    const = lambda i: (0, 0)
    kernel = functools.partial(_out_proj_kernel, final_norm=final_norm)
    return pl.pallas_call(
        kernel,
        grid=(n // tm,),
        in_specs=[
            pl.BlockSpec((tm, D_MODEL), row),
            pl.BlockSpec((tm, FOX_W), row),
            pl.BlockSpec((tm, DIL_W), row),
            pl.BlockSpec((tm, MEM_W), row),
            pl.BlockSpec((MIX_W, D_MODEL), const),
            pl.BlockSpec((1, D_MODEL), const),
        ],
        out_specs=pl.BlockSpec((tm, D_MODEL), row),
        out_shape=jax.ShapeDtypeStruct((n, D_MODEL), F32),
        compiler_params=pltpu.CompilerParams(
            dimension_semantics=("arbitrary",), vmem_limit_bytes=VMEM_LIMIT),
        name="out_proj",
    )(x2, yf, yd, ym, w, g)


def _rope_tables(seq):
    half = ROPE_DIM // 2
    pos = jnp.arange(seq, dtype=F32)
    inv_freq = 1.0 / (ROPE_THETA ** (jnp.arange(0, ROPE_DIM, 2, dtype=F32) / ROPE_DIM))
    ang = pos[:, None] * inv_freq[None, :]
    cos = jnp.cos(ang)
    sin = jnp.sin(ang)
    ones = jnp.ones((seq, HEAD_DIM - ROPE_DIM), F32)
    zeros = jnp.zeros((seq, HEAD_DIM - ROPE_DIM), F32)
    zh = jnp.zeros((seq, half), F32)
    cos_h = jnp.concatenate([cos, cos, ones], axis=1)
    sina_h = jnp.concatenate([zh, sin, zeros], axis=1)
    sinb_h = jnp.concatenate([-sin, zh, zeros], axis=1)
    tile = lambda t: jnp.concatenate([t, t], axis=1)
    return tile(cos_h), tile(sina_h), tile(sinb_h)


def _split_w_in(w):
    sizes = [FOX_W] * 4 + [FOX_HEADS] + [DIL_W] * 4 + [MEM_W] * 2
    offs = [0]
    for s in sizes:
        offs.append(offs[-1] + s)
    cols = [w[:, offs[k]:offs[k + 1]] for k in range(len(sizes))]
    fq, fk, fv, fg, flog, dq, dk, dv, dg, mq, mg = cols
    w_main = jnp.concatenate([fq, fk, fv, fg, dq, dk, dv, dg, mq, mg], axis=1).astype(BF16)
    w_fl = jnp.pad(flog, ((0, 0), (0, LANES - FOX_HEADS))).astype(BF16)
    return w_main, w_fl


def kernel(x, mem, norm_g, w_in, b_forget, mem_norm_g, w_mem_kv, w_out, final_norm_g):
    batch, seq, _ = x.shape
    depth = norm_g.shape[0]
    cos_t, sina_t, sinb_t = _rope_tables(seq)
    x2 = x.reshape(batch * seq, D_MODEL)
    for l in range(depth):
        w_main, w_fl = _split_w_in(w_in[l])
        b_fl = jnp.pad(b_forget[l], (0, LANES - FOX_HEADS)).reshape(1, LANES)
        of, od, odg, om = _in_proj(
            x2, norm_g[l].reshape(1, D_MODEL), w_main, w_fl, b_fl, cos_t, sina_t, sinb_t, seq=seq)
        yf = _fox(of.reshape(batch, seq, FOX_OUT_W))
        yd = _dilated(od.reshape(batch, seq, 2304), odg.reshape(batch, seq, DIL_W))
        mkv = _mem_kv(mem, mem_norm_g[l].reshape(1, D_MODEL), w_mem_kv[l].astype(BF16))
        ym = _mem_attn(om.reshape(batch, seq, 1024), mkv)
        last = l == depth - 1
        x2 = _out_proj(x2, yf.reshape(batch * seq, FOX_W), yd.reshape(batch * seq, DIL_W),
                       ym.reshape(batch * seq, MEM_W), w_out[l].astype(BF16),
                       final_norm_g.reshape(1, D_MODEL), final_norm=last)
    return x2.reshape(batch, seq, D_MODEL)
```

```python
import functools
import math

import jax
import jax.numpy as jnp
from jax import lax
from jax.experimental import pallas as pl
from jax.experimental.pallas import tpu as pltpu

F32 = jnp.float32
BF16 = jnp.bfloat16

D_MODEL = 1024
HEAD_DIM = 64
FOX_HEADS = 12
DIL_HEADS = 12
MEM_HEADS = 4
MEM_HEAD_DIM = 128
FOX_W = FOX_HEADS * HEAD_DIM
DIL_W = DIL_HEADS * HEAD_DIM
MEM_W = MEM_HEADS * MEM_HEAD_DIM
MIX_W = FOX_W + DIL_W + MEM_W
DILATIONS = ((128, 1), (512, 4), (2048, 16))
BLOCK = 128
ROPE_THETA = 500000.0
ROPE_DIM = HEAD_DIM // 4
RMS_EPS = 1e-6
NEG_INF = -1e30
QK_SCALE = 1.0 / math.sqrt(HEAD_DIM)
MEM_SCALE = 1.0 / math.sqrt(MEM_HEAD_DIM)

LANES = 128
HEAD_PAIRS = FOX_HEADS // 2
FOX_AUG_W = FOX_HEADS * LANES
FOX_OUT_W = 3 * FOX_AUG_W + FOX_W
FOX_ROWS = 128
FOX_KEYS = 256
FOX_LOOKAHEAD = 2
DIL_GROUP = 16
DIL_STAGE = 4

_SEG = dict(fq=0, fk=768, fv=1536, fg=2304, dq=3072, dk=3840, dv=4608, dg=5376, mq=6144, mg=6656)
MAIN_W = 7168

VMEM_LIMIT = 56 * 1024 * 1024


def _silu(r):
    return r * (1.0 / (1.0 + jnp.exp(-r)))


def _rows(ref, start, size, stride):
    if stride == 1:
        return ref[pl.ds(start, size), :]
    return ref[pl.ds(start, size, stride=stride), :]


def _set_rows(ref, start, size, stride, val):
    if stride == 1:
        ref[pl.ds(start, size), :] = val
    else:
        ref[pl.ds(start, size, stride=stride), :] = val


def _in_proj_kernel(x_ref, g_ref, w_ref, wfl_ref, bf_ref, cos_ref, sina_ref, sinb_ref,
                    of_ref, od_ref, odg_ref, om_ref, carry_ref,
                    *, tm, tiles_per_seq):
    i = pl.program_id(0)
    x = x_ref[...]
    ms = jnp.mean(x * x, axis=-1, keepdims=True)
    h = (x * lax.rsqrt(ms + RMS_EPS) * g_ref[...]).astype(BF16)

    def seg(name, width):
        off = _SEG[name]
        return jnp.dot(h, w_ref[:, off:off + width], preferred_element_type=F32)

    fl = jnp.dot(h, wfl_ref[...], preferred_element_type=F32) + bf_ref[...]
    logf = jnp.minimum(fl, 0.0) - jnp.log1p(jnp.exp(-jnp.abs(fl)))
    a1 = logf.astype(BF16)
    r1 = logf - a1.astype(F32)
    a2 = r1.astype(BF16)
    a3 = (r1 - a2.astype(F32)).astype(BF16)
    row = lax.broadcasted_iota(jnp.int32, (tm, tm), 0)
    col = lax.broadcasted_iota(jnp.int32, (tm, tm), 1)
    tri = jnp.where(row >= col, 1.0, 0.0).astype(BF16)

    @pl.when(i % tiles_per_seq == 0)
    def _():
        carry_ref[...] = jnp.zeros_like(carry_ref)

    c = (jnp.dot(tri, a1, preferred_element_type=F32)
         + jnp.dot(tri, a2, preferred_element_type=F32)
         + jnp.dot(tri, a3, preferred_element_type=F32)) + carry_ref[...]
    carry_ref[...] = c[tm - 1:tm, :]

    c1 = c.astype(BF16).astype(F32)
    c2 = (c - c1).astype(BF16).astype(F32)
    c3 = (c - c1 - c2).astype(BF16).astype(F32)
    fq = seg("fq", 768) * QK_SCALE
    fk = seg("fk", 768)
    fv = seg("fv", 768)
    lane = lax.broadcasted_iota(jnp.int32, (tm, LANES), 1)
    for hd in range(FOX_HEADS):
        pair = slice(LANES * (hd // 2), LANES * (hd // 2 + 1))
        own = (lane < HEAD_DIM) if hd % 2 == 0 else (lane >= HEAD_DIM)
        base = HEAD_DIM if hd % 2 == 0 else 0
        b1, b2, b3 = (jnp.broadcast_to(t[:, hd:hd + 1], (tm, LANES)) for t in (c1, c2, c3))
        ones_q = jnp.where(jnp.logical_and(lane >= base, lane < base + 3), 1.0, 0.0)
        ones_k = jnp.where(jnp.logical_and(lane >= base + 3, lane < base + 6), 1.0, 0.0)
        q_bias = jnp.where(lane == base + 3, b1, jnp.where(lane == base + 4, b2,
                           jnp.where(lane == base + 5, b3, ones_q)))
        k_bias = jnp.where(lane == base, -b1, jnp.where(lane == base + 1, -b2,
                           jnp.where(lane == base + 2, -b3, ones_k)))
        dst = LANES * hd
        of_ref[:, dst:dst + LANES] = jnp.where(own, fq[:, pair], q_bias).astype(BF16)
        of_ref[:, FOX_AUG_W + dst:FOX_AUG_W + dst + LANES] = \
            jnp.where(own, fk[:, pair], k_bias).astype(BF16)
        of_ref[:, 2 * FOX_AUG_W + dst:2 * FOX_AUG_W + dst + LANES] = \
            jnp.where(own, fv[:, pair], 1.0).astype(BF16)
    of_ref[:, 3 * FOX_AUG_W:3 * FOX_AUG_W + FOX_W] = _silu(seg("fg", 768)).astype(BF16)

    cos = cos_ref[...]
    sina = sina_ref[...]
    sinb = sinb_ref[...]

    def rope_store(r, dst_off, scale):
        for j in range(DIL_W // LANES):
            c = r[:, LANES * j:LANES * (j + 1)]
            rot = c * cos + pltpu.roll(c, ROPE_DIM // 2, 1) * sina \
                + pltpu.roll(c, LANES - ROPE_DIM // 2, 1) * sinb
            if scale != 1.0:
                rot = rot * scale
            od_ref[:, dst_off + LANES * j:dst_off + LANES * (j + 1)] = rot

    rope_store(seg("dq", 768), 0, QK_SCALE)
    rope_store(seg("dk", 768), 768, 1.0)
    od_ref[:, 1536:2304] = seg("dv", 768)
    odg_ref[...] = _silu(seg("dg", 768)).astype(BF16)

    om_ref[:, 0:512] = seg("mq", 512).astype(BF16)
    om_ref[:, 512:1024] = _silu(seg("mg", 512)).astype(BF16)


def _in_proj(x2, g, w_main, w_fl, b_fl, cos_t, sina_t, sinb_t, *, seq, tm=256):
    n = x2.shape[0]
    tiles_per_seq = seq // tm
    const = lambda i: (0, 0)
    row = lambda i: (i, 0)
    pos = lambda i: (i % tiles_per_seq, 0)
    kernel = functools.partial(_in_proj_kernel, tm=tm, tiles_per_seq=tiles_per_seq)
    return pl.pallas_call(
        kernel,
        grid=(n // tm,),
        in_specs=[
            pl.BlockSpec((tm, D_MODEL), row),
            pl.BlockSpec((1, D_MODEL), const),
            pl.BlockSpec((D_MODEL, MAIN_W), const),
            pl.BlockSpec((D_MODEL, LANES), const),
            pl.BlockSpec((1, LANES), const),
            pl.BlockSpec((tm, LANES), pos),
            pl.BlockSpec((tm, LANES), pos),
            pl.BlockSpec((tm, LANES), pos),
        ],
        out_specs=[
            pl.BlockSpec((tm, FOX_OUT_W), row),
            pl.BlockSpec((tm, 2304), row),
            pl.BlockSpec((tm, 768), row),
            pl.BlockSpec((tm, 1024), row),
        ],
        out_shape=[
            jax.ShapeDtypeStruct((n, FOX_OUT_W), BF16),
            jax.ShapeDtypeStruct((n, 2304), F32),
            jax.ShapeDtypeStruct((n, 768), BF16),
            jax.ShapeDtypeStruct((n, 1024), BF16),
        ],
        scratch_shapes=[pltpu.VMEM((1, LANES), F32)],
        compiler_params=pltpu.CompilerParams(
            dimension_semantics=("arbitrary",), vmem_limit_bytes=VMEM_LIMIT),
        name="in_proj",
    )(x2, g, w_main, w_fl, b_fl, cos_t, sina_t, sinb_t)


def _fox_kernel(q_ref, k_ref, v_ref, g_ref, o_ref, acc_ref, m_ref, *, tq):
    i = pl.program_id(1)
    tk = FOX_KEYS
    row_chunks = tq // FOX_ROWS
    lane = lax.broadcasted_iota(jnp.int32, (FOX_ROWS, LANES), 1)
    lo = lane < HEAD_DIM
    row = lax.broadcasted_iota(jnp.int32, (FOX_ROWS, tk), 0)
    col = lax.broadcasted_iota(jnp.int32, (FOX_ROWS, tk), 1)
    bias_lo = jnp.where(row >= col, 0.0, NEG_INF)
    bias_hi = jnp.where(row + FOX_ROWS >= col, 0.0, NEG_INF)

    def run_chains(chains, first_visit=False):
        def scores(c):
            hd, rc, koff, klen, _ = chains[c]
            ls = slice(LANES * hd, LANES * (hd + 1))
            rs = slice(FOX_ROWS * rc, FOX_ROWS * (rc + 1))
            return lax.dot_general(q_ref[0, rs, ls], k_ref[0, pl.ds(koff, klen), ls],
                                   (((1,), (1,)), ((), ())), preferred_element_type=F32)

        pending = [scores(c) for c in range(FOX_LOOKAHEAD)]
        for c, (hd, rc, koff, klen, bias) in enumerate(chains):
            if c + FOX_LOOKAHEAD < len(chains):
                pending.append(scores(c + FOX_LOOKAHEAD))
            s = pending.pop(0)
            ls = slice(LANES * hd, LANES * (hd + 1))
            rs = slice(FOX_ROWS * rc, FOX_ROWS * (rc + 1))
            if bias is not None:
                s = s + bias
            m_new = jnp.broadcast_to(jnp.max(s, axis=-1, keepdims=True), (FOX_ROWS, LANES))
            if not first_visit:
                m_old = m_ref[hd, rs, :]
                m_new = jnp.maximum(m_old, m_new)
            p = jnp.concatenate(
                [jnp.exp(s[:, LANES * t:LANES * (t + 1)] - m_new) for t in range(klen // LANES)],
                axis=1)
            pv = jnp.dot(p.astype(BF16), v_ref[0, pl.ds(koff, klen), ls],
                         preferred_element_type=F32)
            if first_visit:
                acc_ref[hd, rs, :] = pv
            else:
                acc_ref[hd, rs, :] = jnp.exp(m_old - m_new) * acc_ref[hd, rs, :] + pv
            m_ref[hd, rs, :] = m_new

    def body(j, _):
        first = pl.multiple_of(j * tq, tq)
        run_chains([(hd, rc, first + kt * tk, tk, None)
                    for kt in range(tq // tk) for hd in range(FOX_HEADS) for rc in range(row_chunks)])
        return 0

    def diag_body(t, _, masked):
        first = pl.multiple_of(t * tq, tq)
        chains = []
        for kt in range(tq // tk):
            koff = first + kt * tk
            for hd in range(FOX_HEADS):
                if masked:
                    chains += [(hd, rc, koff, tk, bias_lo if rc % 2 == 0 else bias_hi)
                               for rc in (2 * kt, 2 * kt + 1)]
                else:
                    chains += [(hd, rc, koff, tk, None) for rc in range(2 * kt + 2, row_chunks)]
        run_chains(chains, first_visit=masked)
        return 0

    lax.fori_loop(i, i + 1, functools.partial(diag_body, masked=True), 0)
    lax.fori_loop(i, i + 1, functools.partial(diag_body, masked=False), 0)
    lax.fori_loop(0, i, body, 0)

    for hp in range(HEAD_PAIRS):
        ls = slice(LANES * hp, LANES * (hp + 1))
        for rc in range(row_chunks):
            rs = slice(FOX_ROWS * rc, FOX_ROWS * (rc + 1))
            a0 = acc_ref[2 * hp, rs, :]
            a1 = acc_ref[2 * hp + 1, rs, :]
            dens = pltpu.roll(jnp.where(lo, a1, a0), HEAD_DIM, 1)
            o2 = jnp.where(lo, a0, a1) / dens
            o_ref[0, rs, ls] = (o2 * g_ref[0, rs, ls].astype(F32)).astype(BF16)


def _fox(of3, *, tq=512):
    batch, seq, _ = of3.shape
    kernel = functools.partial(_fox_kernel, tq=tq)
    gate_blk = 3 * FOX_AUG_W // FOX_W
    return pl.pallas_call(
        kernel,
        grid=(batch, seq // tq),
        in_specs=[
            pl.BlockSpec((1, tq, FOX_AUG_W), lambda b, i: (b, i, 0)),
            pl.BlockSpec((1, seq, FOX_AUG_W), lambda b, i: (b, 0, 1)),
            pl.BlockSpec((1, seq, FOX_AUG_W), lambda b, i: (b, 0, 2)),
            pl.BlockSpec((1, tq, FOX_W), lambda b, i: (b, i, gate_blk)),
        ],
        out_specs=pl.BlockSpec((1, tq, FOX_W), lambda b, i: (b, i, 0)),
        out_shape=jax.ShapeDtypeStruct((batch, seq, FOX_W), BF16),
        scratch_shapes=[
            pltpu.VMEM((FOX_HEADS, tq, LANES), F32),
            pltpu.VMEM((FOX_HEADS, tq, LANES), F32),
        ],
        compiler_params=pltpu.CompilerParams(
            dimension_semantics=("arbitrary", "arbitrary"), vmem_limit_bytes=VMEM_LIMIT),
        name="fox_attn",
    )(of3, of3, of3, of3)


def _dil_kernel(q_ref, k_ref, v_ref, g_ref, o_ref, q0, q1, kd, v0, v1, nd_ref, m_ref,
                tq_ref, tk_ref, tv_ref, und_ref, um_ref, *, seq):
    n_blocks = seq // BLOCK
    lane = lax.broadcasted_iota(jnp.int32, (BLOCK, LANES), 1)
    lo = lane < HEAD_DIM
    row2 = lax.broadcasted_iota(jnp.int32, (BLOCK, 2 * BLOCK), 0)
    col2 = lax.broadcasted_iota(jnp.int32, (BLOCK, 2 * BLOCK), 1)
    band = jnp.logical_and(col2 >= row2, col2 <= row2 + BLOCK)
    bias_band = jnp.where(band, 0.0, NEG_INF)
    bias_first = jnp.where(jnp.logical_and(band, col2 >= BLOCK), 0.0, NEG_INF)

    kd[0:BLOCK, :] = jnp.zeros((BLOCK, LANES), BF16)
    v0[0:BLOCK, :] = jnp.zeros((BLOCK, LANES), BF16)
    v1[0:BLOCK, :] = jnp.zeros((BLOCK, LANES), BF16)

    order = sorted(range(len(DILATIONS)), key=lambda p: (DILATIONS[p][1] == 1, DILATIONS[p][1]))
    for slot in order:
        window, d = DILATIONS[slot]
        assert window // d == BLOCK
        length = seq // d
        nb = length // BLOCK
        lo_l = lax.broadcasted_iota(jnp.int32, (length, LANES), 1) < HEAD_DIM
        for r in range(d):
            qdst = slice(r * length, (r + 1) * length)
            kdst = slice(BLOCK + r * length, BLOCK + (r + 1) * length)
            if d == DIL_STAGE * DIL_STAGE:
                r4, j = r % DIL_STAGE, r // DIL_STAGE
                start = r4 * (seq // DIL_STAGE) + j
                qx, kx, vx = (_rows(t, start, length, DIL_STAGE) for t in (tq_ref, tk_ref, tv_ref))
            else:
                qx, kx, vx = (_rows(t.at[0], r, length, d) for t in (q_ref, k_ref, v_ref))
            if d == DIL_STAGE:
                tq_ref[qdst, :] = qx
                tk_ref[qdst, :] = kx
                tv_ref[qdst, :] = vx
            qf = qx.astype(BF16)
            q0[qdst, :] = jnp.where(lo_l, qf, jnp.zeros_like(qf))
            q1[qdst, :] = jnp.where(lo_l, jnp.zeros_like(qf), qf)
            kd[kdst, :] = kx.astype(BF16)
            vf = vx.astype(BF16)
            v0[kdst, :] = jnp.where(lo_l, vf, jnp.ones_like(vf))
            v1[kdst, :] = jnp.where(lo_l, jnp.ones_like(vf), vf)

        def body(g, _, slot=slot, d=d, nb=nb):
            qv = ((q0, v0), (q1, v1))
            for u, hh in [(u, hh) for u in range(DIL_GROUP) for hh in range(2)]:
                t = g * DIL_GROUP + u
                base = pl.multiple_of(t * BLOCK, BLOCK)
                if nb == 1:
                    bias = bias_first
                    nat_start = t
                else:
                    r = t // nb
                    n = t - r * nb
                    bias = jnp.where(n == 0, bias_first, bias_band)
                    nat_start = base if d == 1 else n * (BLOCK * d) + r
                s = lax.dot_general(qv[hh][0][pl.ds(base, BLOCK), :], kd[pl.ds(base, 2 * BLOCK), :],
                                    (((1,), (1,)), ((), ())), preferred_element_type=F32) + bias
                mx = jnp.broadcast_to(jnp.max(s, axis=-1, keepdims=True), (BLOCK, LANES))
                e = jnp.concatenate(
                    [jnp.exp(s[:, LANES * t2:LANES * (t2 + 1)] - mx) for t2 in range(2)], axis=1)
                nd = jnp.dot(e.astype(BF16), qv[hh][1][pl.ds(base, 2 * BLOCK), :],
                             preferred_element_type=F32)
                if d == DIL_STAGE * DIL_STAGE:
                    stage = (t % DIL_STAGE) * (seq // DIL_STAGE) + t // DIL_STAGE
                    _set_rows(und_ref.at[hh], stage, BLOCK, DIL_STAGE, nd)
                    _set_rows(um_ref.at[hh], stage, BLOCK, DIL_STAGE, mx)
                else:
                    _set_rows(nd_ref.at[slot, hh], nat_start, BLOCK, d, nd)
                    _set_rows(m_ref.at[slot, hh], nat_start, BLOCK, d, mx)
            return 0

        lax.fori_loop(0, n_blocks // DIL_GROUP, body, 0)
        if d == DIL_STAGE * DIL_STAGE:
            for hh in range(2):
                for r4 in range(DIL_STAGE):
                    src = slice(r4 * (seq // DIL_STAGE), (r4 + 1) * (seq // DIL_STAGE))
                    _set_rows(nd_ref.at[slot, hh], r4, seq // DIL_STAGE, DIL_STAGE, und_ref[hh, src, :])
                    _set_rows(m_ref.at[slot, hh], r4, seq // DIL_STAGE, DIL_STAGE, um_ref[hh, src, :])

    def merge(c, _):
        rs = pl.ds(pl.multiple_of(c * BLOCK, BLOCK), BLOCK)
        tots = []
        for hh in range(2):
            ms = [m_ref[p, hh, rs, :] for p in range(len(DILATIONS))]
            m_all = jnp.maximum(jnp.maximum(ms[0], ms[1]), ms[2])
            tot = nd_ref[0, hh, rs, :] * jnp.exp(ms[0] - m_all)
            for p in range(1, len(DILATIONS)):
                tot = tot + nd_ref[p, hh, rs, :] * jnp.exp(ms[p] - m_all)
            tots.append(tot)
        dens = pltpu.roll(jnp.where(lo, tots[1], tots[0]), HEAD_DIM, 1)
        o = jnp.where(lo, tots[0], tots[1]) / dens * g_ref[0, rs, :].astype(F32)
        o_ref[0, rs, :] = o.astype(BF16)
        return 0

    lax.fori_loop(0, n_blocks, merge, 0, unroll=4)


def _dilated(od3, odg3):
    batch, seq, _ = od3.shape
    kernel = functools.partial(_dil_kernel, seq=seq)
    n_pat = len(DILATIONS)
    return pl.pallas_call(
        kernel,
        grid=(batch, HEAD_PAIRS),
        in_specs=[
            pl.BlockSpec((1, seq, LANES), lambda b, p: (b, 0, p)),
            pl.BlockSpec((1, seq, LANES), lambda b, p: (b, 0, HEAD_PAIRS + p)),
            pl.BlockSpec((1, seq, LANES), lambda b, p: (b, 0, 2 * HEAD_PAIRS + p)),
            pl.BlockSpec((1, seq, LANES), lambda b, p: (b, 0, p)),
        ],
        out_specs=pl.BlockSpec((1, seq, LANES), lambda b, p: (b, 0, p)),
        out_shape=jax.ShapeDtypeStruct((batch, seq, DIL_W), BF16),
        scratch_shapes=[
            pltpu.VMEM((seq, LANES), BF16),
            pltpu.VMEM((seq, LANES), BF16),
            pltpu.VMEM((seq + BLOCK, LANES), BF16),
            pltpu.VMEM((seq + BLOCK, LANES), BF16),
            pltpu.VMEM((seq + BLOCK, LANES), BF16),
            pltpu.VMEM((n_pat, 2, seq, LANES), F32),
            pltpu.VMEM((n_pat, 2, seq, LANES), F32),
            pltpu.VMEM((seq, LANES), F32),
            pltpu.VMEM((seq, LANES), F32),
            pltpu.VMEM((seq, LANES), F32),
            pltpu.VMEM((2, seq, LANES), F32),
            pltpu.VMEM((2, seq, LANES), F32),
        ],
        compiler_params=pltpu.CompilerParams(
            dimension_semantics=("arbitrary", "arbitrary"), vmem_limit_bytes=VMEM_LIMIT),
        name="dilated_attn",
    )(od3, od3, od3, odg3)


def _mem_kv_kernel(mem_ref, g_ref, w_ref, o_ref):
    x = mem_ref[0]
    ms = jnp.mean(x * x, axis=-1, keepdims=True)
    h = (x * lax.rsqrt(ms + RMS_EPS) * g_ref[...]).astype(BF16)
    o_ref[0] = jnp.dot(h, w_ref[...], preferred_element_type=F32).astype(BF16)


def _mem_kv(mem, g, w):
    batch, mlen, _ = mem.shape
    return pl.pallas_call(
        _mem_kv_kernel,
        grid=(batch,),
        in_specs=[
            pl.BlockSpec((1, mlen, D_MODEL), lambda b: (b, 0, 0)),
            pl.BlockSpec((1, D_MODEL), lambda b: (0, 0)),
            pl.BlockSpec((D_MODEL, 2 * MEM_W), lambda b: (0, 0)),
        ],
        out_specs=pl.BlockSpec((1, mlen, 2 * MEM_W), lambda b: (b, 0, 0)),
        out_shape=jax.ShapeDtypeStruct((batch, mlen, 2 * MEM_W), BF16),
        compiler_params=pltpu.CompilerParams(
            dimension_semantics=("arbitrary",), vmem_limit_bytes=VMEM_LIMIT),
        name="mem_kv",
    )(mem, g, w)


def _mem_attn_kernel(q_ref, g_ref, mk_ref, mv_ref, o_ref):
    tq = q_ref.shape[1]
    chains = [(h, rc) for h in range(MEM_HEADS) for rc in range(tq // BLOCK)]

    def scores(c):
        h, rc = chains[c]
        ls = slice(MEM_HEAD_DIM * h, MEM_HEAD_DIM * (h + 1))
        return lax.dot_general(q_ref[0, BLOCK * rc:BLOCK * (rc + 1), ls], mk_ref[0, :, ls],
                               (((1,), (1,)), ((), ())), preferred_element_type=F32)

    pending = [scores(c) for c in range(FOX_LOOKAHEAD)]
    for c, (h, rc) in enumerate(chains):
        if c + FOX_LOOKAHEAD < len(chains):
            pending.append(scores(c + FOX_LOOKAHEAD))
        ls = slice(MEM_HEAD_DIM * h, MEM_HEAD_DIM * (h + 1))
        rs = slice(BLOCK * rc, BLOCK * (rc + 1))
        s = pending.pop(0) * MEM_SCALE
        mx = jnp.max(s, axis=-1, keepdims=True)
        e = jnp.exp(s - mx)
        den = jnp.sum(e, axis=-1, keepdims=True)
        o = jnp.dot(e.astype(BF16), mv_ref[0, :, ls], preferred_element_type=F32) / den
        o_ref[0, rs, ls] = (o * g_ref[0, rs, ls].astype(F32)).astype(BF16)


def _mem_attn(om3, mkv, *, tq=2048):
    batch, seq, _ = om3.shape
    mlen = mkv.shape[1]
    return pl.pallas_call(
        _mem_attn_kernel,
        grid=(batch, seq // tq),
        in_specs=[
            pl.BlockSpec((1, tq, MEM_W), lambda b, i: (b, i, 0)),
            pl.BlockSpec((1, tq, MEM_W), lambda b, i: (b, i, 1)),
            pl.BlockSpec((1, mlen, MEM_W), lambda b, i: (b, 0, 0)),
            pl.BlockSpec((1, mlen, MEM_W), lambda b, i: (b, 0, 1)),
        ],
        out_specs=pl.BlockSpec((1, tq, MEM_W), lambda b, i: (b, i, 0)),
        out_shape=jax.ShapeDtypeStruct((batch, seq, MEM_W), BF16),
        compiler_params=pltpu.CompilerParams(
            dimension_semantics=("arbitrary", "arbitrary"), vmem_limit_bytes=VMEM_LIMIT),
        name="mem_attn",
    )(om3, om3, mkv, mkv)


def _out_proj_kernel(x_ref, yf_ref, yd_ref, ym_ref, w_ref, g_ref, o_ref, *, final_norm):
    z = x_ref[...]
    z = z + jnp.dot(yf_ref[...], w_ref[0:FOX_W, :], preferred_element_type=F32)
    z = z + jnp.dot(yd_ref[...], w_ref[FOX_W:FOX_W + DIL_W, :], preferred_element_type=F32)
    z = z + jnp.dot(ym_ref[...], w_ref[FOX_W + DIL_W:MIX_W, :], preferred_element_type=F32)
    if final_norm:
        ms = jnp.mean(z * z, axis=-1, keepdims=True)
        z = z * lax.rsqrt(ms + RMS_EPS) * g_ref[...]
    o_ref[...] = z


def _out_proj(x2, yf, yd, ym, w, g, *, final_norm, tm=512):
    n = x2.shape[0]
    row = lambda i: (i, 0)
    const = lambda i: (0, 0)
    kernel = functools.partial(_out_proj_kernel, final_norm=final_norm)
    return pl.pallas_call(
        kernel,
        grid=(n // tm,),
        in_specs=[
            pl.BlockSpec((tm, D_MODEL), row),
            pl.BlockSpec((tm, FOX_W), row),
            pl.BlockSpec((tm, DIL_W), row),
            pl.BlockSpec((tm, MEM_W), row),
            pl.BlockSpec((MIX_W, D_MODEL), const),
            pl.BlockSpec((1, D_MODEL), const),
        ],
        out_specs=pl.BlockSpec((tm, D_MODEL), row),
        out_shape=jax.ShapeDtypeStruct((n, D_MODEL), F32),
        compiler_params=pltpu.CompilerParams(
            dimension_semantics=("arbitrary",), vmem_limit_bytes=VMEM_LIMIT),
        name="out_proj",
    )(x2, yf, yd, ym, w, g)


def _rope_tables(seq):
    half = ROPE_DIM // 2
    pos = jnp.arange(seq, dtype=F32)
    inv_freq = 1.0 / (ROPE_THETA ** (jnp.arange(0, ROPE_DIM, 2, dtype=F32) / ROPE_DIM))
    ang = pos[:, None] * inv_freq[None, :]
    cos = jnp.cos(ang)
    sin = jnp.sin(ang)
    ones = jnp.ones((seq, HEAD_DIM - ROPE_DIM), F32)
    zeros = jnp.zeros((seq, HEAD_DIM - ROPE_DIM), F32)
    zh = jnp.zeros((seq, half), F32)
    cos_h = jnp.concatenate([cos, cos, ones], axis=1)
    sina_h = jnp.concatenate([zh, sin, zeros], axis=1)
    sinb_h = jnp.concatenate([-sin, zh, zeros], axis=1)
    tile = lambda t: jnp.concatenate([t, t], axis=1)
    return tile(cos_h), tile(sina_h), tile(sinb_h)


def _split_w_in(w):
    sizes = [FOX_W] * 4 + [FOX_HEADS] + [DIL_W] * 4 + [MEM_W] * 2
    offs = [0]
    for s in sizes:
        offs.append(offs[-1] + s)
    wt = jnp.swapaxes(w, 0, 1).astype(BF16)
    rows = [wt[offs[k]:offs[k + 1]] for k in range(len(sizes))]
    fq, fk, fv, fg, flog, dq, dk, dv, dg, mq, mg = rows
    w_main = jnp.concatenate([fq, fk, fv, fg, dq, dk, dv, dg, mq, mg], axis=0).T
    w_fl = jnp.pad(flog, ((0, LANES - FOX_HEADS), (0, 0))).T
    return w_main, w_fl


def kernel(x, mem, norm_g, w_in, b_forget, mem_norm_g, w_mem_kv, w_out, final_norm_g):
    batch, seq, _ = x.shape
    depth = norm_g.shape[0]
    cos_t, sina_t, sinb_t = _rope_tables(seq)
    x2 = x.reshape(batch * seq, D_MODEL)
    for l in range(depth):
        w_main, w_fl = _split_w_in(w_in[l])
        b_fl = jnp.pad(b_forget[l], (0, LANES - FOX_HEADS)).reshape(1, LANES)
        of, od, odg, om = _in_proj(
            x2, norm_g[l].reshape(1, D_MODEL), w_main, w_fl, b_fl, cos_t, sina_t, sinb_t, seq=seq)
        yf = _fox(of.reshape(batch, seq, FOX_OUT_W))
        yd = _dilated(od.reshape(batch, seq, 2304), odg.reshape(batch, seq, DIL_W))
        mkv = _mem_kv(mem, mem_norm_g[l].reshape(1, D_MODEL), w_mem_kv[l].astype(BF16))
        ym = _mem_attn(om.reshape(batch, seq, 1024), mkv)
        last = l == depth - 1
        x2 = _out_proj(x2, yf.reshape(batch * seq, FOX_W), yd.reshape(batch * seq, DIL_W),
                       ym.reshape(batch * seq, MEM_W), w_out[l].astype(BF16),
                       final_norm_g.reshape(1, D_MODEL), final_norm=last)
    return x2.reshape(batch, seq, D_MODEL)
```

```python
import functools
import math

import jax
import jax.numpy as jnp
from jax import lax
from jax.experimental import pallas as pl
from jax.experimental.pallas import tpu as pltpu

F32 = jnp.float32
BF16 = jnp.bfloat16

D_MODEL = 1024
HEAD_DIM = 64
FOX_HEADS = 12
DIL_HEADS = 12
MEM_HEADS = 4
MEM_HEAD_DIM = 128
FOX_W = FOX_HEADS * HEAD_DIM
DIL_W = DIL_HEADS * HEAD_DIM
MEM_W = MEM_HEADS * MEM_HEAD_DIM
MIX_W = FOX_W + DIL_W + MEM_W
DILATIONS = ((128, 1), (512, 4), (2048, 16))
BLOCK = 128
ROPE_THETA = 500000.0
ROPE_DIM = HEAD_DIM // 4
RMS_EPS = 1e-6
NEG_INF = -1e30
QK_SCALE = 1.0 / math.sqrt(HEAD_DIM)
MEM_SCALE = 1.0 / math.sqrt(MEM_HEAD_DIM)

LANES = 128
HEAD_PAIRS = FOX_HEADS // 2
FOX_AUG_W = FOX_HEADS * LANES
FOX_OUT_W = 3 * FOX_AUG_W + FOX_W
FOX_ROWS = 128
FOX_KEYS = 256
FOX_LOOKAHEAD = 2
DIL_GROUP = 16
DIL_STAGE = 4

_SEG = dict(fq=0, fk=768, fv=1536, fg=2304, dq=3072, dk=3840, dv=4608, dg=5376, mq=6144, mg=6656)
MAIN_W = 7168

VMEM_LIMIT = 56 * 1024 * 1024


def _silu(r):
    return r * (1.0 / (1.0 + jnp.exp(-r)))


def _rows(ref, start, size, stride):
    if stride == 1:
        return ref[pl.ds(start, size), :]
    return ref[pl.ds(start, size, stride=stride), :]


def _set_rows(ref, start, size, stride, val):
    if stride == 1:
        ref[pl.ds(start, size), :] = val
    else:
        ref[pl.ds(start, size, stride=stride), :] = val


def _in_proj_kernel(x_ref, g_ref, w_ref, wfl_ref, bf_ref, cos_ref, sina_ref, sinb_ref,
                    of_ref, od_ref, odg_ref, om_ref, carry_ref,
                    *, tm, tiles_per_seq):
    i = pl.program_id(0)
    x = x_ref[...]
    ms = jnp.mean(x * x, axis=-1, keepdims=True)
    h = (x * lax.rsqrt(ms + RMS_EPS) * g_ref[...]).astype(BF16)

    def seg(name, width):
        off = _SEG[name]
        return jnp.dot(h, w_ref[:, off:off + width], preferred_element_type=F32)

    fl = jnp.dot(h, wfl_ref[...], preferred_element_type=F32) + bf_ref[...]
    logf = jnp.minimum(fl, 0.0) - jnp.log1p(jnp.exp(-jnp.abs(fl)))
    a1 = logf.astype(BF16)
    r1 = logf - a1.astype(F32)
    a2 = r1.astype(BF16)
    a3 = (r1 - a2.astype(F32)).astype(BF16)
    row = lax.broadcasted_iota(jnp.int32, (tm, tm), 0)
    col = lax.broadcasted_iota(jnp.int32, (tm, tm), 1)
    tri = jnp.where(row >= col, 1.0, 0.0).astype(BF16)

    @pl.when(i % tiles_per_seq == 0)
    def _():
        carry_ref[...] = jnp.zeros_like(carry_ref)

    c = (jnp.dot(tri, a1, preferred_element_type=F32)
         + jnp.dot(tri, a2, preferred_element_type=F32)
         + jnp.dot(tri, a3, preferred_element_type=F32)) + carry_ref[...]
    carry_ref[...] = c[tm - 1:tm, :]

    c1 = c.astype(BF16).astype(F32)
    c2 = (c - c1).astype(BF16).astype(F32)
    c3 = (c - c1 - c2).astype(BF16).astype(F32)
    fq = seg("fq", 768) * QK_SCALE
    fk = seg("fk", 768)
    fv = seg("fv", 768)
    lane = lax.broadcasted_iota(jnp.int32, (tm, LANES), 1)
    for hd in range(FOX_HEADS):
        pair = slice(LANES * (hd // 2), LANES * (hd // 2 + 1))
        own = (lane < HEAD_DIM) if hd % 2 == 0 else (lane >= HEAD_DIM)
        base = HEAD_DIM if hd % 2 == 0 else 0
        b1, b2, b3 = (jnp.broadcast_to(t[:, hd:hd + 1], (tm, LANES)) for t in (c1, c2, c3))
        ones_q = jnp.where(jnp.logical_and(lane >= base, lane < base + 3), 1.0, 0.0)
        ones_k = jnp.where(jnp.logical_and(lane >= base + 3, lane < base + 6), 1.0, 0.0)
        q_bias = jnp.where(lane == base + 3, b1, jnp.where(lane == base + 4, b2,
                           jnp.where(lane == base + 5, b3, ones_q)))
        k_bias = jnp.where(lane == base, -b1, jnp.where(lane == base + 1, -b2,
                           jnp.where(lane == base + 2, -b3, ones_k)))
        dst = LANES * hd
        of_ref[:, dst:dst + LANES] = jnp.where(own, fq[:, pair], q_bias).astype(BF16)
        of_ref[:, FOX_AUG_W + dst:FOX_AUG_W + dst + LANES] = \
            jnp.where(own, fk[:, pair], k_bias).astype(BF16)
        of_ref[:, 2 * FOX_AUG_W + dst:2 * FOX_AUG_W + dst + LANES] = \
            jnp.where(own, fv[:, pair], 1.0).astype(BF16)
    of_ref[:, 3 * FOX_AUG_W:3 * FOX_AUG_W + FOX_W] = _silu(seg("fg", 768)).astype(BF16)

    cos = cos_ref[...]
    sina = sina_ref[...]
    sinb = sinb_ref[...]

    def rope_store(r, dst_off, scale):
        for j in range(DIL_W // LANES):
            c = r[:, LANES * j:LANES * (j + 1)]
            rot = c * cos + pltpu.roll(c, ROPE_DIM // 2, 1) * sina \
                + pltpu.roll(c, LANES - ROPE_DIM // 2, 1) * sinb
            if scale != 1.0:
                rot = rot * scale
            od_ref[0, dst_off // LANES + j] = rot

    rope_store(seg("dq", 768), 0, QK_SCALE)
    rope_store(seg("dk", 768), 768, 1.0)
    dv = seg("dv", 768)
    dg = _silu(seg("dg", 768)).astype(BF16)
    for j in range(DIL_W // LANES):
        od_ref[0, 2 * HEAD_PAIRS + j] = dv[:, LANES * j:LANES * (j + 1)]
        odg_ref[0, j] = dg[:, LANES * j:LANES * (j + 1)]

    om_ref[:, 0:512] = seg("mq", 512).astype(BF16)
    om_ref[:, 512:1024] = _silu(seg("mg", 512)).astype(BF16)


def _in_proj(x2, g, w_main, w_fl, b_fl, cos_t, sina_t, sinb_t, *, seq, tm=256):
    n = x2.shape[0]
    tiles_per_seq = seq // tm
    const = lambda i: (0, 0)
    row = lambda i: (i, 0)
    pos = lambda i: (i % tiles_per_seq, 0)
    grp = lambda i: (i // tiles_per_seq, 0, i % tiles_per_seq, 0)
    kernel = functools.partial(_in_proj_kernel, tm=tm, tiles_per_seq=tiles_per_seq)
    return pl.pallas_call(
        kernel,
        grid=(n // tm,),
        in_specs=[
            pl.BlockSpec((tm, D_MODEL), row),
            pl.BlockSpec((1, D_MODEL), const),
            pl.BlockSpec((D_MODEL, MAIN_W), const),
            pl.BlockSpec((D_MODEL, LANES), const),
            pl.BlockSpec((1, LANES), const),
            pl.BlockSpec((tm, LANES), pos),
            pl.BlockSpec((tm, LANES), pos),
            pl.BlockSpec((tm, LANES), pos),
        ],
        out_specs=[
            pl.BlockSpec((tm, FOX_OUT_W), row),
            pl.BlockSpec((1, 3 * HEAD_PAIRS, tm, LANES), grp),
            pl.BlockSpec((1, HEAD_PAIRS, tm, LANES), grp),
            pl.BlockSpec((tm, 1024), row),
        ],
        out_shape=[
            jax.ShapeDtypeStruct((n, FOX_OUT_W), BF16),
            jax.ShapeDtypeStruct((n // seq, 3 * HEAD_PAIRS, seq, LANES), F32),
            jax.ShapeDtypeStruct((n // seq, HEAD_PAIRS, seq, LANES), BF16),
            jax.ShapeDtypeStruct((n, 1024), BF16),
        ],
        scratch_shapes=[pltpu.VMEM((1, LANES), F32)],
        compiler_params=pltpu.CompilerParams(
            dimension_semantics=("arbitrary",), vmem_limit_bytes=VMEM_LIMIT),
        name="in_proj",
    )(x2, g, w_main, w_fl, b_fl, cos_t, sina_t, sinb_t)


def _fox_kernel(q_ref, k_ref, v_ref, g_ref, o_ref, acc_ref, m_ref, *, tq):
    i = pl.program_id(1)
    tk = FOX_KEYS
    row_chunks = tq // FOX_ROWS
    lane = lax.broadcasted_iota(jnp.int32, (FOX_ROWS, LANES), 1)
    lo = lane < HEAD_DIM
    row = lax.broadcasted_iota(jnp.int32, (FOX_ROWS, tk), 0)
    col = lax.broadcasted_iota(jnp.int32, (FOX_ROWS, tk), 1)
    bias_lo = jnp.where(row >= col, 0.0, NEG_INF)
    bias_hi = jnp.where(row + FOX_ROWS >= col, 0.0, NEG_INF)

    def run_chains(chains, first_visit=False):
        def scores(c):
            hd, rc, koff, klen, _ = chains[c]
            ls = slice(LANES * hd, LANES * (hd + 1))
            rs = slice(FOX_ROWS * rc, FOX_ROWS * (rc + 1))
            return lax.dot_general(q_ref[0, rs, ls], k_ref[0, pl.ds(koff, klen), ls],
                                   (((1,), (1,)), ((), ())), preferred_element_type=F32)

        pending = [scores(c) for c in range(FOX_LOOKAHEAD)]
        for c, (hd, rc, koff, klen, bias) in enumerate(chains):
            if c + FOX_LOOKAHEAD < len(chains):
                pending.append(scores(c + FOX_LOOKAHEAD))
            s = pending.pop(0)
            ls = slice(LANES * hd, LANES * (hd + 1))
            rs = slice(FOX_ROWS * rc, FOX_ROWS * (rc + 1))
            if bias is not None:
                s = s + bias
            m_new = jnp.broadcast_to(jnp.max(s, axis=-1, keepdims=True), (FOX_ROWS, LANES))
            if not first_visit:
                m_old = m_ref[hd, rs, :]
                m_new = jnp.maximum(m_old, m_new)
            p = jnp.concatenate(
                [jnp.exp(s[:, LANES * t:LANES * (t + 1)] - m_new) for t in range(klen // LANES)],
                axis=1)
            pv = jnp.dot(p.astype(BF16), v_ref[0, pl.ds(koff, klen), ls],
                         preferred_element_type=F32)
            if first_visit:
                acc_ref[hd, rs, :] = pv
            else:
                acc_ref[hd, rs, :] = jnp.exp(m_old - m_new) * acc_ref[hd, rs, :] + pv
            m_ref[hd, rs, :] = m_new

    def body(j, _):
        first = pl.multiple_of(j * tq, tq)
        run_chains([(hd, rc, first + kt * tk, tk, None)
                    for kt in range(tq // tk) for hd in range(FOX_HEADS) for rc in range(row_chunks)])
        return 0

    def diag_body(t, _, masked):
        first = pl.multiple_of(t * tq, tq)
        chains = []
        for kt in range(tq // tk):
            koff = first + kt * tk
            for hd in range(FOX_HEADS):
                if masked:
                    chains += [(hd, rc, koff, tk, bias_lo if rc % 2 == 0 else bias_hi)
                               for rc in (2 * kt, 2 * kt + 1)]
                else:
                    chains += [(hd, rc, koff, tk, None) for rc in range(2 * kt + 2, row_chunks)]
        run_chains(chains, first_visit=masked)
        return 0

    lax.fori_loop(i, i + 1, functools.partial(diag_body, masked=True), 0)
    lax.fori_loop(i, i + 1, functools.partial(diag_body, masked=False), 0)
    lax.fori_loop(0, i, body, 0)

    for hp in range(HEAD_PAIRS):
        ls = slice(LANES * hp, LANES * (hp + 1))
        for rc in range(row_chunks):
            rs = slice(FOX_ROWS * rc, FOX_ROWS * (rc + 1))
            a0 = acc_ref[2 * hp, rs, :]
            a1 = acc_ref[2 * hp + 1, rs, :]
            dens = pltpu.roll(jnp.where(lo, a1, a0), HEAD_DIM, 1)
            o2 = jnp.where(lo, a0, a1) / dens
            o_ref[0, rs, ls] = (o2 * g_ref[0, rs, ls].astype(F32)).astype(BF16)


def _fox(of3, *, tq=512):
    batch, seq, _ = of3.shape
    kernel = functools.partial(_fox_kernel, tq=tq)
    gate_blk = 3 * FOX_AUG_W // FOX_W
    return pl.pallas_call(
        kernel,
        grid=(batch, seq // tq),
        in_specs=[
            pl.BlockSpec((1, tq, FOX_AUG_W), lambda b, i: (b, i, 0)),
            pl.BlockSpec((1, seq, FOX_AUG_W), lambda b, i: (b, 0, 1)),
            pl.BlockSpec((1, seq, FOX_AUG_W), lambda b, i: (b, 0, 2)),
            pl.BlockSpec((1, tq, FOX_W), lambda b, i: (b, i, gate_blk)),
        ],
        out_specs=pl.BlockSpec((1, tq, FOX_W), lambda b, i: (b, i, 0)),
        out_shape=jax.ShapeDtypeStruct((batch, seq, FOX_W), BF16),
        scratch_shapes=[
            pltpu.VMEM((FOX_HEADS, tq, LANES), F32),
            pltpu.VMEM((FOX_HEADS, tq, LANES), F32),
        ],
        compiler_params=pltpu.CompilerParams(
            dimension_semantics=("arbitrary", "arbitrary"), vmem_limit_bytes=VMEM_LIMIT),
        name="fox_attn",
    )(of3, of3, of3, of3)


def _dil_kernel(q_ref, k_ref, v_ref, g_ref, o_ref, q0, q1, kd, v0, v1, nd_ref, m_ref,
                tq_ref, tk_ref, tv_ref, und_ref, um_ref, *, seq):
    n_blocks = seq // BLOCK
    lane = lax.broadcasted_iota(jnp.int32, (BLOCK, LANES), 1)
    lo = lane < HEAD_DIM
    row2 = lax.broadcasted_iota(jnp.int32, (BLOCK, 2 * BLOCK), 0)
    col2 = lax.broadcasted_iota(jnp.int32, (BLOCK, 2 * BLOCK), 1)
    band = jnp.logical_and(col2 >= row2, col2 <= row2 + BLOCK)
    bias_band = jnp.where(band, 0.0, NEG_INF)
    bias_first = jnp.where(jnp.logical_and(band, col2 >= BLOCK), 0.0, NEG_INF)

    kd[0:BLOCK, :] = jnp.zeros((BLOCK, LANES), BF16)
    v0[0:BLOCK, :] = jnp.zeros((BLOCK, LANES), BF16)
    v1[0:BLOCK, :] = jnp.zeros((BLOCK, LANES), BF16)

    order = sorted(range(len(DILATIONS)), key=lambda p: (DILATIONS[p][1] == 1, DILATIONS[p][1]))
    for slot in order:
        window, d = DILATIONS[slot]
        assert window // d == BLOCK
        length = seq // d
        nb = length // BLOCK
        lo_l = lax.broadcasted_iota(jnp.int32, (length, LANES), 1) < HEAD_DIM
        for r in range(d):
            qdst = slice(r * length, (r + 1) * length)
            kdst = slice(BLOCK + r * length, BLOCK + (r + 1) * length)
            if d == DIL_STAGE * DIL_STAGE:
                r4, j = r % DIL_STAGE, r // DIL_STAGE
                start = r4 * (seq // DIL_STAGE) + j
                qx, kx, vx = (_rows(t, start, length, DIL_STAGE) for t in (tq_ref, tk_ref, tv_ref))
            else:
                qx, kx, vx = (_rows(t.at[0, 0], r, length, d) for t in (q_ref, k_ref, v_ref))
            if d == DIL_STAGE:
                tq_ref[qdst, :] = qx
                tk_ref[qdst, :] = kx
                tv_ref[qdst, :] = vx
            qf = qx.astype(BF16)
            q0[qdst, :] = jnp.where(lo_l, qf, jnp.zeros_like(qf))
            q1[qdst, :] = jnp.where(lo_l, jnp.zeros_like(qf), qf)
            kd[kdst, :] = kx.astype(BF16)
            vf = vx.astype(BF16)
            v0[kdst, :] = jnp.where(lo_l, vf, jnp.ones_like(vf))
            v1[kdst, :] = jnp.where(lo_l, jnp.ones_like(vf), vf)

        def body(g, _, slot=slot, d=d, nb=nb):
            qv = ((q0, v0), (q1, v1))
            for u, hh in [(u, hh) for u in range(DIL_GROUP) for hh in range(2)]:
                t = g * DIL_GROUP + u
                base = pl.multiple_of(t * BLOCK, BLOCK)
                if nb == 1:
                    bias = bias_first
                    nat_start = t
                else:
                    r = t // nb
                    n = t - r * nb
                    bias = jnp.where(n == 0, bias_first, bias_band)
                    nat_start = base if d == 1 else n * (BLOCK * d) + r
                s = lax.dot_general(qv[hh][0][pl.ds(base, BLOCK), :], kd[pl.ds(base, 2 * BLOCK), :],
                                    (((1,), (1,)), ((), ())), preferred_element_type=F32) + bias
                mx = jnp.broadcast_to(jnp.max(s, axis=-1, keepdims=True), (BLOCK, LANES))
                e = jnp.concatenate(
                    [jnp.exp(s[:, LANES * t2:LANES * (t2 + 1)] - mx) for t2 in range(2)], axis=1)
                nd = jnp.dot(e.astype(BF16), qv[hh][1][pl.ds(base, 2 * BLOCK), :],
                             preferred_element_type=F32)
                if d == DIL_STAGE * DIL_STAGE:
                    stage = (t % DIL_STAGE) * (seq // DIL_STAGE) + t // DIL_STAGE
                    _set_rows(und_ref.at[hh], stage, BLOCK, DIL_STAGE, nd)
                    _set_rows(um_ref.at[hh], stage, BLOCK, DIL_STAGE, mx)
                else:
                    _set_rows(nd_ref.at[slot, hh], nat_start, BLOCK, d, nd)
                    _set_rows(m_ref.at[slot, hh], nat_start, BLOCK, d, mx)
            return 0

        lax.fori_loop(0, n_blocks // DIL_GROUP, body, 0)
        if d == DIL_STAGE * DIL_STAGE:
            for hh in range(2):
                for r4 in range(DIL_STAGE):
                    src = slice(r4 * (seq // DIL_STAGE), (r4 + 1) * (seq // DIL_STAGE))
                    _set_rows(nd_ref.at[slot, hh], r4, seq // DIL_STAGE, DIL_STAGE, und_ref[hh, src, :])
                    _set_rows(m_ref.at[slot, hh], r4, seq // DIL_STAGE, DIL_STAGE, um_ref[hh, src, :])

    def merge(c, _):
        rs = pl.ds(pl.multiple_of(c * BLOCK, BLOCK), BLOCK)
        tots = []
        for hh in range(2):
            ms = [m_ref[p, hh, rs, :] for p in range(len(DILATIONS))]
            m_all = jnp.maximum(jnp.maximum(ms[0], ms[1]), ms[2])
            tot = nd_ref[0, hh, rs, :] * jnp.exp(ms[0] - m_all)
            for p in range(1, len(DILATIONS)):
                tot = tot + nd_ref[p, hh, rs, :] * jnp.exp(ms[p] - m_all)
            tots.append(tot)
        dens = pltpu.roll(jnp.where(lo, tots[1], tots[0]), HEAD_DIM, 1)
        o = jnp.where(lo, tots[0], tots[1]) / dens * g_ref[0, 0, rs, :].astype(F32)
        o_ref[0, rs, :] = o.astype(BF16)
        return 0

    lax.fori_loop(0, n_blocks, merge, 0, unroll=4)


def _dilated(od4, odg4):
    batch, _, seq, _ = od4.shape
    kernel = functools.partial(_dil_kernel, seq=seq)
    n_pat = len(DILATIONS)
    return pl.pallas_call(
        kernel,
        grid=(batch, HEAD_PAIRS),
        in_specs=[
            pl.BlockSpec((1, 1, seq, LANES), lambda b, p: (b, p, 0, 0)),
            pl.BlockSpec((1, 1, seq, LANES), lambda b, p: (b, HEAD_PAIRS + p, 0, 0)),
            pl.BlockSpec((1, 1, seq, LANES), lambda b, p: (b, 2 * HEAD_PAIRS + p, 0, 0)),
            pl.BlockSpec((1, 1, seq, LANES), lambda b, p: (b, p, 0, 0)),
        ],
        out_specs=pl.BlockSpec((1, seq, LANES), lambda b, p: (b, 0, p)),
        out_shape=jax.ShapeDtypeStruct((batch, seq, DIL_W), BF16),
        scratch_shapes=[
            pltpu.VMEM((seq, LANES), BF16),
            pltpu.VMEM((seq, LANES), BF16),
            pltpu.VMEM((seq + BLOCK, LANES), BF16),
            pltpu.VMEM((seq + BLOCK, LANES), BF16),
            pltpu.VMEM((seq + BLOCK, LANES), BF16),
            pltpu.VMEM((n_pat, 2, seq, LANES), F32),
            pltpu.VMEM((n_pat, 2, seq, LANES), F32),
            pltpu.VMEM((seq, LANES), F32),
            pltpu.VMEM((seq, LANES), F32),
            pltpu.VMEM((seq, LANES), F32),
            pltpu.VMEM((2, seq, LANES), F32),
            pltpu.VMEM((2, seq, LANES), F32),
        ],
        compiler_params=pltpu.CompilerParams(
            dimension_semantics=("arbitrary", "arbitrary"), vmem_limit_bytes=VMEM_LIMIT),
        name="dilated_attn",
    )(od4, od4, od4, odg4)


def _mem_kv_kernel(mem_ref, g_ref, w_ref, o_ref):
    x = mem_ref[0]
    ms = jnp.mean(x * x, axis=-1, keepdims=True)
    h = (x * lax.rsqrt(ms + RMS_EPS) * g_ref[...]).astype(BF16)
    o_ref[0] = jnp.dot(h, w_ref[...], preferred_element_type=F32).astype(BF16)


def _mem_kv(mem, g, w):
    batch, mlen, _ = mem.shape
    return pl.pallas_call(
        _mem_kv_kernel,
        grid=(batch,),
        in_specs=[
            pl.BlockSpec((1, mlen, D_MODEL), lambda b: (b, 0, 0)),
            pl.BlockSpec((1, D_MODEL), lambda b: (0, 0)),
            pl.BlockSpec((D_MODEL, 2 * MEM_W), lambda b: (0, 0)),
        ],
        out_specs=pl.BlockSpec((1, mlen, 2 * MEM_W), lambda b: (b, 0, 0)),
        out_shape=jax.ShapeDtypeStruct((batch, mlen, 2 * MEM_W), BF16),
        compiler_params=pltpu.CompilerParams(
            dimension_semantics=("arbitrary",), vmem_limit_bytes=VMEM_LIMIT),
        name="mem_kv",
    )(mem, g, w)


def _mem_attn_kernel(q_ref, g_ref, mk_ref, mv_ref, o_ref):
    tq = q_ref.shape[1]
    chains = [(h, rc) for h in range(MEM_HEADS) for rc in range(tq // BLOCK)]

    def scores(c):
        h, rc = chains[c]
        ls = slice(MEM_HEAD_DIM * h, MEM_HEAD_DIM * (h + 1))
        return lax.dot_general(q_ref[0, BLOCK * rc:BLOCK * (rc + 1), ls], mk_ref[0, :, ls],
                               (((1,), (1,)), ((), ())), preferred_element_type=F32)

    pending = [scores(c) for c in range(FOX_LOOKAHEAD)]
    for c, (h, rc) in enumerate(chains):
        if c + FOX_LOOKAHEAD < len(chains):
            pending.append(scores(c + FOX_LOOKAHEAD))
        ls = slice(MEM_HEAD_DIM * h, MEM_HEAD_DIM * (h + 1))
        rs = slice(BLOCK * rc, BLOCK * (rc + 1))
        s = pending.pop(0) * MEM_SCALE
        mx = jnp.max(s, axis=-1, keepdims=True)
        e = jnp.exp(s - mx)
        den = jnp.sum(e, axis=-1, keepdims=True)
        o = jnp.dot(e.astype(BF16), mv_ref[0, :, ls], preferred_element_type=F32) / den
        o_ref[0, rs, ls] = (o * g_ref[0, rs, ls].astype(F32)).astype(BF16)


def _mem_attn(om3, mkv, *, tq=2048):
    batch, seq, _ = om3.shape
    mlen = mkv.shape[1]
    return pl.pallas_call(
        _mem_attn_kernel,
        grid=(batch, seq // tq),
        in_specs=[
            pl.BlockSpec((1, tq, MEM_W), lambda b, i: (b, i, 0)),
            pl.BlockSpec((1, tq, MEM_W), lambda b, i: (b, i, 1)),
            pl.BlockSpec((1, mlen, MEM_W), lambda b, i: (b, 0, 0)),
            pl.BlockSpec((1, mlen, MEM_W), lambda b, i: (b, 0, 1)),
        ],
        out_specs=pl.BlockSpec((1, tq, MEM_W), lambda b, i: (b, i, 0)),
        out_shape=jax.ShapeDtypeStruct((batch, seq, MEM_W), BF16),
        compiler_params=pltpu.CompilerParams(
            dimension_semantics=("arbitrary", "arbitrary"), vmem_limit_bytes=VMEM_LIMIT),
        name="mem_attn",
    )(om3, om3, mkv, mkv)


def _out_proj_kernel(x_ref, yf_ref, yd_ref, ym_ref, w_ref, g_ref, o_ref, *, final_norm):
    z = x_ref[...]
    z = z + jnp.dot(yf_ref[...], w_ref[0:FOX_W, :], preferred_element_type=F32)
    z = z + jnp.dot(yd_ref[...], w_ref[FOX_W:FOX_W + DIL_W, :], preferred_element_type=F32)
    z = z + jnp.dot(ym_ref[...], w_ref[FOX_W + DIL_W:MIX_W, :], preferred_element_type=F32)
    if final_norm:
        ms = jnp.mean(z * z, axis=-1, keepdims=True)
        z = z * lax.rsqrt(ms + RMS_EPS) * g_ref[...]
    o_ref[...] = z


def _out_proj(x2, yf, yd, ym, w, g, *, final_norm, tm=512):
    n = x2.shape[0]
    row = lambda i: (i, 0)
    const = lambda i: (0, 0)
    kernel = functools.partial(_out_proj_kernel, final_norm=final_norm)
    return pl.pallas_call(
        kernel,
        grid=(n // tm,),
        in_specs=[
            pl.BlockSpec((tm, D_MODEL), row),
            pl.BlockSpec((tm, FOX_W), row),
            pl.BlockSpec((tm, DIL_W), row),
            pl.BlockSpec((tm, MEM_W), row),
            pl.BlockSpec((MIX_W, D_MODEL), const),
            pl.BlockSpec((1, D_MODEL), const),
        ],
        out_specs=pl.BlockSpec((tm, D_MODEL), row),
        out_shape=jax.ShapeDtypeStruct((n, D_MODEL), F32),
        compiler_params=pltpu.CompilerParams(
            dimension_semantics=("arbitrary",), vmem_limit_bytes=VMEM_LIMIT),
        name="out_proj",
    )(x2, yf, yd, ym, w, g)


def _rope_tables(seq):
    half = ROPE_DIM // 2
    pos = jnp.arange(seq, dtype=F32)
    inv_freq = 1.0 / (ROPE_THETA ** (jnp.arange(0, ROPE_DIM, 2, dtype=F32) / ROPE_DIM))
    ang = pos[:, None] * inv_freq[None, :]
    cos = jnp.cos(ang)
    sin = jnp.sin(ang)
    ones = jnp.ones((seq, HEAD_DIM - ROPE_DIM), F32)
    zeros = jnp.zeros((seq, HEAD_DIM - ROPE_DIM), F32)
    zh = jnp.zeros((seq, half), F32)
    cos_h = jnp.concatenate([cos, cos, ones], axis=1)
    sina_h = jnp.concatenate([zh, sin, zeros], axis=1)
    sinb_h = jnp.concatenate([-sin, zh, zeros], axis=1)
    tile = lambda t: jnp.concatenate([t, t], axis=1)
    return tile(cos_h), tile(sina_h), tile(sinb_h)


def _split_w_in(w):
    sizes = [FOX_W] * 4 + [FOX_HEADS] + [DIL_W] * 4 + [MEM_W] * 2
    offs = [0]
    for s in sizes:
        offs.append(offs[-1] + s)
    wt = jnp.swapaxes(w, 0, 1).astype(BF16)
    rows = [wt[offs[k]:offs[k + 1]] for k in range(len(sizes))]
    fq, fk, fv, fg, flog, dq, dk, dv, dg, mq, mg = rows
    w_main = jnp.concatenate([fq, fk, fv, fg, dq, dk, dv, dg, mq, mg], axis=0).T
    w_fl = jnp.pad(flog, ((0, LANES - FOX_HEADS), (0, 0))).T
    return w_main, w_fl


def kernel(x, mem, norm_g, w_in, b_forget, mem_norm_g, w_mem_kv, w_out, final_norm_g):
    batch, seq, _ = x.shape
    depth = norm_g.shape[0]
    cos_t, sina_t, sinb_t = _rope_tables(seq)
    x2 = x.reshape(batch * seq, D_MODEL)
    for l in range(depth):
        w_main, w_fl = _split_w_in(w_in[l])
        b_fl = jnp.pad(b_forget[l], (0, LANES - FOX_HEADS)).reshape(1, LANES)
        of, od, odg, om = _in_proj(
            x2, norm_g[l].reshape(1, D_MODEL), w_main, w_fl, b_fl, cos_t, sina_t, sinb_t, seq=seq)
        yf = _fox(of.reshape(batch, seq, FOX_OUT_W))
        yd = _dilated(od, odg)
        mkv = _mem_kv(mem, mem_norm_g[l].reshape(1, D_MODEL), w_mem_kv[l].astype(BF16))
        ym = _mem_attn(om.reshape(batch, seq, 1024), mkv)
        last = l == depth - 1
        x2 = _out_proj(x2, yf.reshape(batch * seq, FOX_W), yd.reshape(batch * seq, DIL_W),
                       ym.reshape(batch * seq, MEM_W), w_out[l].astype(BF16),
                       final_norm_g.reshape(1, D_MODEL), final_norm=last)
    return x2.reshape(batch, seq, D_MODEL)
```

```python
import functools
import itertools
import math

import jax
import jax.numpy as jnp
from jax import lax
from jax.experimental import pallas as pl
from jax.experimental.pallas import tpu as pltpu

F32 = jnp.float32
BF16 = jnp.bfloat16

D_MODEL = 1024
HEAD_DIM = 64
FOX_HEADS = 12
DIL_HEADS = 12
MEM_HEADS = 4
MEM_HEAD_DIM = 128
FOX_W = FOX_HEADS * HEAD_DIM
DIL_W = DIL_HEADS * HEAD_DIM
MEM_W = MEM_HEADS * MEM_HEAD_DIM
MIX_W = FOX_W + DIL_W + MEM_W
DILATIONS = ((128, 1), (512, 4), (2048, 16))
BLOCK = 128
ROPE_THETA = 500000.0
ROPE_DIM = HEAD_DIM // 4
RMS_EPS = 1e-6
NEG_INF = -1e30
QK_SCALE = 1.0 / math.sqrt(HEAD_DIM)
MEM_SCALE = 1.0 / math.sqrt(MEM_HEAD_DIM)

LANES = 128
HEAD_PAIRS = FOX_HEADS // 2
FOX_AUG_W = FOX_HEADS * LANES
FOX_OUT_W = 3 * FOX_AUG_W + FOX_W
FOX_ROWS = 128
FOX_KEYS = 256
LOOKAHEAD = 2
DIL_GROUP = 16
DIL_STAGE = 4
DIL_QKV_W = 3 * DIL_W
MEM_QG_W = 2 * MEM_W

_SEG_W = dict(fq=FOX_W, fk=FOX_W, fv=FOX_W, fg=FOX_W, dq=DIL_W, dk=DIL_W, dv=DIL_W, dg=DIL_W,
              mq=MEM_W, mg=MEM_W)
_SEG = dict(zip(_SEG_W, itertools.accumulate(_SEG_W.values(), initial=0)))
MAIN_W = sum(_SEG_W.values())

V7X_VMEM_BYTES = 64 * 1024 * 1024
VMEM_LIMIT = V7X_VMEM_BYTES - 8 * 1024 * 1024


def _silu(r):
    return r * (1.0 / (1.0 + jnp.exp(-r)))


def _rows(ref, start, size, stride):
    if stride == 1:
        return ref[pl.ds(start, size), :]
    return ref[pl.ds(start, size, stride=stride), :]


def _set_rows(ref, start, size, stride, val):
    if stride == 1:
        ref[pl.ds(start, size), :] = val
    else:
        ref[pl.ds(start, size, stride=stride), :] = val


def _in_proj_kernel(x_ref, g_ref, w_ref, wfl_ref, bf_ref, cos_ref, sina_ref, sinb_ref,
                    of_ref, od_ref, odg_ref, om_ref, carry_ref,
                    *, tm, tiles_per_seq):
    i = pl.program_id(0)
    x = x_ref[...]
    ms = jnp.mean(x * x, axis=-1, keepdims=True)
    h = (x * lax.rsqrt(ms + RMS_EPS) * g_ref[...]).astype(BF16)

    def seg(name):
        off = _SEG[name]
        return jnp.dot(h, w_ref[:, off:off + _SEG_W[name]], preferred_element_type=F32)

    fl = jnp.dot(h, wfl_ref[...], preferred_element_type=F32) + bf_ref[...]
    logf = jnp.minimum(fl, 0.0) - jnp.log1p(jnp.exp(-jnp.abs(fl)))
    a1 = logf.astype(BF16)
    r1 = logf - a1.astype(F32)
    a2 = r1.astype(BF16)
    a3 = (r1 - a2.astype(F32)).astype(BF16)
    row = lax.broadcasted_iota(jnp.int32, (tm, tm), 0)
    col = lax.broadcasted_iota(jnp.int32, (tm, tm), 1)
    tri = jnp.where(row >= col, 1.0, 0.0).astype(BF16)

    @pl.when(i % tiles_per_seq == 0)
    def _():
        carry_ref[...] = jnp.zeros_like(carry_ref)

    c = (jnp.dot(tri, a1, preferred_element_type=F32)
         + jnp.dot(tri, a2, preferred_element_type=F32)
         + jnp.dot(tri, a3, preferred_element_type=F32)) + carry_ref[...]
    carry_ref[...] = c[tm - 1:tm, :]

    c1 = c.astype(BF16).astype(F32)
    c2 = (c - c1).astype(BF16).astype(F32)
    c3 = (c - c1 - c2).astype(BF16).astype(F32)
    fq = seg("fq") * QK_SCALE
    fk = seg("fk")
    fv = seg("fv")
    lane = lax.broadcasted_iota(jnp.int32, (tm, LANES), 1)
    for hd in range(FOX_HEADS):
        pair = slice(LANES * (hd // 2), LANES * (hd // 2 + 1))
        own = (lane < HEAD_DIM) if hd % 2 == 0 else (lane >= HEAD_DIM)
        base = HEAD_DIM if hd % 2 == 0 else 0
        b1, b2, b3 = (jnp.broadcast_to(t[:, hd:hd + 1], (tm, LANES)) for t in (c1, c2, c3))
        ones_q = jnp.where(jnp.logical_and(lane >= base, lane < base + 3), 1.0, 0.0)
        ones_k = jnp.where(jnp.logical_and(lane >= base + 3, lane < base + 6), 1.0, 0.0)
        q_bias = jnp.where(lane == base + 3, b1, jnp.where(lane == base + 4, b2,
                           jnp.where(lane == base + 5, b3, ones_q)))
        k_bias = jnp.where(lane == base, -b1, jnp.where(lane == base + 1, -b2,
                           jnp.where(lane == base + 2, -b3, ones_k)))
        dst = LANES * hd
        of_ref[:, dst:dst + LANES] = jnp.where(own, fq[:, pair], q_bias).astype(BF16)
        of_ref[:, FOX_AUG_W + dst:FOX_AUG_W + dst + LANES] = \
            jnp.where(own, fk[:, pair], k_bias).astype(BF16)
        of_ref[:, 2 * FOX_AUG_W + dst:2 * FOX_AUG_W + dst + LANES] = \
            jnp.where(own, fv[:, pair], 1.0).astype(BF16)
    of_ref[:, 3 * FOX_AUG_W:3 * FOX_AUG_W + FOX_W] = _silu(seg("fg")).astype(BF16)

    cos = cos_ref[...]
    sina = sina_ref[...]
    sinb = sinb_ref[...]

    def rope_store(r, dst_off, scale):
        for j in range(DIL_W // LANES):
            c = r[:, LANES * j:LANES * (j + 1)]
            rot = c * cos + pltpu.roll(c, ROPE_DIM // 2, 1) * sina \
                + pltpu.roll(c, LANES - ROPE_DIM // 2, 1) * sinb
            if scale != 1.0:
                rot = rot * scale
            od_ref[:, dst_off + LANES * j:dst_off + LANES * (j + 1)] = rot

    rope_store(seg("dq"), 0, QK_SCALE)
    rope_store(seg("dk"), DIL_W, 1.0)
    od_ref[:, 2 * DIL_W:DIL_QKV_W] = seg("dv")
    odg_ref[...] = _silu(seg("dg")).astype(BF16)

    om_ref[:, 0:MEM_W] = seg("mq").astype(BF16)
    om_ref[:, MEM_W:MEM_QG_W] = _silu(seg("mg")).astype(BF16)


def _in_proj(x2, g, w_main, w_fl, b_fl, cos_t, sina_t, sinb_t, *, seq, tm=512):
    n = x2.shape[0]
    tiles_per_seq = seq // tm
    const = lambda i: (0, 0)
    row = lambda i: (i, 0)
    pos = lambda i: (i % tiles_per_seq, 0)
    kernel = functools.partial(_in_proj_kernel, tm=tm, tiles_per_seq=tiles_per_seq)
    return pl.pallas_call(
        kernel,
        grid=(n // tm,),
        in_specs=[
            pl.BlockSpec((tm, D_MODEL), row),
            pl.BlockSpec((1, D_MODEL), const),
            pl.BlockSpec((D_MODEL, MAIN_W), const),
            pl.BlockSpec((D_MODEL, LANES), const),
            pl.BlockSpec((1, LANES), const),
            pl.BlockSpec((tm, LANES), pos),
            pl.BlockSpec((tm, LANES), pos),
            pl.BlockSpec((tm, LANES), pos),
        ],
        out_specs=[
            pl.BlockSpec((tm, FOX_OUT_W), row),
            pl.BlockSpec((tm, DIL_QKV_W), row),
            pl.BlockSpec((tm, DIL_W), row),
            pl.BlockSpec((tm, MEM_QG_W), row),
        ],
        out_shape=[
            jax.ShapeDtypeStruct((n, FOX_OUT_W), BF16),
            jax.ShapeDtypeStruct((n, DIL_QKV_W), F32),
            jax.ShapeDtypeStruct((n, DIL_W), BF16),
            jax.ShapeDtypeStruct((n, MEM_QG_W), BF16),
        ],
        scratch_shapes=[pltpu.VMEM((1, LANES), F32)],
        compiler_params=pltpu.CompilerParams(
            dimension_semantics=("arbitrary",), vmem_limit_bytes=VMEM_LIMIT),
        name="in_proj",
    )(x2, g, w_main, w_fl, b_fl, cos_t, sina_t, sinb_t)


def _fox_kernel(q_ref, k_ref, v_ref, g_ref, o_ref, acc_ref, m_ref, *, tq):
    i = pl.program_id(1)
    tk = FOX_KEYS
    row_chunks = tq // FOX_ROWS
    lane = lax.broadcasted_iota(jnp.int32, (FOX_ROWS, LANES), 1)
    lo = lane < HEAD_DIM
    row = lax.broadcasted_iota(jnp.int32, (FOX_ROWS, tk), 0)
    col = lax.broadcasted_iota(jnp.int32, (FOX_ROWS, tk), 1)
    bias_lo = jnp.where(row >= col, 0.0, NEG_INF)
    bias_hi = jnp.where(row + FOX_ROWS >= col, 0.0, NEG_INF)

    def run_chains(chains, first_visit=False):
        def scores(c):
            hd, rc, koff, klen, _ = chains[c]
            ls = slice(LANES * hd, LANES * (hd + 1))
            rs = slice(FOX_ROWS * rc, FOX_ROWS * (rc + 1))
            return lax.dot_general(q_ref[0, rs, ls], k_ref[0, pl.ds(koff, klen), ls],
                                   (((1,), (1,)), ((), ())), preferred_element_type=F32)

        pending = [scores(c) for c in range(LOOKAHEAD)]
        for c, (hd, rc, koff, klen, bias) in enumerate(chains):
            if c + LOOKAHEAD < len(chains):
                pending.append(scores(c + LOOKAHEAD))
            s = pending.pop(0)
            ls = slice(LANES * hd, LANES * (hd + 1))
            rs = slice(FOX_ROWS * rc, FOX_ROWS * (rc + 1))
            if bias is not None:
                s = s + bias
            m_new = jnp.broadcast_to(jnp.max(s, axis=-1, keepdims=True), (FOX_ROWS, LANES))
            if not first_visit:
                m_old = m_ref[hd, rs, :]
                m_new = jnp.maximum(m_old, m_new)
            p = jnp.concatenate(
                [jnp.exp(s[:, LANES * t:LANES * (t + 1)] - m_new) for t in range(klen // LANES)],
                axis=1)
            pv = jnp.dot(p.astype(BF16), v_ref[0, pl.ds(koff, klen), ls],
                         preferred_element_type=F32)
            if first_visit:
                acc_ref[hd, rs, :] = pv
            else:
                acc_ref[hd, rs, :] = jnp.exp(m_old - m_new) * acc_ref[hd, rs, :] + pv
            m_ref[hd, rs, :] = m_new

    def body(j, _):
        first = pl.multiple_of(j * tq, tq)
        run_chains([(hd, rc, first + kt * tk, tk, None)
                    for kt in range(tq // tk) for hd in range(FOX_HEADS) for rc in range(row_chunks)])
        return 0

    def diag_body(t, _, masked):
        first = pl.multiple_of(t * tq, tq)
        chains = []
        for kt in range(tq // tk):
            koff = first + kt * tk
            for hd in range(FOX_HEADS):
                if masked:
                    chains += [(hd, rc, koff, tk, bias_lo if rc % 2 == 0 else bias_hi)
                               for rc in (2 * kt, 2 * kt + 1)]
                else:
                    chains += [(hd, rc, koff, tk, None) for rc in range(2 * kt + 2, row_chunks)]
        run_chains(chains, first_visit=masked)
        return 0

    lax.fori_loop(i, i + 1, functools.partial(diag_body, masked=True), 0)
    lax.fori_loop(i, i + 1, functools.partial(diag_body, masked=False), 0)
    lax.fori_loop(0, i, body, 0)

    for hp in range(HEAD_PAIRS):
        ls = slice(LANES * hp, LANES * (hp + 1))
        for rc in range(row_chunks):
            rs = slice(FOX_ROWS * rc, FOX_ROWS * (rc + 1))
            a0 = acc_ref[2 * hp, rs, :]
            a1 = acc_ref[2 * hp + 1, rs, :]
            dens = pltpu.roll(jnp.where(lo, a1, a0), HEAD_DIM, 1)
            o2 = jnp.where(lo, a0, a1) / dens
            o_ref[0, rs, ls] = (o2 * g_ref[0, rs, ls].astype(F32)).astype(BF16)


def _fox(of3, *, tq=512):
    batch, seq, _ = of3.shape
    kernel = functools.partial(_fox_kernel, tq=tq)
    gate_blk = 3 * FOX_AUG_W // FOX_W
    return pl.pallas_call(
        kernel,
        grid=(batch, seq // tq),
        in_specs=[
            pl.BlockSpec((1, tq, FOX_AUG_W), lambda b, i: (b, i, 0)),
            pl.BlockSpec((1, seq, FOX_AUG_W), lambda b, i: (b, 0, 1)),
            pl.BlockSpec((1, seq, FOX_AUG_W), lambda b, i: (b, 0, 2)),
            pl.BlockSpec((1, tq, FOX_W), lambda b, i: (b, i, gate_blk)),
        ],
        out_specs=pl.BlockSpec((1, tq, FOX_W), lambda b, i: (b, i, 0)),
        out_shape=jax.ShapeDtypeStruct((batch, seq, FOX_W), BF16),
        scratch_shapes=[
            pltpu.VMEM((FOX_HEADS, tq, LANES), F32),
            pltpu.VMEM((FOX_HEADS, tq, LANES), F32),
        ],
        compiler_params=pltpu.CompilerParams(
            dimension_semantics=("arbitrary", "arbitrary"), vmem_limit_bytes=VMEM_LIMIT),
        name="fox_attn",
    )(of3, of3, of3, of3)


def _dil_kernel(q_ref, k_ref, v_ref, g_ref, o_ref, q0, q1, kd, v0, v1, nd_ref, m_ref,
                tq_ref, tk_ref, tv_ref, und_ref, um_ref, *, seq):
    n_blocks = seq // BLOCK
    lane = lax.broadcasted_iota(jnp.int32, (BLOCK, LANES), 1)
    lo = lane < HEAD_DIM
    row2 = lax.broadcasted_iota(jnp.int32, (BLOCK, 2 * BLOCK), 0)
    col2 = lax.broadcasted_iota(jnp.int32, (BLOCK, 2 * BLOCK), 1)
    band = jnp.logical_and(col2 >= row2, col2 <= row2 + BLOCK)
    bias_band = jnp.where(band, 0.0, NEG_INF)
    bias_first = jnp.where(jnp.logical_and(band, col2 >= BLOCK), 0.0, NEG_INF)

    kd[0:BLOCK, :] = jnp.zeros((BLOCK, LANES), BF16)
    v0[0:BLOCK, :] = jnp.zeros((BLOCK, LANES), BF16)
    v1[0:BLOCK, :] = jnp.zeros((BLOCK, LANES), BF16)

    order = sorted(range(len(DILATIONS)), key=lambda p: (DILATIONS[p][1] == 1, DILATIONS[p][1]))
    for slot in order:
        window, d = DILATIONS[slot]
        assert window // d == BLOCK
        length = seq // d
        nb = length // BLOCK
        lo_l = lax.broadcasted_iota(jnp.int32, (length, LANES), 1) < HEAD_DIM
        for r in range(d):
            qdst = slice(r * length, (r + 1) * length)
            kdst = slice(BLOCK + r * length, BLOCK + (r + 1) * length)
            if d == DIL_STAGE * DIL_STAGE:
                r4, j = r % DIL_STAGE, r // DIL_STAGE
                start = r4 * (seq // DIL_STAGE) + j
                qx, kx, vx = (_rows(t, start, length, DIL_STAGE) for t in (tq_ref, tk_ref, tv_ref))
            else:
                qx, kx, vx = (_rows(t.at[0], r, length, d) for t in (q_ref, k_ref, v_ref))
            if d == DIL_STAGE:
                tq_ref[qdst, :] = qx
                tk_ref[qdst, :] = kx
                tv_ref[qdst, :] = vx
            qf = qx.astype(BF16)
            q0[qdst, :] = jnp.where(lo_l, qf, jnp.zeros_like(qf))
            q1[qdst, :] = jnp.where(lo_l, jnp.zeros_like(qf), qf)
            kd[kdst, :] = kx.astype(BF16)
            vf = vx.astype(BF16)
            v0[kdst, :] = jnp.where(lo_l, vf, jnp.ones_like(vf))
            v1[kdst, :] = jnp.where(lo_l, jnp.ones_like(vf), vf)

        def body(g, _, slot=slot, d=d, nb=nb):
            qv = ((q0, v0), (q1, v1))
            for u, hh in [(u, hh) for u in range(DIL_GROUP) for hh in range(2)]:
                t = g * DIL_GROUP + u
                base = pl.multiple_of(t * BLOCK, BLOCK)
                if nb == 1:
                    bias = bias_first
                    nat_start = t
                else:
                    r = t // nb
                    n = t - r * nb
                    bias = jnp.where(n == 0, bias_first, bias_band)
                    nat_start = base if d == 1 else n * (BLOCK * d) + r
                s = lax.dot_general(qv[hh][0][pl.ds(base, BLOCK), :], kd[pl.ds(base, 2 * BLOCK), :],
                                    (((1,), (1,)), ((), ())), preferred_element_type=F32) + bias
                mx = jnp.broadcast_to(jnp.max(s, axis=-1, keepdims=True), (BLOCK, LANES))
                e = jnp.concatenate(
                    [jnp.exp(s[:, LANES * t2:LANES * (t2 + 1)] - mx) for t2 in range(2)], axis=1)
                nd = jnp.dot(e.astype(BF16), qv[hh][1][pl.ds(base, 2 * BLOCK), :],
                             preferred_element_type=F32)
                if d == DIL_STAGE * DIL_STAGE:
                    stage = (t % DIL_STAGE) * (seq // DIL_STAGE) + t // DIL_STAGE
                    _set_rows(und_ref.at[hh], stage, BLOCK, DIL_STAGE, nd)
                    _set_rows(um_ref.at[hh], stage, BLOCK, DIL_STAGE, mx)
                else:
                    _set_rows(nd_ref.at[slot, hh], nat_start, BLOCK, d, nd)
                    _set_rows(m_ref.at[slot, hh], nat_start, BLOCK, d, mx)
            return 0

        lax.fori_loop(0, n_blocks // DIL_GROUP, body, 0)
        if d == DIL_STAGE * DIL_STAGE:
            for hh in range(2):
                for r4 in range(DIL_STAGE):
                    src = slice(r4 * (seq // DIL_STAGE), (r4 + 1) * (seq // DIL_STAGE))
                    _set_rows(nd_ref.at[slot, hh], r4, seq // DIL_STAGE, DIL_STAGE, und_ref[hh, src, :])
                    _set_rows(m_ref.at[slot, hh], r4, seq // DIL_STAGE, DIL_STAGE, um_ref[hh, src, :])

    def merge(c, _):
        rs = pl.ds(pl.multiple_of(c * BLOCK, BLOCK), BLOCK)
        tots = []
        for hh in range(2):
            ms = [m_ref[p, hh, rs, :] for p in range(len(DILATIONS))]
            m_all = jnp.maximum(jnp.maximum(ms[0], ms[1]), ms[2])
            tot = nd_ref[0, hh, rs, :] * jnp.exp(ms[0] - m_all)
            for p in range(1, len(DILATIONS)):
                tot = tot + nd_ref[p, hh, rs, :] * jnp.exp(ms[p] - m_all)
            tots.append(tot)
        dens = pltpu.roll(jnp.where(lo, tots[1], tots[0]), HEAD_DIM, 1)
        o = jnp.where(lo, tots[0], tots[1]) / dens * g_ref[0, rs, :].astype(F32)
        o_ref[0, rs, :] = o.astype(BF16)
        return 0

    lax.fori_loop(0, n_blocks, merge, 0, unroll=4)


def _dilated(od3, odg3):
    batch, seq, _ = od3.shape
    kernel = functools.partial(_dil_kernel, seq=seq)
    n_pat = len(DILATIONS)
    return pl.pallas_call(
        kernel,
        grid=(batch, HEAD_PAIRS),
        in_specs=[
            pl.BlockSpec((1, seq, LANES), lambda b, p: (b, 0, p)),
            pl.BlockSpec((1, seq, LANES), lambda b, p: (b, 0, HEAD_PAIRS + p)),
            pl.BlockSpec((1, seq, LANES), lambda b, p: (b, 0, 2 * HEAD_PAIRS + p)),
            pl.BlockSpec((1, seq, LANES), lambda b, p: (b, 0, p)),
        ],
        out_specs=pl.BlockSpec((1, seq, LANES), lambda b, p: (b, 0, p)),
        out_shape=jax.ShapeDtypeStruct((batch, seq, DIL_W), BF16),
        scratch_shapes=[
            pltpu.VMEM((seq, LANES), BF16),
            pltpu.VMEM((seq, LANES), BF16),
            pltpu.VMEM((seq + BLOCK, LANES), BF16),
            pltpu.VMEM((seq + BLOCK, LANES), BF16),
            pltpu.VMEM((seq + BLOCK, LANES), BF16),
            pltpu.VMEM((n_pat, 2, seq, LANES), F32),
            pltpu.VMEM((n_pat, 2, seq, LANES), F32),
            pltpu.VMEM((seq, LANES), F32),
            pltpu.VMEM((seq, LANES), F32),
            pltpu.VMEM((seq, LANES), F32),
            pltpu.VMEM((2, seq, LANES), F32),
            pltpu.VMEM((2, seq, LANES), F32),
        ],
        compiler_params=pltpu.CompilerParams(
            dimension_semantics=("arbitrary", "arbitrary"), vmem_limit_bytes=VMEM_LIMIT),
        name="dilated_attn",
    )(od3, od3, od3, odg3)


def _mem_kv_kernel(mem_ref, g_ref, w_ref, o_ref):
    x = mem_ref[0]
    ms = jnp.mean(x * x, axis=-1, keepdims=True)
    h = (x * lax.rsqrt(ms + RMS_EPS) * g_ref[...]).astype(BF16)
    o_ref[0] = jnp.dot(h, w_ref[...], preferred_element_type=F32).astype(BF16)


def _mem_kv(mem, g, w):
    batch, mlen, _ = mem.shape
    return pl.pallas_call(
        _mem_kv_kernel,
        grid=(batch,),
        in_specs=[
            pl.BlockSpec((1, mlen, D_MODEL), lambda b: (b, 0, 0)),
            pl.BlockSpec((1, D_MODEL), lambda b: (0, 0)),
            pl.BlockSpec((D_MODEL, 2 * MEM_W), lambda b: (0, 0)),
        ],
        out_specs=pl.BlockSpec((1, mlen, 2 * MEM_W), lambda b: (b, 0, 0)),
        out_shape=jax.ShapeDtypeStruct((batch, mlen, 2 * MEM_W), BF16),
        compiler_params=pltpu.CompilerParams(
            dimension_semantics=("arbitrary",), vmem_limit_bytes=VMEM_LIMIT),
        name="mem_kv",
    )(mem, g, w)


def _mem_attn_kernel(q_ref, g_ref, mk_ref, mv_ref, o_ref):
    tq = q_ref.shape[1]
    chains = [(h, rc) for h in range(MEM_HEADS) for rc in range(tq // BLOCK)]

    def scores(c):
        h, rc = chains[c]
        ls = slice(MEM_HEAD_DIM * h, MEM_HEAD_DIM * (h + 1))
        return lax.dot_general(q_ref[0, BLOCK * rc:BLOCK * (rc + 1), ls], mk_ref[0, :, ls],
                               (((1,), (1,)), ((), ())), preferred_element_type=F32)

    pending = [scores(c) for c in range(LOOKAHEAD)]
    for c, (h, rc) in enumerate(chains):
        if c + LOOKAHEAD < len(chains):
            pending.append(scores(c + LOOKAHEAD))
        ls = slice(MEM_HEAD_DIM * h, MEM_HEAD_DIM * (h + 1))
        rs = slice(BLOCK * rc, BLOCK * (rc + 1))
        s = pending.pop(0) * MEM_SCALE
        mx = jnp.max(s, axis=-1, keepdims=True)
        e = jnp.exp(s - mx)
        den = jnp.sum(e, axis=-1, keepdims=True)
        o = jnp.dot(e.astype(BF16), mv_ref[0, :, ls], preferred_element_type=F32) / den
        o_ref[0, rs, ls] = (o * g_ref[0, rs, ls].astype(F32)).astype(BF16)


def _mem_attn(om3, mkv, *, tq=2048):
    batch, seq, _ = om3.shape
    mlen = mkv.shape[1]
    return pl.pallas_call(
        _mem_attn_kernel,
        grid=(batch, seq // tq),
        in_specs=[
            pl.BlockSpec((1, tq, MEM_W), lambda b, i: (b, i, 0)),
            pl.BlockSpec((1, tq, MEM_W), lambda b, i: (b, i, 1)),
            pl.BlockSpec((1, mlen, MEM_W), lambda b, i: (b, 0, 0)),
            pl.BlockSpec((1, mlen, MEM_W), lambda b, i: (b, 0, 1)),
        ],
        out_specs=pl.BlockSpec((1, tq, MEM_W), lambda b, i: (b, i, 0)),
        out_shape=jax.ShapeDtypeStruct((batch, seq, MEM_W), BF16),
        compiler_params=pltpu.CompilerParams(
            dimension_semantics=("arbitrary", "arbitrary"), vmem_limit_bytes=VMEM_LIMIT),
        name="mem_attn",
    )(om3, om3, mkv, mkv)


def _out_proj_kernel(x_ref, yf_ref, yd_ref, ym_ref, w_ref, g_ref, o_ref, *, final_norm):
    z = x_ref[...]
    z = z + jnp.dot(yf_ref[...], w_ref[0:FOX_W, :], preferred_element_type=F32)
    z = z + jnp.dot(yd_ref[...], w_ref[FOX_W:FOX_W + DIL_W, :], preferred_element_type=F32)
    z = z + jnp.dot(ym_ref[...], w_ref[FOX_W + DIL_W:MIX_W, :], preferred_element_type=F32)
    if final_norm:
        ms = jnp.mean(z * z, axis=-1, keepdims=True)
        z = z * lax.rsqrt(ms + RMS_EPS) * g_ref[...]
    o_ref[...] = z


def _out_proj(x2, yf, yd, ym, w, g, *, final_norm, tm=1024):
    n = x2.shape[0]
    row = lambda i: (i, 0)
    const = lambda i: (0, 0)
    kernel = functools.partial(_out_proj_kernel, final_norm=final_norm)
    return pl.pallas_call(
        kernel,
        grid=(n // tm,),
        in_specs=[
            pl.BlockSpec((tm, D_MODEL), row),
            pl.BlockSpec((tm, FOX_W), row),
            pl.BlockSpec((tm, DIL_W), row),
            pl.BlockSpec((tm, MEM_W), row),
            pl.BlockSpec((MIX_W, D_MODEL), const),
            pl.BlockSpec((1, D_MODEL), const),
        ],
        out_specs=pl.BlockSpec((tm, D_MODEL), row),
        out_shape=jax.ShapeDtypeStruct((n, D_MODEL), F32),
        compiler_params=pltpu.CompilerParams(
            dimension_semantics=("arbitrary",), vmem_limit_bytes=VMEM_LIMIT),
        name="out_proj",
    )(x2, yf, yd, ym, w, g)


def _rope_tables(seq):
    half = ROPE_DIM // 2
    pos = jnp.arange(seq, dtype=F32)
    inv_freq = 1.0 / (ROPE_THETA ** (jnp.arange(0, ROPE_DIM, 2, dtype=F32) / ROPE_DIM))
    ang = pos[:, None] * inv_freq[None, :]
    cos = jnp.cos(ang)
    sin = jnp.sin(ang)
    ones = jnp.ones((seq, HEAD_DIM - ROPE_DIM), F32)
    zeros = jnp.zeros((seq, HEAD_DIM - ROPE_DIM), F32)
    zh = jnp.zeros((seq, half), F32)
    cos_h = jnp.concatenate([cos, cos, ones], axis=1)
    sina_h = jnp.concatenate([zh, sin, zeros], axis=1)
    sinb_h = jnp.concatenate([-sin, zh, zeros], axis=1)
    tile = lambda t: jnp.concatenate([t, t], axis=1)
    return tile(cos_h), tile(sina_h), tile(sinb_h)


def _split_w_in(w):
    sizes = [FOX_W] * 4 + [FOX_HEADS] + [DIL_W] * 4 + [MEM_W] * 2
    offs = [0]
    for s in sizes:
        offs.append(offs[-1] + s)
    wt = jnp.swapaxes(w, 0, 1).astype(BF16)
    rows = [wt[offs[k]:offs[k + 1]] for k in range(len(sizes))]
    fq, fk, fv, fg, flog, dq, dk, dv, dg, mq, mg = rows
    w_main = jnp.concatenate([fq, fk, fv, fg, dq, dk, dv, dg, mq, mg], axis=0).T
    w_fl = jnp.pad(flog, ((0, LANES - FOX_HEADS), (0, 0))).T
    return w_main, w_fl


def kernel(x, mem, norm_g, w_in, b_forget, mem_norm_g, w_mem_kv, w_out, final_norm_g):
    batch, seq, _ = x.shape
    depth = norm_g.shape[0]
    cos_t, sina_t, sinb_t = _rope_tables(seq)
    x2 = x.reshape(batch * seq, D_MODEL)
    for l in range(depth):
        w_main, w_fl = _split_w_in(w_in[l])
        b_fl = jnp.pad(b_forget[l], (0, LANES - FOX_HEADS)).reshape(1, LANES)
        of, od, odg, om = _in_proj(
            x2, norm_g[l].reshape(1, D_MODEL), w_main, w_fl, b_fl, cos_t, sina_t, sinb_t, seq=seq)
        yf = _fox(of.reshape(batch, seq, FOX_OUT_W))
        yd = _dilated(od.reshape(batch, seq, DIL_QKV_W), odg.reshape(batch, seq, DIL_W))
        mkv = _mem_kv(mem, mem_norm_g[l].reshape(1, D_MODEL), w_mem_kv[l].astype(BF16))
        ym = _mem_attn(om.reshape(batch, seq, MEM_QG_W), mkv)
        last = l == depth - 1
        x2 = _out_proj(x2, yf.reshape(batch * seq, FOX_W), yd.reshape(batch * seq, DIL_W),
                       ym.reshape(batch * seq, MEM_W), w_out[l].astype(BF16),
                       final_norm_g.reshape(1, D_MODEL), final_norm=last)
    return x2.reshape(batch, seq, D_MODEL)
```

```python
import functools
import itertools
import math

import jax
import jax.numpy as jnp
from jax import lax
from jax.experimental import pallas as pl
from jax.experimental.pallas import tpu as pltpu

F32 = jnp.float32
BF16 = jnp.bfloat16

D_MODEL = 1024
HEAD_DIM = 64
FOX_HEADS = 12
DIL_HEADS = 12
MEM_HEADS = 4
MEM_HEAD_DIM = 128
FOX_W = FOX_HEADS * HEAD_DIM
DIL_W = DIL_HEADS * HEAD_DIM
MEM_W = MEM_HEADS * MEM_HEAD_DIM
MIX_W = FOX_W + DIL_W + MEM_W
DILATIONS = ((128, 1), (512, 4), (2048, 16))
BLOCK = 128
ROPE_THETA = 500000.0
ROPE_DIM = HEAD_DIM // 4
RMS_EPS = 1e-6
NEG_INF = -1e30
QK_SCALE = 1.0 / math.sqrt(HEAD_DIM)
MEM_SCALE = 1.0 / math.sqrt(MEM_HEAD_DIM)

LANES = 128
HEAD_PAIRS = FOX_HEADS // 2
FOX_AUG_W = FOX_HEADS * LANES
FOX_OUT_W = 3 * FOX_AUG_W + FOX_W
FOX_ROWS = 128
FOX_KEYS = 256
LOOKAHEAD = 2
DIL_GROUP = 16
DIL_STAGE = 4
DIL_QKV_W = 3 * DIL_W
MEM_QG_W = 2 * MEM_W

_SEG_W = dict(fq=FOX_W, fk=FOX_W, fv=FOX_W, fg=FOX_W, dq=DIL_W, dk=DIL_W, dv=DIL_W, dg=DIL_W,
              mq=MEM_W, mg=MEM_W)
_SEG = dict(zip(_SEG_W, itertools.accumulate(_SEG_W.values(), initial=0)))
MAIN_W = sum(_SEG_W.values())

V7X_VMEM_BYTES = 64 * 1024 * 1024
VMEM_LIMIT = V7X_VMEM_BYTES - 8 * 1024 * 1024


def _silu(r):
    return r * (1.0 / (1.0 + jnp.exp(-r)))


def _rows(ref, start, size, stride):
    if stride == 1:
        return ref[pl.ds(start, size), :]
    return ref[pl.ds(start, size, stride=stride), :]


def _set_rows(ref, start, size, stride, val):
    if stride == 1:
        ref[pl.ds(start, size), :] = val
    else:
        ref[pl.ds(start, size, stride=stride), :] = val


def _in_proj_kernel(x_ref, g_ref, w_ref, wfl_ref, bf_ref, cos_ref, sina_ref, sinb_ref,
                    of_ref, od_ref, odg_ref, om_ref, carry_ref,
                    *, tm, tiles_per_seq):
    i = pl.program_id(0)
    x = x_ref[...]
    ms = jnp.mean(x * x, axis=-1, keepdims=True)
    h = (x * lax.rsqrt(ms + RMS_EPS) * g_ref[...]).astype(BF16)

    def seg(name):
        off = _SEG[name]
        return lax.dot_general(h, w_ref[off:off + _SEG_W[name], :], (((1,), (1,)), ((), ())),
                               preferred_element_type=F32)

    fl = lax.dot_general(h, wfl_ref[...], (((1,), (1,)), ((), ())),
                         preferred_element_type=F32) + bf_ref[...]
    logf = jnp.minimum(fl, 0.0) - jnp.log1p(jnp.exp(-jnp.abs(fl)))
    a1 = logf.astype(BF16)
    r1 = logf - a1.astype(F32)
    a2 = r1.astype(BF16)
    a3 = (r1 - a2.astype(F32)).astype(BF16)
    row = lax.broadcasted_iota(jnp.int32, (tm, tm), 0)
    col = lax.broadcasted_iota(jnp.int32, (tm, tm), 1)
    tri = jnp.where(row >= col, 1.0, 0.0).astype(BF16)

    @pl.when(i % tiles_per_seq == 0)
    def _():
        carry_ref[...] = jnp.zeros_like(carry_ref)

    c = (jnp.dot(tri, a1, preferred_element_type=F32)
         + jnp.dot(tri, a2, preferred_element_type=F32)
         + jnp.dot(tri, a3, preferred_element_type=F32)) + carry_ref[...]
    carry_ref[...] = c[tm - 1:tm, :]

    c1 = c.astype(BF16).astype(F32)
    c2 = (c - c1).astype(BF16).astype(F32)
    c3 = (c - c1 - c2).astype(BF16).astype(F32)
    fq = seg("fq") * QK_SCALE
    fk = seg("fk")
    fv = seg("fv")
    lane = lax.broadcasted_iota(jnp.int32, (tm, LANES), 1)
    for hd in range(FOX_HEADS):
        pair = slice(LANES * (hd // 2), LANES * (hd // 2 + 1))
        own = (lane < HEAD_DIM) if hd % 2 == 0 else (lane >= HEAD_DIM)
        base = HEAD_DIM if hd % 2 == 0 else 0
        b1, b2, b3 = (jnp.broadcast_to(t[:, hd:hd + 1], (tm, LANES)) for t in (c1, c2, c3))
        ones_q = jnp.where(jnp.logical_and(lane >= base, lane < base + 3), 1.0, 0.0)
        ones_k = jnp.where(jnp.logical_and(lane >= base + 3, lane < base + 6), 1.0, 0.0)
        q_bias = jnp.where(lane == base + 3, b1, jnp.where(lane == base + 4, b2,
                           jnp.where(lane == base + 5, b3, ones_q)))
        k_bias = jnp.where(lane == base, -b1, jnp.where(lane == base + 1, -b2,
                           jnp.where(lane == base + 2, -b3, ones_k)))
        dst = LANES * hd
        of_ref[:, dst:dst + LANES] = jnp.where(own, fq[:, pair], q_bias).astype(BF16)
        of_ref[:, FOX_AUG_W + dst:FOX_AUG_W + dst + LANES] = \
            jnp.where(own, fk[:, pair], k_bias).astype(BF16)
        of_ref[:, 2 * FOX_AUG_W + dst:2 * FOX_AUG_W + dst + LANES] = \
            jnp.where(own, fv[:, pair], 1.0).astype(BF16)
    of_ref[:, 3 * FOX_AUG_W:3 * FOX_AUG_W + FOX_W] = _silu(seg("fg")).astype(BF16)

    cos = cos_ref[...]
    sina = sina_ref[...]
    sinb = sinb_ref[...]

    def rope_store(r, dst_off, scale):
        for j in range(DIL_W // LANES):
            c = r[:, LANES * j:LANES * (j + 1)]
            rot = c * cos + pltpu.roll(c, ROPE_DIM // 2, 1) * sina \
                + pltpu.roll(c, LANES - ROPE_DIM // 2, 1) * sinb
            if scale != 1.0:
                rot = rot * scale
            od_ref[:, dst_off + LANES * j:dst_off + LANES * (j + 1)] = rot

    rope_store(seg("dq"), 0, QK_SCALE)
    rope_store(seg("dk"), DIL_W, 1.0)
    od_ref[:, 2 * DIL_W:DIL_QKV_W] = seg("dv")
    odg_ref[...] = _silu(seg("dg")).astype(BF16)

    om_ref[:, 0:MEM_W] = seg("mq").astype(BF16)
    om_ref[:, MEM_W:MEM_QG_W] = _silu(seg("mg")).astype(BF16)


def _in_proj(x2, g, w_main, w_fl, b_fl, cos_t, sina_t, sinb_t, *, seq, tm=512):
    n = x2.shape[0]
    tiles_per_seq = seq // tm
    const = lambda i: (0, 0)
    row = lambda i: (i, 0)
    pos = lambda i: (i % tiles_per_seq, 0)
    kernel = functools.partial(_in_proj_kernel, tm=tm, tiles_per_seq=tiles_per_seq)
    return pl.pallas_call(
        kernel,
        grid=(n // tm,),
        in_specs=[
            pl.BlockSpec((tm, D_MODEL), row),
            pl.BlockSpec((1, D_MODEL), const),
            pl.BlockSpec((MAIN_W, D_MODEL), const),
            pl.BlockSpec((LANES, D_MODEL), const),
            pl.BlockSpec((1, LANES), const),
            pl.BlockSpec((tm, LANES), pos),
            pl.BlockSpec((tm, LANES), pos),
            pl.BlockSpec((tm, LANES), pos),
        ],
        out_specs=[
            pl.BlockSpec((tm, FOX_OUT_W), row),
            pl.BlockSpec((tm, DIL_QKV_W), row),
            pl.BlockSpec((tm, DIL_W), row),
            pl.BlockSpec((tm, MEM_QG_W), row),
        ],
        out_shape=[
            jax.ShapeDtypeStruct((n, FOX_OUT_W), BF16),
            jax.ShapeDtypeStruct((n, DIL_QKV_W), F32),
            jax.ShapeDtypeStruct((n, DIL_W), BF16),
            jax.ShapeDtypeStruct((n, MEM_QG_W), BF16),
        ],
        scratch_shapes=[pltpu.VMEM((1, LANES), F32)],
        compiler_params=pltpu.CompilerParams(
            dimension_semantics=("arbitrary",), vmem_limit_bytes=VMEM_LIMIT),
        name="in_proj",
    )(x2, g, w_main, w_fl, b_fl, cos_t, sina_t, sinb_t)


def _fox_kernel(q_ref, k_ref, v_ref, g_ref, o_ref, acc_ref, m_ref, *, tq):
    i = pl.program_id(1)
    tk = FOX_KEYS
    row_chunks = tq // FOX_ROWS
    lane = lax.broadcasted_iota(jnp.int32, (FOX_ROWS, LANES), 1)
    lo = lane < HEAD_DIM
    row = lax.broadcasted_iota(jnp.int32, (FOX_ROWS, tk), 0)
    col = lax.broadcasted_iota(jnp.int32, (FOX_ROWS, tk), 1)
    bias_lo = jnp.where(row >= col, 0.0, NEG_INF)
    bias_hi = jnp.where(row + FOX_ROWS >= col, 0.0, NEG_INF)

    def run_chains(chains, first_visit=False):
        def scores(c):
            hd, rc, koff, klen, _ = chains[c]
            ls = slice(LANES * hd, LANES * (hd + 1))
            rs = slice(FOX_ROWS * rc, FOX_ROWS * (rc + 1))
            return lax.dot_general(q_ref[0, rs, ls], k_ref[0, pl.ds(koff, klen), ls],
                                   (((1,), (1,)), ((), ())), preferred_element_type=F32)

        pending = [scores(c) for c in range(LOOKAHEAD)]
        for c, (hd, rc, koff, klen, bias) in enumerate(chains):
            if c + LOOKAHEAD < len(chains):
                pending.append(scores(c + LOOKAHEAD))
            s = pending.pop(0)
            ls = slice(LANES * hd, LANES * (hd + 1))
            rs = slice(FOX_ROWS * rc, FOX_ROWS * (rc + 1))
            if bias is not None:
                s = s + bias
            m_new = jnp.broadcast_to(jnp.max(s, axis=-1, keepdims=True), (FOX_ROWS, LANES))
            if not first_visit:
                m_old = m_ref[hd, rs, :]
                m_new = jnp.maximum(m_old, m_new)
            p = jnp.concatenate(
                [jnp.exp(s[:, LANES * t:LANES * (t + 1)] - m_new) for t in range(klen // LANES)],
                axis=1)
            pv = jnp.dot(p.astype(BF16), v_ref[0, pl.ds(koff, klen), ls],
                         preferred_element_type=F32)
            if first_visit:
                acc_ref[hd, rs, :] = pv
            else:
                acc_ref[hd, rs, :] = jnp.exp(m_old - m_new) * acc_ref[hd, rs, :] + pv
            m_ref[hd, rs, :] = m_new

    def body(j, _):
        first = pl.multiple_of(j * tq, tq)
        run_chains([(hd, rc, first + kt * tk, tk, None)
                    for kt in range(tq // tk) for hd in range(FOX_HEADS) for rc in range(row_chunks)])
        return 0

    def diag_body(t, _, masked):
        first = pl.multiple_of(t * tq, tq)
        chains = []
        for kt in range(tq // tk):
            koff = first + kt * tk
            for hd in range(FOX_HEADS):
                if masked:
                    chains += [(hd, rc, koff, tk, bias_lo if rc % 2 == 0 else bias_hi)
                               for rc in (2 * kt, 2 * kt + 1)]
                else:
                    chains += [(hd, rc, koff, tk, None) for rc in range(2 * kt + 2, row_chunks)]
        run_chains(chains, first_visit=masked)
        return 0

    lax.fori_loop(i, i + 1, functools.partial(diag_body, masked=True), 0)
    lax.fori_loop(i, i + 1, functools.partial(diag_body, masked=False), 0)
    lax.fori_loop(0, i, body, 0)

    for hp in range(HEAD_PAIRS):
        ls = slice(LANES * hp, LANES * (hp + 1))
        for rc in range(row_chunks):
            rs = slice(FOX_ROWS * rc, FOX_ROWS * (rc + 1))
            a0 = acc_ref[2 * hp, rs, :]
            a1 = acc_ref[2 * hp + 1, rs, :]
            dens = pltpu.roll(jnp.where(lo, a1, a0), HEAD_DIM, 1)
            o2 = jnp.where(lo, a0, a1) / dens
            o_ref[0, rs, ls] = (o2 * g_ref[0, rs, ls].astype(F32)).astype(BF16)


def _fox(of3, *, tq=512):
    batch, seq, _ = of3.shape
    kernel = functools.partial(_fox_kernel, tq=tq)
    gate_blk = 3 * FOX_AUG_W // FOX_W
    return pl.pallas_call(
        kernel,
        grid=(batch, seq // tq),
        in_specs=[
            pl.BlockSpec((1, tq, FOX_AUG_W), lambda b, i: (b, i, 0)),
            pl.BlockSpec((1, seq, FOX_AUG_W), lambda b, i: (b, 0, 1)),
            pl.BlockSpec((1, seq, FOX_AUG_W), lambda b, i: (b, 0, 2)),
            pl.BlockSpec((1, tq, FOX_W), lambda b, i: (b, i, gate_blk)),
        ],
        out_specs=pl.BlockSpec((1, tq, FOX_W), lambda b, i: (b, i, 0)),
        out_shape=jax.ShapeDtypeStruct((batch, seq, FOX_W), BF16),
        scratch_shapes=[
            pltpu.VMEM((FOX_HEADS, tq, LANES), F32),
            pltpu.VMEM((FOX_HEADS, tq, LANES), F32),
        ],
        compiler_params=pltpu.CompilerParams(
            dimension_semantics=("arbitrary", "arbitrary"), vmem_limit_bytes=VMEM_LIMIT),
        name="fox_attn",
    )(of3, of3, of3, of3)


def _dil_kernel(q_ref, k_ref, v_ref, g_ref, o_ref, q0, q1, kd, v0, v1, nd_ref, m_ref,
                tq_ref, tk_ref, tv_ref, und_ref, um_ref, *, seq):
    n_blocks = seq // BLOCK
    lane = lax.broadcasted_iota(jnp.int32, (BLOCK, LANES), 1)
    lo = lane < HEAD_DIM
    row2 = lax.broadcasted_iota(jnp.int32, (BLOCK, 2 * BLOCK), 0)
    col2 = lax.broadcasted_iota(jnp.int32, (BLOCK, 2 * BLOCK), 1)
    band = jnp.logical_and(col2 >= row2, col2 <= row2 + BLOCK)
    bias_band = jnp.where(band, 0.0, NEG_INF)
    bias_first = jnp.where(jnp.logical_and(band, col2 >= BLOCK), 0.0, NEG_INF)

    kd[0:BLOCK, :] = jnp.zeros((BLOCK, LANES), BF16)
    v0[0:BLOCK, :] = jnp.zeros((BLOCK, LANES), BF16)
    v1[0:BLOCK, :] = jnp.zeros((BLOCK, LANES), BF16)

    order = sorted(range(len(DILATIONS)), key=lambda p: (DILATIONS[p][1] == 1, DILATIONS[p][1]))
    for slot in order:
        window, d = DILATIONS[slot]
        assert window // d == BLOCK
        length = seq // d
        nb = length // BLOCK
        lo_l = lax.broadcasted_iota(jnp.int32, (length, LANES), 1) < HEAD_DIM
        for r in range(d):
            qdst = slice(r * length, (r + 1) * length)
            kdst = slice(BLOCK + r * length, BLOCK + (r + 1) * length)
            if d == DIL_STAGE * DIL_STAGE:
                r4, j = r % DIL_STAGE, r // DIL_STAGE
                start = r4 * (seq // DIL_STAGE) + j
                qx, kx, vx = (_rows(t, start, length, DIL_STAGE) for t in (tq_ref, tk_ref, tv_ref))
            else:
                qx, kx, vx = (_rows(t.at[0], r, length, d) for t in (q_ref, k_ref, v_ref))
            if d == DIL_STAGE:
                tq_ref[qdst, :] = qx
                tk_ref[qdst, :] = kx
                tv_ref[qdst, :] = vx
            qf = qx.astype(BF16)
            q0[qdst, :] = jnp.where(lo_l, qf, jnp.zeros_like(qf))
            q1[qdst, :] = jnp.where(lo_l, jnp.zeros_like(qf), qf)
            kd[kdst, :] = kx.astype(BF16)
            vf = vx.astype(BF16)
            v0[kdst, :] = jnp.where(lo_l, vf, jnp.ones_like(vf))
            v1[kdst, :] = jnp.where(lo_l, jnp.ones_like(vf), vf)

        def body(g, _, slot=slot, d=d, nb=nb):
            qv = ((q0, v0), (q1, v1))
            for u, hh in [(u, hh) for u in range(DIL_GROUP) for hh in range(2)]:
                t = g * DIL_GROUP + u
                base = pl.multiple_of(t * BLOCK, BLOCK)
                if nb == 1:
                    bias = bias_first
                    nat_start = t
                else:
                    r = t // nb
                    n = t - r * nb
                    bias = jnp.where(n == 0, bias_first, bias_band)
                    nat_start = base if d == 1 else n * (BLOCK * d) + r
                s = lax.dot_general(qv[hh][0][pl.ds(base, BLOCK), :], kd[pl.ds(base, 2 * BLOCK), :],
                                    (((1,), (1,)), ((), ())), preferred_element_type=F32) + bias
                mx = jnp.broadcast_to(jnp.max(s, axis=-1, keepdims=True), (BLOCK, LANES))
                e = jnp.concatenate(
                    [jnp.exp(s[:, LANES * t2:LANES * (t2 + 1)] - mx) for t2 in range(2)], axis=1)
                nd = jnp.dot(e.astype(BF16), qv[hh][1][pl.ds(base, 2 * BLOCK), :],
                             preferred_element_type=F32)
                if d == DIL_STAGE * DIL_STAGE:
                    stage = (t % DIL_STAGE) * (seq // DIL_STAGE) + t // DIL_STAGE
                    _set_rows(und_ref.at[hh], stage, BLOCK, DIL_STAGE, nd)
                    _set_rows(um_ref.at[hh], stage, BLOCK, DIL_STAGE, mx)
                else:
                    _set_rows(nd_ref.at[slot, hh], nat_start, BLOCK, d, nd)
                    _set_rows(m_ref.at[slot, hh], nat_start, BLOCK, d, mx)
            return 0

        lax.fori_loop(0, n_blocks // DIL_GROUP, body, 0)
        if d == DIL_STAGE * DIL_STAGE:
            for hh in range(2):
                for r4 in range(DIL_STAGE):
                    src = slice(r4 * (seq // DIL_STAGE), (r4 + 1) * (seq // DIL_STAGE))
                    _set_rows(nd_ref.at[slot, hh], r4, seq // DIL_STAGE, DIL_STAGE, und_ref[hh, src, :])
                    _set_rows(m_ref.at[slot, hh], r4, seq // DIL_STAGE, DIL_STAGE, um_ref[hh, src, :])

    def merge(c, _):
        rs = pl.ds(pl.multiple_of(c * BLOCK, BLOCK), BLOCK)
        tots = []
        for hh in range(2):
            ms = [m_ref[p, hh, rs, :] for p in range(len(DILATIONS))]
            m_all = jnp.maximum(jnp.maximum(ms[0], ms[1]), ms[2])
            tot = nd_ref[0, hh, rs, :] * jnp.exp(ms[0] - m_all)
            for p in range(1, len(DILATIONS)):
                tot = tot + nd_ref[p, hh, rs, :] * jnp.exp(ms[p] - m_all)
            tots.append(tot)
        dens = pltpu.roll(jnp.where(lo, tots[1], tots[0]), HEAD_DIM, 1)
        o = jnp.where(lo, tots[0], tots[1]) / dens * g_ref[0, rs, :].astype(F32)
        o_ref[0, rs, :] = o.astype(BF16)
        return 0

    lax.fori_loop(0, n_blocks, merge, 0, unroll=4)


def _dilated(od3, odg3):
    batch, seq, _ = od3.shape
    kernel = functools.partial(_dil_kernel, seq=seq)
    n_pat = len(DILATIONS)
    return pl.pallas_call(
        kernel,
        grid=(batch, HEAD_PAIRS),
        in_specs=[
            pl.BlockSpec((1, seq, LANES), lambda b, p: (b, 0, p)),
            pl.BlockSpec((1, seq, LANES), lambda b, p: (b, 0, HEAD_PAIRS + p)),
            pl.BlockSpec((1, seq, LANES), lambda b, p: (b, 0, 2 * HEAD_PAIRS + p)),
            pl.BlockSpec((1, seq, LANES), lambda b, p: (b, 0, p)),
        ],
        out_specs=pl.BlockSpec((1, seq, LANES), lambda b, p: (b, 0, p)),
        out_shape=jax.ShapeDtypeStruct((batch, seq, DIL_W), BF16),
        scratch_shapes=[
            pltpu.VMEM((seq, LANES), BF16),
            pltpu.VMEM((seq, LANES), BF16),
            pltpu.VMEM((seq + BLOCK, LANES), BF16),
            pltpu.VMEM((seq + BLOCK, LANES), BF16),
            pltpu.VMEM((seq + BLOCK, LANES), BF16),
            pltpu.VMEM((n_pat, 2, seq, LANES), F32),
            pltpu.VMEM((n_pat, 2, seq, LANES), F32),
            pltpu.VMEM((seq, LANES), F32),
            pltpu.VMEM((seq, LANES), F32),
            pltpu.VMEM((seq, LANES), F32),
            pltpu.VMEM((2, seq, LANES), F32),
            pltpu.VMEM((2, seq, LANES), F32),
        ],
        compiler_params=pltpu.CompilerParams(
            dimension_semantics=("arbitrary", "arbitrary"), vmem_limit_bytes=VMEM_LIMIT),
        name="dilated_attn",
    )(od3, od3, od3, odg3)


def _mem_kv_kernel(mem_ref, g_ref, w_ref, o_ref):
    x = mem_ref[0]
    ms = jnp.mean(x * x, axis=-1, keepdims=True)
    h = (x * lax.rsqrt(ms + RMS_EPS) * g_ref[...]).astype(BF16)
    o_ref[0] = jnp.dot(h, w_ref[...], preferred_element_type=F32).astype(BF16)


def _mem_kv(mem, g, w):
    batch, mlen, _ = mem.shape
    return pl.pallas_call(
        _mem_kv_kernel,
        grid=(batch,),
        in_specs=[
            pl.BlockSpec((1, mlen, D_MODEL), lambda b: (b, 0, 0)),
            pl.BlockSpec((1, D_MODEL), lambda b: (0, 0)),
            pl.BlockSpec((D_MODEL, 2 * MEM_W), lambda b: (0, 0)),
        ],
        out_specs=pl.BlockSpec((1, mlen, 2 * MEM_W), lambda b: (b, 0, 0)),
        out_shape=jax.ShapeDtypeStruct((batch, mlen, 2 * MEM_W), BF16),
        compiler_params=pltpu.CompilerParams(
            dimension_semantics=("arbitrary",), vmem_limit_bytes=VMEM_LIMIT),
        name="mem_kv",
    )(mem, g, w)


def _mem_attn_kernel(q_ref, g_ref, mk_ref, mv_ref, o_ref):
    tq = q_ref.shape[1]
    chains = [(h, rc) for h in range(MEM_HEADS) for rc in range(tq // BLOCK)]

    def scores(c):
        h, rc = chains[c]
        ls = slice(MEM_HEAD_DIM * h, MEM_HEAD_DIM * (h + 1))
        return lax.dot_general(q_ref[0, BLOCK * rc:BLOCK * (rc + 1), ls], mk_ref[0, :, ls],
                               (((1,), (1,)), ((), ())), preferred_element_type=F32)

    pending = [scores(c) for c in range(LOOKAHEAD)]
    for c, (h, rc) in enumerate(chains):
        if c + LOOKAHEAD < len(chains):
            pending.append(scores(c + LOOKAHEAD))
        ls = slice(MEM_HEAD_DIM * h, MEM_HEAD_DIM * (h + 1))
        rs = slice(BLOCK * rc, BLOCK * (rc + 1))
        s = pending.pop(0) * MEM_SCALE
        mx = jnp.max(s, axis=-1, keepdims=True)
        e = jnp.exp(s - mx)
        den = jnp.sum(e, axis=-1, keepdims=True)
        o = jnp.dot(e.astype(BF16), mv_ref[0, :, ls], preferred_element_type=F32) / den
        o_ref[0, rs, ls] = (o * g_ref[0, rs, ls].astype(F32)).astype(BF16)


def _mem_attn(om3, mkv, *, tq=2048):
    batch, seq, _ = om3.shape
    mlen = mkv.shape[1]
    return pl.pallas_call(
        _mem_attn_kernel,
        grid=(batch, seq // tq),
        in_specs=[
            pl.BlockSpec((1, tq, MEM_W), lambda b, i: (b, i, 0)),
            pl.BlockSpec((1, tq, MEM_W), lambda b, i: (b, i, 1)),
            pl.BlockSpec((1, mlen, MEM_W), lambda b, i: (b, 0, 0)),
            pl.BlockSpec((1, mlen, MEM_W), lambda b, i: (b, 0, 1)),
        ],
        out_specs=pl.BlockSpec((1, tq, MEM_W), lambda b, i: (b, i, 0)),
        out_shape=jax.ShapeDtypeStruct((batch, seq, MEM_W), BF16),
        compiler_params=pltpu.CompilerParams(
            dimension_semantics=("arbitrary", "arbitrary"), vmem_limit_bytes=VMEM_LIMIT),
        name="mem_attn",
    )(om3, om3, mkv, mkv)


def _out_proj_kernel(x_ref, yf_ref, yd_ref, ym_ref, w_ref, g_ref, o_ref, *, final_norm):
    z = x_ref[...]
    z = z + jnp.dot(yf_ref[...], w_ref[0:FOX_W, :], preferred_element_type=F32)
    z = z + jnp.dot(yd_ref[...], w_ref[FOX_W:FOX_W + DIL_W, :], preferred_element_type=F32)
    z = z + jnp.dot(ym_ref[...], w_ref[FOX_W + DIL_W:MIX_W, :], preferred_element_type=F32)
    if final_norm:
        ms = jnp.mean(z * z, axis=-1, keepdims=True)
        z = z * lax.rsqrt(ms + RMS_EPS) * g_ref[...]
    o_ref[...] = z


def _out_proj(x2, yf, yd, ym, w, g, *, final_norm, tm=1024):
    n = x2.shape[0]
    row = lambda i: (i, 0)
    const = lambda i: (0, 0)
    kernel = functools.partial(_out_proj_kernel, final_norm=final_norm)
    return pl.pallas_call(
        kernel,
        grid=(n // tm,),
        in_specs=[
            pl.BlockSpec((tm, D_MODEL), row),
            pl.BlockSpec((tm, FOX_W), row),
            pl.BlockSpec((tm, DIL_W), row),
            pl.BlockSpec((tm, MEM_W), row),
            pl.BlockSpec((MIX_W, D_MODEL), const),
            pl.BlockSpec((1, D_MODEL), const),
        ],
        out_specs=pl.BlockSpec((tm, D_MODEL), row),
        out_shape=jax.ShapeDtypeStruct((n, D_MODEL), F32),
        compiler_params=pltpu.CompilerParams(
            dimension_semantics=("arbitrary",), vmem_limit_bytes=VMEM_LIMIT),
        name="out_proj",
    )(x2, yf, yd, ym, w, g)


def _rope_tables(seq):
    half = ROPE_DIM // 2
    pos = jnp.arange(seq, dtype=F32)
    inv_freq = 1.0 / (ROPE_THETA ** (jnp.arange(0, ROPE_DIM, 2, dtype=F32) / ROPE_DIM))
    ang = pos[:, None] * inv_freq[None, :]
    cos = jnp.cos(ang)
    sin = jnp.sin(ang)
    ones = jnp.ones((seq, HEAD_DIM - ROPE_DIM), F32)
    zeros = jnp.zeros((seq, HEAD_DIM - ROPE_DIM), F32)
    zh = jnp.zeros((seq, half), F32)
    cos_h = jnp.concatenate([cos, cos, ones], axis=1)
    sina_h = jnp.concatenate([zh, sin, zeros], axis=1)
    sinb_h = jnp.concatenate([-sin, zh, zeros], axis=1)
    tile = lambda t: jnp.concatenate([t, t], axis=1)
    return tile(cos_h), tile(sina_h), tile(sinb_h)


def _split_w_in(w):
    sizes = [FOX_W] * 4 + [FOX_HEADS] + [DIL_W] * 4 + [MEM_W] * 2
    offs = [0]
    for s in sizes:
        offs.append(offs[-1] + s)
    wt = jnp.swapaxes(w, 0, 1).astype(BF16)
    rows = [wt[offs[k]:offs[k + 1]] for k in range(len(sizes))]
    fq, fk, fv, fg, flog, dq, dk, dv, dg, mq, mg = rows
    w_main = jnp.concatenate([fq, fk, fv, fg, dq, dk, dv, dg, mq, mg], axis=0)
    w_fl = jnp.pad(flog, ((0, LANES - FOX_HEADS), (0, 0)))
    return w_main, w_fl


def kernel(x, mem, norm_g, w_in, b_forget, mem_norm_g, w_mem_kv, w_out, final_norm_g):
    batch, seq, _ = x.shape
    depth = norm_g.shape[0]
    cos_t, sina_t, sinb_t = _rope_tables(seq)
    x2 = x.reshape(batch * seq, D_MODEL)
    for l in range(depth):
        w_main, w_fl = _split_w_in(w_in[l])
        b_fl = jnp.pad(b_forget[l], (0, LANES - FOX_HEADS)).reshape(1, LANES)
        of, od, odg, om = _in_proj(
            x2, norm_g[l].reshape(1, D_MODEL), w_main, w_fl, b_fl, cos_t, sina_t, sinb_t, seq=seq)
        yf = _fox(of.reshape(batch, seq, FOX_OUT_W))
        yd = _dilated(od.reshape(batch, seq, DIL_QKV_W), odg.reshape(batch, seq, DIL_W))
        mkv = _mem_kv(mem, mem_norm_g[l].reshape(1, D_MODEL), w_mem_kv[l].astype(BF16))
        ym = _mem_attn(om.reshape(batch, seq, MEM_QG_W), mkv)
        last = l == depth - 1
        x2 = _out_proj(x2, yf.reshape(batch * seq, FOX_W), yd.reshape(batch * seq, DIL_W),
                       ym.reshape(batch * seq, MEM_W), w_out[l].astype(BF16),
                       final_norm_g.reshape(1, D_MODEL), final_norm=last)
    return x2.reshape(batch, seq, D_MODEL)
```

```python
import functools
import itertools
import math

import jax
import jax.numpy as jnp
from jax import lax
from jax.experimental import pallas as pl
from jax.experimental.pallas import tpu as pltpu

F32 = jnp.float32
BF16 = jnp.bfloat16

D_MODEL = 1024
HEAD_DIM = 64
FOX_HEADS = 12
DIL_HEADS = 12
MEM_HEADS = 4
MEM_HEAD_DIM = 128
FOX_W = FOX_HEADS * HEAD_DIM
DIL_W = DIL_HEADS * HEAD_DIM
MEM_W = MEM_HEADS * MEM_HEAD_DIM
MIX_W = FOX_W + DIL_W + MEM_W
DILATIONS = ((128, 1), (512, 4), (2048, 16))
BLOCK = 128
ROPE_THETA = 500000.0
ROPE_DIM = HEAD_DIM // 4
RMS_EPS = 1e-6
NEG_INF = -1e30
QK_SCALE = 1.0 / math.sqrt(HEAD_DIM)
MEM_SCALE = 1.0 / math.sqrt(MEM_HEAD_DIM)

LANES = 128
HEAD_PAIRS = FOX_HEADS // 2
FOX_AUG_W = FOX_HEADS * LANES
FOX_OUT_W = 3 * FOX_AUG_W + FOX_W
FOX_ROWS = 128
FOX_KEYS = 256
LOOKAHEAD = 2
DIL_GROUP = 16
DIL_STAGE = 4
DIL_QKV_W = 3 * DIL_W
MEM_QG_W = 2 * MEM_W

_SEG_W = dict(fq=FOX_W, fk=FOX_W, fv=FOX_W, fg=FOX_W, dq=DIL_W, dk=DIL_W, dv=DIL_W, dg=DIL_W,
              mq=MEM_W, mg=MEM_W)
_SEG = dict(zip(_SEG_W, itertools.accumulate(_SEG_W.values(), initial=0)))
MAIN_W = sum(_SEG_W.values())
FOX_SEG_W = 4 * FOX_W

V7X_VMEM_BYTES = 64 * 1024 * 1024
VMEM_LIMIT = V7X_VMEM_BYTES - 8 * 1024 * 1024


def _silu(r):
    return r * (1.0 / (1.0 + jnp.exp(-r)))


def _rows(ref, start, size, stride):
    if stride == 1:
        return ref[pl.ds(start, size), :]
    return ref[pl.ds(start, size, stride=stride), :]


def _set_rows(ref, start, size, stride, val):
    if stride == 1:
        ref[pl.ds(start, size), :] = val
    else:
        ref[pl.ds(start, size, stride=stride), :] = val


def _in_proj_kernel(x_ref, g_ref, wfox_ref, wfl_ref, wrest_ref, bf_ref, cos_ref, sina_ref, sinb_ref,
                    of_ref, od_ref, odg_ref, om_ref, carry_ref,
                    *, tm, tiles_per_seq):
    i = pl.program_id(0)
    x = x_ref[...]
    ms = jnp.mean(x * x, axis=-1, keepdims=True)
    h = (x * lax.rsqrt(ms + RMS_EPS) * g_ref[...]).astype(BF16)

    def seg(name):
        off = _SEG[name]
        w_ref = wfox_ref if off < FOX_SEG_W else wrest_ref
        off = off if off < FOX_SEG_W else off - FOX_SEG_W
        return lax.dot_general(h, w_ref[off:off + _SEG_W[name], :], (((1,), (1,)), ((), ())),
                               preferred_element_type=F32)

    fl = lax.dot_general(h, wfl_ref[...], (((1,), (1,)), ((), ())),
                         preferred_element_type=F32) + bf_ref[...]
    logf = jnp.minimum(fl, 0.0) - jnp.log1p(jnp.exp(-jnp.abs(fl)))
    a1 = logf.astype(BF16)
    r1 = logf - a1.astype(F32)
    a2 = r1.astype(BF16)
    a3 = (r1 - a2.astype(F32)).astype(BF16)
    row = lax.broadcasted_iota(jnp.int32, (tm, tm), 0)
    col = lax.broadcasted_iota(jnp.int32, (tm, tm), 1)
    tri = jnp.where(row >= col, 1.0, 0.0).astype(BF16)

    @pl.when(i % tiles_per_seq == 0)
    def _():
        carry_ref[...] = jnp.zeros_like(carry_ref)

    c = (jnp.dot(tri, a1, preferred_element_type=F32)
         + jnp.dot(tri, a2, preferred_element_type=F32)
         + jnp.dot(tri, a3, preferred_element_type=F32)) + carry_ref[...]
    carry_ref[...] = c[tm - 1:tm, :]

    c1 = c.astype(BF16).astype(F32)
    c2 = (c - c1).astype(BF16).astype(F32)
    c3 = (c - c1 - c2).astype(BF16).astype(F32)
    fq = seg("fq") * QK_SCALE
    fk = seg("fk")
    fv = seg("fv")
    lane = lax.broadcasted_iota(jnp.int32, (tm, LANES), 1)
    for hd in range(FOX_HEADS):
        pair = slice(LANES * (hd // 2), LANES * (hd // 2 + 1))
        own = (lane < HEAD_DIM) if hd % 2 == 0 else (lane >= HEAD_DIM)
        base = HEAD_DIM if hd % 2 == 0 else 0
        b1, b2, b3 = (jnp.broadcast_to(t[:, hd:hd + 1], (tm, LANES)) for t in (c1, c2, c3))
        ones_q = jnp.where(jnp.logical_and(lane >= base, lane < base + 3), 1.0, 0.0)
        ones_k = jnp.where(jnp.logical_and(lane >= base + 3, lane < base + 6), 1.0, 0.0)
        q_bias = jnp.where(lane == base + 3, b1, jnp.where(lane == base + 4, b2,
                           jnp.where(lane == base + 5, b3, ones_q)))
        k_bias = jnp.where(lane == base, -b1, jnp.where(lane == base + 1, -b2,
                           jnp.where(lane == base + 2, -b3, ones_k)))
        dst = LANES * hd
        of_ref[:, dst:dst + LANES] = jnp.where(own, fq[:, pair], q_bias).astype(BF16)
        of_ref[:, FOX_AUG_W + dst:FOX_AUG_W + dst + LANES] = \
            jnp.where(own, fk[:, pair], k_bias).astype(BF16)
        of_ref[:, 2 * FOX_AUG_W + dst:2 * FOX_AUG_W + dst + LANES] = \
            jnp.where(own, fv[:, pair], 1.0).astype(BF16)
    of_ref[:, 3 * FOX_AUG_W:3 * FOX_AUG_W + FOX_W] = _silu(seg("fg")).astype(BF16)

    cos = cos_ref[...]
    sina = sina_ref[...]
    sinb = sinb_ref[...]

    def rope_store(r, dst_off, scale):
        for j in range(DIL_W // LANES):
            c = r[:, LANES * j:LANES * (j + 1)]
            rot = c * cos + pltpu.roll(c, ROPE_DIM // 2, 1) * sina \
                + pltpu.roll(c, LANES - ROPE_DIM // 2, 1) * sinb
            if scale != 1.0:
                rot = rot * scale
            od_ref[:, dst_off + LANES * j:dst_off + LANES * (j + 1)] = rot

    rope_store(seg("dq"), 0, QK_SCALE)
    rope_store(seg("dk"), DIL_W, 1.0)
    od_ref[:, 2 * DIL_W:DIL_QKV_W] = seg("dv")
    odg_ref[...] = _silu(seg("dg")).astype(BF16)

    om_ref[:, 0:MEM_W] = seg("mq").astype(BF16)
    om_ref[:, MEM_W:MEM_QG_W] = _silu(seg("mg")).astype(BF16)


def _in_proj(x2, g, w_fox, w_fl, w_rest, b_fl, cos_t, sina_t, sinb_t, *, seq, tm=512):
    n = x2.shape[0]
    tiles_per_seq = seq // tm
    const = lambda i: (0, 0)
    row = lambda i: (i, 0)
    pos = lambda i: (i % tiles_per_seq, 0)
    kernel = functools.partial(_in_proj_kernel, tm=tm, tiles_per_seq=tiles_per_seq)
    return pl.pallas_call(
        kernel,
        grid=(n // tm,),
        in_specs=[
            pl.BlockSpec((tm, D_MODEL), row),
            pl.BlockSpec((1, D_MODEL), const),
            pl.BlockSpec((FOX_SEG_W, D_MODEL), const),
            pl.BlockSpec((LANES, D_MODEL), const),
            pl.BlockSpec((MAIN_W - FOX_SEG_W, D_MODEL), const),
            pl.BlockSpec((1, LANES), const),
            pl.BlockSpec((tm, LANES), pos),
            pl.BlockSpec((tm, LANES), pos),
            pl.BlockSpec((tm, LANES), pos),
        ],
        out_specs=[
            pl.BlockSpec((tm, FOX_OUT_W), row),
            pl.BlockSpec((tm, DIL_QKV_W), row),
            pl.BlockSpec((tm, DIL_W), row),
            pl.BlockSpec((tm, MEM_QG_W), row),
        ],
        out_shape=[
            jax.ShapeDtypeStruct((n, FOX_OUT_W), BF16),
            jax.ShapeDtypeStruct((n, DIL_QKV_W), F32),
            jax.ShapeDtypeStruct((n, DIL_W), BF16),
            jax.ShapeDtypeStruct((n, MEM_QG_W), BF16),
        ],
        scratch_shapes=[pltpu.VMEM((1, LANES), F32)],
        compiler_params=pltpu.CompilerParams(
            dimension_semantics=("arbitrary",), vmem_limit_bytes=VMEM_LIMIT),
        name="in_proj",
    )(x2, g, w_fox, w_fl, w_rest, b_fl, cos_t, sina_t, sinb_t)


def _fox_kernel(q_ref, k_ref, v_ref, g_ref, o_ref, acc_ref, m_ref, *, tq):
    i = pl.program_id(1)
    tk = FOX_KEYS
    row_chunks = tq // FOX_ROWS
    lane = lax.broadcasted_iota(jnp.int32, (FOX_ROWS, LANES), 1)
    lo = lane < HEAD_DIM
    row = lax.broadcasted_iota(jnp.int32, (FOX_ROWS, tk), 0)
    col = lax.broadcasted_iota(jnp.int32, (FOX_ROWS, tk), 1)
    bias_lo = jnp.where(row >= col, 0.0, NEG_INF)
    bias_hi = jnp.where(row + FOX_ROWS >= col, 0.0, NEG_INF)

    def run_chains(chains, first_visit=False):
        def scores(c):
            hd, rc, koff, klen, _ = chains[c]
            ls = slice(LANES * hd, LANES * (hd + 1))
            rs = slice(FOX_ROWS * rc, FOX_ROWS * (rc + 1))
            return lax.dot_general(q_ref[0, rs, ls], k_ref[0, pl.ds(koff, klen), ls],
                                   (((1,), (1,)), ((), ())), preferred_element_type=F32)

        pending = [scores(c) for c in range(LOOKAHEAD)]
        for c, (hd, rc, koff, klen, bias) in enumerate(chains):
            if c + LOOKAHEAD < len(chains):
                pending.append(scores(c + LOOKAHEAD))
            s = pending.pop(0)
            ls = slice(LANES * hd, LANES * (hd + 1))
            rs = slice(FOX_ROWS * rc, FOX_ROWS * (rc + 1))
            if bias is not None:
                s = s + bias
            m_new = jnp.broadcast_to(jnp.max(s, axis=-1, keepdims=True), (FOX_ROWS, LANES))
            if not first_visit:
                m_old = m_ref[hd, rs, :]
                m_new = jnp.maximum(m_old, m_new)
            p = jnp.concatenate(
                [jnp.exp(s[:, LANES * t:LANES * (t + 1)] - m_new) for t in range(klen // LANES)],
                axis=1)
            pv = jnp.dot(p.astype(BF16), v_ref[0, pl.ds(koff, klen), ls],
                         preferred_element_type=F32)
            if first_visit:
                acc_ref[hd, rs, :] = pv
            else:
                acc_ref[hd, rs, :] = jnp.exp(m_old - m_new) * acc_ref[hd, rs, :] + pv
            m_ref[hd, rs, :] = m_new

    def body(j, _):
        first = pl.multiple_of(j * tq, tq)
        run_chains([(hd, rc, first + kt * tk, tk, None)
                    for kt in range(tq // tk) for hd in range(FOX_HEADS) for rc in range(row_chunks)])
        return 0

    def diag_body(t, _, masked):
        first = pl.multiple_of(t * tq, tq)
        chains = []
        for kt in range(tq // tk):
            koff = first + kt * tk
            for hd in range(FOX_HEADS):
                if masked:
                    chains += [(hd, rc, koff, tk, bias_lo if rc % 2 == 0 else bias_hi)
                               for rc in (2 * kt, 2 * kt + 1)]
                else:
                    chains += [(hd, rc, koff, tk, None) for rc in range(2 * kt + 2, row_chunks)]
        run_chains(chains, first_visit=masked)
        return 0

    lax.fori_loop(i, i + 1, functools.partial(diag_body, masked=True), 0)
    lax.fori_loop(i, i + 1, functools.partial(diag_body, masked=False), 0)
    lax.fori_loop(0, i, body, 0)

    for hp in range(HEAD_PAIRS):
        ls = slice(LANES * hp, LANES * (hp + 1))
        for rc in range(row_chunks):
            rs = slice(FOX_ROWS * rc, FOX_ROWS * (rc + 1))
            a0 = acc_ref[2 * hp, rs, :]
            a1 = acc_ref[2 * hp + 1, rs, :]
            dens = pltpu.roll(jnp.where(lo, a1, a0), HEAD_DIM, 1)
            o2 = jnp.where(lo, a0, a1) / dens
            o_ref[0, rs, ls] = (o2 * g_ref[0, rs, ls].astype(F32)).astype(BF16)


def _fox(of3, *, tq=512):
    batch, seq, _ = of3.shape
    kernel = functools.partial(_fox_kernel, tq=tq)
    gate_blk = 3 * FOX_AUG_W // FOX_W
    return pl.pallas_call(
        kernel,
        grid=(batch, seq // tq),
        in_specs=[
            pl.BlockSpec((1, tq, FOX_AUG_W), lambda b, i: (b, i, 0)),
            pl.BlockSpec((1, seq, FOX_AUG_W), lambda b, i: (b, 0, 1)),
            pl.BlockSpec((1, seq, FOX_AUG_W), lambda b, i: (b, 0, 2)),
            pl.BlockSpec((1, tq, FOX_W), lambda b, i: (b, i, gate_blk)),
        ],
        out_specs=pl.BlockSpec((1, tq, FOX_W), lambda b, i: (b, i, 0)),
        out_shape=jax.ShapeDtypeStruct((batch, seq, FOX_W), BF16),
        scratch_shapes=[
            pltpu.VMEM((FOX_HEADS, tq, LANES), F32),
            pltpu.VMEM((FOX_HEADS, tq, LANES), F32),
        ],
        compiler_params=pltpu.CompilerParams(
            dimension_semantics=("arbitrary", "arbitrary"), vmem_limit_bytes=VMEM_LIMIT),
        name="fox_attn",
    )(of3, of3, of3, of3)


def _dil_kernel(q_ref, k_ref, v_ref, g_ref, o_ref, q0, q1, kd, v0, v1, nd_ref, m_ref,
                tq_ref, tk_ref, tv_ref, und_ref, um_ref, *, seq):
    n_blocks = seq // BLOCK
    lane = lax.broadcasted_iota(jnp.int32, (BLOCK, LANES), 1)
    lo = lane < HEAD_DIM
    row2 = lax.broadcasted_iota(jnp.int32, (BLOCK, 2 * BLOCK), 0)
    col2 = lax.broadcasted_iota(jnp.int32, (BLOCK, 2 * BLOCK), 1)
    band = jnp.logical_and(col2 >= row2, col2 <= row2 + BLOCK)
    bias_band = jnp.where(band, 0.0, NEG_INF)
    bias_first = jnp.where(jnp.logical_and(band, col2 >= BLOCK), 0.0, NEG_INF)

    kd[0:BLOCK, :] = jnp.zeros((BLOCK, LANES), BF16)
    v0[0:BLOCK, :] = jnp.zeros((BLOCK, LANES), BF16)
    v1[0:BLOCK, :] = jnp.zeros((BLOCK, LANES), BF16)

    order = sorted(range(len(DILATIONS)), key=lambda p: (DILATIONS[p][1] == 1, DILATIONS[p][1]))
    for slot in order:
        window, d = DILATIONS[slot]
        assert window // d == BLOCK
        length = seq // d
        nb = length // BLOCK
        lo_l = lax.broadcasted_iota(jnp.int32, (length, LANES), 1) < HEAD_DIM
        for r in range(d):
            qdst = slice(r * length, (r + 1) * length)
            kdst = slice(BLOCK + r * length, BLOCK + (r + 1) * length)
            if d == DIL_STAGE * DIL_STAGE:
                r4, j = r % DIL_STAGE, r // DIL_STAGE
                start = r4 * (seq // DIL_STAGE) + j
                qx, kx, vx = (_rows(t, start, length, DIL_STAGE) for t in (tq_ref, tk_ref, tv_ref))
            else:
                qx, kx, vx = (_rows(t.at[0], r, length, d) for t in (q_ref, k_ref, v_ref))
            if d == DIL_STAGE:
                tq_ref[qdst, :] = qx
                tk_ref[qdst, :] = kx
                tv_ref[qdst, :] = vx
            qf = qx.astype(BF16)
            q0[qdst, :] = jnp.where(lo_l, qf, jnp.zeros_like(qf))
            q1[qdst, :] = jnp.where(lo_l, jnp.zeros_like(qf), qf)
            kd[kdst, :] = kx.astype(BF16)
            vf = vx.astype(BF16)
            v0[kdst, :] = jnp.where(lo_l, vf, jnp.ones_like(vf))
            v1[kdst, :] = jnp.where(lo_l, jnp.ones_like(vf), vf)

        def body(g, _, slot=slot, d=d, nb=nb):
            qv = ((q0, v0), (q1, v1))
            for u, hh in [(u, hh) for u in range(DIL_GROUP) for hh in range(2)]:
                t = g * DIL_GROUP + u
                base = pl.multiple_of(t * BLOCK, BLOCK)
                if nb == 1:
                    bias = bias_first
                    nat_start = t
                else:
                    r = t // nb
                    n = t - r * nb
                    bias = jnp.where(n == 0, bias_first, bias_band)
                    nat_start = base if d == 1 else n * (BLOCK * d) + r
                s = lax.dot_general(qv[hh][0][pl.ds(base, BLOCK), :], kd[pl.ds(base, 2 * BLOCK), :],
                                    (((1,), (1,)), ((), ())), preferred_element_type=F32) + bias
                mx = jnp.broadcast_to(jnp.max(s, axis=-1, keepdims=True), (BLOCK, LANES))
                e = jnp.concatenate(
                    [jnp.exp(s[:, LANES * t2:LANES * (t2 + 1)] - mx) for t2 in range(2)], axis=1)
                nd = jnp.dot(e.astype(BF16), qv[hh][1][pl.ds(base, 2 * BLOCK), :],
                             preferred_element_type=F32)
                if d == DIL_STAGE * DIL_STAGE:
                    stage = (t % DIL_STAGE) * (seq // DIL_STAGE) + t // DIL_STAGE
                    _set_rows(und_ref.at[hh], stage, BLOCK, DIL_STAGE, nd)
                    _set_rows(um_ref.at[hh], stage, BLOCK, DIL_STAGE, mx)
                else:
                    _set_rows(nd_ref.at[slot, hh], nat_start, BLOCK, d, nd)
                    _set_rows(m_ref.at[slot, hh], nat_start, BLOCK, d, mx)
            return 0

        lax.fori_loop(0, n_blocks // DIL_GROUP, body, 0)
        if d == DIL_STAGE * DIL_STAGE:
            for hh in range(2):
                for r4 in range(DIL_STAGE):
                    src = slice(r4 * (seq // DIL_STAGE), (r4 + 1) * (seq // DIL_STAGE))
                    _set_rows(nd_ref.at[slot, hh], r4, seq // DIL_STAGE, DIL_STAGE, und_ref[hh, src, :])
                    _set_rows(m_ref.at[slot, hh], r4, seq // DIL_STAGE, DIL_STAGE, um_ref[hh, src, :])

    def merge(c, _):
        rs = pl.ds(pl.multiple_of(c * BLOCK, BLOCK), BLOCK)
        tots = []
        for hh in range(2):
            ms = [m_ref[p, hh, rs, :] for p in range(len(DILATIONS))]
            m_all = jnp.maximum(jnp.maximum(ms[0], ms[1]), ms[2])
            tot = nd_ref[0, hh, rs, :] * jnp.exp(ms[0] - m_all)
            for p in range(1, len(DILATIONS)):
                tot = tot + nd_ref[p, hh, rs, :] * jnp.exp(ms[p] - m_all)
            tots.append(tot)
        dens = pltpu.roll(jnp.where(lo, tots[1], tots[0]), HEAD_DIM, 1)
        o = jnp.where(lo, tots[0], tots[1]) / dens * g_ref[0, rs, :].astype(F32)
        o_ref[0, rs, :] = o.astype(BF16)
        return 0

    lax.fori_loop(0, n_blocks, merge, 0, unroll=4)


def _dilated(od3, odg3):
    batch, seq, _ = od3.shape
    kernel = functools.partial(_dil_kernel, seq=seq)
    n_pat = len(DILATIONS)
    return pl.pallas_call(
        kernel,
        grid=(batch, HEAD_PAIRS),
        in_specs=[
            pl.BlockSpec((1, seq, LANES), lambda b, p: (b, 0, p)),
            pl.BlockSpec((1, seq, LANES), lambda b, p: (b, 0, HEAD_PAIRS + p)),
            pl.BlockSpec((1, seq, LANES), lambda b, p: (b, 0, 2 * HEAD_PAIRS + p)),
            pl.BlockSpec((1, seq, LANES), lambda b, p: (b, 0, p)),
        ],
        out_specs=pl.BlockSpec((1, seq, LANES), lambda b, p: (b, 0, p)),
        out_shape=jax.ShapeDtypeStruct((batch, seq, DIL_W), BF16),
        scratch_shapes=[
            pltpu.VMEM((seq, LANES), BF16),
            pltpu.VMEM((seq, LANES), BF16),
            pltpu.VMEM((seq + BLOCK, LANES), BF16),
            pltpu.VMEM((seq + BLOCK, LANES), BF16),
            pltpu.VMEM((seq + BLOCK, LANES), BF16),
            pltpu.VMEM((n_pat, 2, seq, LANES), F32),
            pltpu.VMEM((n_pat, 2, seq, LANES), F32),
            pltpu.VMEM((seq, LANES), F32),
            pltpu.VMEM((seq, LANES), F32),
            pltpu.VMEM((seq, LANES), F32),
            pltpu.VMEM((2, seq, LANES), F32),
            pltpu.VMEM((2, seq, LANES), F32),
        ],
        compiler_params=pltpu.CompilerParams(
            dimension_semantics=("arbitrary", "arbitrary"), vmem_limit_bytes=VMEM_LIMIT),
        name="dilated_attn",
    )(od3, od3, od3, odg3)


def _mem_kv_kernel(mem_ref, g_ref, w_ref, o_ref):
    x = mem_ref[0]
    ms = jnp.mean(x * x, axis=-1, keepdims=True)
    h = (x * lax.rsqrt(ms + RMS_EPS) * g_ref[...]).astype(BF16)
    o_ref[0] = jnp.dot(h, w_ref[...], preferred_element_type=F32).astype(BF16)


def _mem_kv(mem, g, w):
    batch, mlen, _ = mem.shape
    return pl.pallas_call(
        _mem_kv_kernel,
        grid=(batch,),
        in_specs=[
            pl.BlockSpec((1, mlen, D_MODEL), lambda b: (b, 0, 0)),
            pl.BlockSpec((1, D_MODEL), lambda b: (0, 0)),
            pl.BlockSpec((D_MODEL, 2 * MEM_W), lambda b: (0, 0)),
        ],
        out_specs=pl.BlockSpec((1, mlen, 2 * MEM_W), lambda b: (b, 0, 0)),
        out_shape=jax.ShapeDtypeStruct((batch, mlen, 2 * MEM_W), BF16),
        compiler_params=pltpu.CompilerParams(
            dimension_semantics=("arbitrary",), vmem_limit_bytes=VMEM_LIMIT),
        name="mem_kv",
    )(mem, g, w)


def _mem_attn_kernel(q_ref, g_ref, mk_ref, mv_ref, o_ref):
    tq = q_ref.shape[1]
    chains = [(h, rc) for h in range(MEM_HEADS) for rc in range(tq // BLOCK)]

    def scores(c):
        h, rc = chains[c]
        ls = slice(MEM_HEAD_DIM * h, MEM_HEAD_DIM * (h + 1))
        return lax.dot_general(q_ref[0, BLOCK * rc:BLOCK * (rc + 1), ls], mk_ref[0, :, ls],
                               (((1,), (1,)), ((), ())), preferred_element_type=F32)

    pending = [scores(c) for c in range(LOOKAHEAD)]
    for c, (h, rc) in enumerate(chains):
        if c + LOOKAHEAD < len(chains):
            pending.append(scores(c + LOOKAHEAD))
        ls = slice(MEM_HEAD_DIM * h, MEM_HEAD_DIM * (h + 1))
        rs = slice(BLOCK * rc, BLOCK * (rc + 1))
        s = pending.pop(0) * MEM_SCALE
        mx = jnp.max(s, axis=-1, keepdims=True)
        e = jnp.exp(s - mx)
        den = jnp.sum(e, axis=-1, keepdims=True)
        o = jnp.dot(e.astype(BF16), mv_ref[0, :, ls], preferred_element_type=F32) / den
        o_ref[0, rs, ls] = (o * g_ref[0, rs, ls].astype(F32)).astype(BF16)


def _mem_attn(om3, mkv, *, tq=2048):
    batch, seq, _ = om3.shape
    mlen = mkv.shape[1]
    return pl.pallas_call(
        _mem_attn_kernel,
        grid=(batch, seq // tq),
        in_specs=[
            pl.BlockSpec((1, tq, MEM_W), lambda b, i: (b, i, 0)),
            pl.BlockSpec((1, tq, MEM_W), lambda b, i: (b, i, 1)),
            pl.BlockSpec((1, mlen, MEM_W), lambda b, i: (b, 0, 0)),
            pl.BlockSpec((1, mlen, MEM_W), lambda b, i: (b, 0, 1)),
        ],
        out_specs=pl.BlockSpec((1, tq, MEM_W), lambda b, i: (b, i, 0)),
        out_shape=jax.ShapeDtypeStruct((batch, seq, MEM_W), BF16),
        compiler_params=pltpu.CompilerParams(
            dimension_semantics=("arbitrary", "arbitrary"), vmem_limit_bytes=VMEM_LIMIT),
        name="mem_attn",
    )(om3, om3, mkv, mkv)


def _out_proj_kernel(x_ref, yf_ref, yd_ref, ym_ref, w_ref, g_ref, o_ref, *, final_norm):
    z = x_ref[...]
    z = z + jnp.dot(yf_ref[...], w_ref[0:FOX_W, :], preferred_element_type=F32)
    z = z + jnp.dot(yd_ref[...], w_ref[FOX_W:FOX_W + DIL_W, :], preferred_element_type=F32)
    z = z + jnp.dot(ym_ref[...], w_ref[FOX_W + DIL_W:MIX_W, :], preferred_element_type=F32)
    if final_norm:
        ms = jnp.mean(z * z, axis=-1, keepdims=True)
        z = z * lax.rsqrt(ms + RMS_EPS) * g_ref[...]
    o_ref[...] = z


def _out_proj(x2, yf, yd, ym, w, g, *, final_norm, tm=1024):
    n = x2.shape[0]
    row = lambda i: (i, 0)
    const = lambda i: (0, 0)
    kernel = functools.partial(_out_proj_kernel, final_norm=final_norm)
    return pl.pallas_call(
        kernel,
        grid=(n // tm,),
        in_specs=[
            pl.BlockSpec((tm, D_MODEL), row),
            pl.BlockSpec((tm, FOX_W), row),
            pl.BlockSpec((tm, DIL_W), row),
            pl.BlockSpec((tm, MEM_W), row),
            pl.BlockSpec((MIX_W, D_MODEL), const),
            pl.BlockSpec((1, D_MODEL), const),
        ],
        out_specs=pl.BlockSpec((tm, D_MODEL), row),
        out_shape=jax.ShapeDtypeStruct((n, D_MODEL), F32),
        compiler_params=pltpu.CompilerParams(
            dimension_semantics=("arbitrary",), vmem_limit_bytes=VMEM_LIMIT),
        name="out_proj",
    )(x2, yf, yd, ym, w, g)


def _rope_tables(seq):
    half = ROPE_DIM // 2
    pos = jnp.arange(seq, dtype=F32)
    inv_freq = 1.0 / (ROPE_THETA ** (jnp.arange(0, ROPE_DIM, 2, dtype=F32) / ROPE_DIM))
    ang = pos[:, None] * inv_freq[None, :]
    cos = jnp.cos(ang)
    sin = jnp.sin(ang)
    ones = jnp.ones((seq, HEAD_DIM - ROPE_DIM), F32)
    zeros = jnp.zeros((seq, HEAD_DIM - ROPE_DIM), F32)
    zh = jnp.zeros((seq, half), F32)
    cos_h = jnp.concatenate([cos, cos, ones], axis=1)
    sina_h = jnp.concatenate([zh, sin, zeros], axis=1)
    sinb_h = jnp.concatenate([-sin, zh, zeros], axis=1)
    tile = lambda t: jnp.concatenate([t, t], axis=1)
    return tile(cos_h), tile(sina_h), tile(sinb_h)


def _split_w_in(w):
    wt = jnp.swapaxes(w, 0, 1)
    w_fox = wt[0:FOX_SEG_W].astype(BF16)
    w_fl = jnp.pad(wt[FOX_SEG_W:FOX_SEG_W + FOX_HEADS], ((0, LANES - FOX_HEADS), (0, 0))).astype(BF16)
    w_rest = wt[FOX_SEG_W + FOX_HEADS:].astype(BF16)
    return w_fox, w_fl, w_rest


def kernel(x, mem, norm_g, w_in, b_forget, mem_norm_g, w_mem_kv, w_out, final_norm_g):
    batch, seq, _ = x.shape
    depth = norm_g.shape[0]
    cos_t, sina_t, sinb_t = _rope_tables(seq)
    x2 = x.reshape(batch * seq, D_MODEL)
    for l in range(depth):
        w_fox, w_fl, w_rest = _split_w_in(w_in[l])
        b_fl = jnp.pad(b_forget[l], (0, LANES - FOX_HEADS)).reshape(1, LANES)
        of, od, odg, om = _in_proj(x2, norm_g[l].reshape(1, D_MODEL), w_fox, w_fl, w_rest, b_fl,
                                   cos_t, sina_t, sinb_t, seq=seq)
        yf = _fox(of.reshape(batch, seq, FOX_OUT_W))
        yd = _dilated(od.reshape(batch, seq, DIL_QKV_W), odg.reshape(batch, seq, DIL_W))
        mkv = _mem_kv(mem, mem_norm_g[l].reshape(1, D_MODEL), w_mem_kv[l].astype(BF16))
        ym = _mem_attn(om.reshape(batch, seq, MEM_QG_W), mkv)
        last = l == depth - 1
        x2 = _out_proj(x2, yf.reshape(batch * seq, FOX_W), yd.reshape(batch * seq, DIL_W),
                       ym.reshape(batch * seq, MEM_W), w_out[l].astype(BF16),
                       final_norm_g.reshape(1, D_MODEL), final_norm=last)
    return x2.reshape(batch, seq, D_MODEL)
```

```python
import functools
import itertools
import math

import jax
import jax.numpy as jnp
import numpy as np
from jax import lax
from jax.experimental import pallas as pl
from jax.experimental.pallas import tpu as pltpu

F32 = jnp.float32
BF16 = jnp.bfloat16

D_MODEL = 1024
HEAD_DIM = 64
FOX_HEADS = 12
DIL_HEADS = 12
MEM_HEADS = 4
MEM_HEAD_DIM = 128
FOX_W = FOX_HEADS * HEAD_DIM
DIL_W = DIL_HEADS * HEAD_DIM
MEM_W = MEM_HEADS * MEM_HEAD_DIM
MIX_W = FOX_W + DIL_W + MEM_W
DILATIONS = ((128, 1), (512, 4), (2048, 16))
BLOCK = 128
ROPE_THETA = 500000.0
ROPE_DIM = HEAD_DIM // 4
RMS_EPS = 1e-6
NEG_INF = -1e30
QK_SCALE = 1.0 / math.sqrt(HEAD_DIM)
MEM_SCALE = 1.0 / math.sqrt(MEM_HEAD_DIM)

LANES = 128
HEAD_PAIRS = FOX_HEADS // 2
FOX_AUG_W = FOX_HEADS * LANES
FOX_OUT_W = 3 * FOX_AUG_W + FOX_W
FOX_ROWS = 128
FOX_KEYS = 256
LOOKAHEAD = 2
DIL_GROUP = 16
DIL_STAGE = 4
DIL_QKV_W = 3 * DIL_W
MEM_QG_W = 2 * MEM_W

_SEG_W = dict(fq=FOX_W, fk=FOX_W, fv=FOX_W, fg=FOX_W, dq=DIL_W, dk=DIL_W, dv=DIL_W, dg=DIL_W,
              mq=MEM_W, mg=MEM_W)
_SEG = dict(zip(_SEG_W, itertools.accumulate(_SEG_W.values(), initial=0)))
MAIN_W = sum(_SEG_W.values())
FOX_SEG_W = 4 * FOX_W

V7X_VMEM_BYTES = 64 * 1024 * 1024
VMEM_LIMIT = V7X_VMEM_BYTES - 8 * 1024 * 1024


def _silu(r):
    return r * (1.0 / (1.0 + jnp.exp(-r)))


def _rows(ref, start, size, stride):
    if stride == 1:
        return ref[pl.ds(start, size), :]
    return ref[pl.ds(start, size, stride=stride), :]


def _set_rows(ref, start, size, stride, val):
    if stride == 1:
        ref[pl.ds(start, size), :] = val
    else:
        ref[pl.ds(start, size, stride=stride), :] = val


def _in_proj_kernel(x_ref, g_ref, wfox_ref, wfl_ref, wrest_ref, bf_ref, cos_ref, sina_ref, sinb_ref,
                    of_ref, od_ref, odg_ref, om_ref, carry_ref,
                    *, tm, tiles_per_seq):
    i = pl.program_id(0)
    x = x_ref[...]
    ms = jnp.mean(x * x, axis=-1, keepdims=True)
    h = (x * lax.rsqrt(ms + RMS_EPS) * g_ref[...]).astype(BF16)

    def seg(name):
        off = _SEG[name]
        w_ref = wfox_ref if off < FOX_SEG_W else wrest_ref
        off = off if off < FOX_SEG_W else off - FOX_SEG_W
        return lax.dot_general(h, w_ref[off:off + _SEG_W[name], :], (((1,), (1,)), ((), ())),
                               preferred_element_type=F32)

    fl = lax.dot_general(h, wfl_ref[...], (((1,), (1,)), ((), ())),
                         preferred_element_type=F32) + bf_ref[...]
    logf = jnp.minimum(fl, 0.0) - jnp.log1p(jnp.exp(-jnp.abs(fl)))
    a1 = logf.astype(BF16)
    r1 = logf - a1.astype(F32)
    a2 = r1.astype(BF16)
    a3 = (r1 - a2.astype(F32)).astype(BF16)
    row = lax.broadcasted_iota(jnp.int32, (tm, tm), 0)
    col = lax.broadcasted_iota(jnp.int32, (tm, tm), 1)
    tri = jnp.where(row >= col, 1.0, 0.0).astype(BF16)

    @pl.when(i % tiles_per_seq == 0)
    def _():
        carry_ref[...] = jnp.zeros_like(carry_ref)

    c = (jnp.dot(tri, a1, preferred_element_type=F32)
         + jnp.dot(tri, a2, preferred_element_type=F32)
         + jnp.dot(tri, a3, preferred_element_type=F32)) + carry_ref[...]
    carry_ref[...] = c[tm - 1:tm, :]

    c1 = c.astype(BF16).astype(F32)
    c2 = (c - c1).astype(BF16).astype(F32)
    c3 = (c - c1 - c2).astype(BF16).astype(F32)
    fq = seg("fq") * QK_SCALE
    fk = seg("fk")
    fv = seg("fv")
    lane = lax.broadcasted_iota(jnp.int32, (tm, LANES), 1)
    for hd in range(FOX_HEADS):
        pair = slice(LANES * (hd // 2), LANES * (hd // 2 + 1))
        own = (lane < HEAD_DIM) if hd % 2 == 0 else (lane >= HEAD_DIM)
        base = HEAD_DIM if hd % 2 == 0 else 0
        b1, b2, b3 = (jnp.broadcast_to(t[:, hd:hd + 1], (tm, LANES)) for t in (c1, c2, c3))
        ones_q = jnp.where(jnp.logical_and(lane >= base, lane < base + 3), 1.0, 0.0)
        ones_k = jnp.where(jnp.logical_and(lane >= base + 3, lane < base + 6), 1.0, 0.0)
        q_bias = jnp.where(lane == base + 3, b1, jnp.where(lane == base + 4, b2,
                           jnp.where(lane == base + 5, b3, ones_q)))
        k_bias = jnp.where(lane == base, -b1, jnp.where(lane == base + 1, -b2,
                           jnp.where(lane == base + 2, -b3, ones_k)))
        dst = LANES * hd
        of_ref[:, dst:dst + LANES] = jnp.where(own, fq[:, pair], q_bias).astype(BF16)
        of_ref[:, FOX_AUG_W + dst:FOX_AUG_W + dst + LANES] = \
            jnp.where(own, fk[:, pair], k_bias).astype(BF16)
        of_ref[:, 2 * FOX_AUG_W + dst:2 * FOX_AUG_W + dst + LANES] = \
            jnp.where(own, fv[:, pair], 1.0).astype(BF16)
    of_ref[:, 3 * FOX_AUG_W:3 * FOX_AUG_W + FOX_W] = _silu(seg("fg")).astype(BF16)

    cos = cos_ref[...]
    sina = sina_ref[...]
    sinb = sinb_ref[...]

    def rope_store(r, dst_off, scale):
        for j in range(DIL_W // LANES):
            c = r[:, LANES * j:LANES * (j + 1)]
            rot = c * cos + pltpu.roll(c, ROPE_DIM // 2, 1) * sina \
                + pltpu.roll(c, LANES - ROPE_DIM // 2, 1) * sinb
            if scale != 1.0:
                rot = rot * scale
            od_ref[:, dst_off + LANES * j:dst_off + LANES * (j + 1)] = rot

    rope_store(seg("dq"), 0, QK_SCALE)
    rope_store(seg("dk"), DIL_W, 1.0)
    od_ref[:, 2 * DIL_W:DIL_QKV_W] = seg("dv")
    odg_ref[...] = _silu(seg("dg")).astype(BF16)

    om_ref[:, 0:MEM_W] = seg("mq").astype(BF16)
    om_ref[:, MEM_W:MEM_QG_W] = _silu(seg("mg")).astype(BF16)


def _in_proj(x2, g, w_fox, w_fl, w_rest, b_fl, cos_t, sina_t, sinb_t, *, seq, tm=512):
    n = x2.shape[0]
    tiles_per_seq = seq // tm
    const = lambda i: (0, 0)
    row = lambda i: (i, 0)
    pos = lambda i: (i % tiles_per_seq, 0)
    kernel = functools.partial(_in_proj_kernel, tm=tm, tiles_per_seq=tiles_per_seq)
    return pl.pallas_call(
        kernel,
        grid=(n // tm,),
        in_specs=[
            pl.BlockSpec((tm, D_MODEL), row),
            pl.BlockSpec((1, D_MODEL), const),
            pl.BlockSpec((FOX_SEG_W, D_MODEL), const),
            pl.BlockSpec((LANES, D_MODEL), const),
            pl.BlockSpec((MAIN_W - FOX_SEG_W, D_MODEL), const),
            pl.BlockSpec((1, LANES), const),
            pl.BlockSpec((tm, LANES), pos),
            pl.BlockSpec((tm, LANES), pos),
            pl.BlockSpec((tm, LANES), pos),
        ],
        out_specs=[
            pl.BlockSpec((tm, FOX_OUT_W), row),
            pl.BlockSpec((tm, DIL_QKV_W), row),
            pl.BlockSpec((tm, DIL_W), row),
            pl.BlockSpec((tm, MEM_QG_W), row),
        ],
        out_shape=[
            jax.ShapeDtypeStruct((n, FOX_OUT_W), BF16),
            jax.ShapeDtypeStruct((n, DIL_QKV_W), F32),
            jax.ShapeDtypeStruct((n, DIL_W), BF16),
            jax.ShapeDtypeStruct((n, MEM_QG_W), BF16),
        ],
        scratch_shapes=[pltpu.VMEM((1, LANES), F32)],
        compiler_params=pltpu.CompilerParams(
            dimension_semantics=("arbitrary",), vmem_limit_bytes=VMEM_LIMIT),
        name="in_proj",
    )(x2, g, w_fox, w_fl, w_rest, b_fl, cos_t, sina_t, sinb_t)


def _fox_kernel(q_ref, k_ref, v_ref, g_ref, o_ref, acc_ref, m_ref, *, tq):
    i = pl.program_id(1)
    tk = FOX_KEYS
    row_chunks = tq // FOX_ROWS
    lane = lax.broadcasted_iota(jnp.int32, (FOX_ROWS, LANES), 1)
    lo = lane < HEAD_DIM
    row = lax.broadcasted_iota(jnp.int32, (FOX_ROWS, tk), 0)
    col = lax.broadcasted_iota(jnp.int32, (FOX_ROWS, tk), 1)
    bias_lo = jnp.where(row >= col, 0.0, NEG_INF)
    bias_hi = jnp.where(row + FOX_ROWS >= col, 0.0, NEG_INF)

    def run_chains(chains, first_visit=False):
        def scores(c):
            hd, rc, koff, klen, _ = chains[c]
            ls = slice(LANES * hd, LANES * (hd + 1))
            rs = slice(FOX_ROWS * rc, FOX_ROWS * (rc + 1))
            return lax.dot_general(q_ref[0, rs, ls], k_ref[0, pl.ds(koff, klen), ls],
                                   (((1,), (1,)), ((), ())), preferred_element_type=F32)

        pending = [scores(c) for c in range(LOOKAHEAD)]
        for c, (hd, rc, koff, klen, bias) in enumerate(chains):
            if c + LOOKAHEAD < len(chains):
                pending.append(scores(c + LOOKAHEAD))
            s = pending.pop(0)
            ls = slice(LANES * hd, LANES * (hd + 1))
            rs = slice(FOX_ROWS * rc, FOX_ROWS * (rc + 1))
            if bias is not None:
                s = s + bias
            m_new = jnp.broadcast_to(jnp.max(s, axis=-1, keepdims=True), (FOX_ROWS, LANES))
            if not first_visit:
                m_old = m_ref[hd, rs, :]
                m_new = jnp.maximum(m_old, m_new)
            p = jnp.concatenate(
                [jnp.exp(s[:, LANES * t:LANES * (t + 1)] - m_new) for t in range(klen // LANES)],
                axis=1)
            pv = jnp.dot(p.astype(BF16), v_ref[0, pl.ds(koff, klen), ls],
                         preferred_element_type=F32)
            if first_visit:
                acc_ref[hd, rs, :] = pv
            else:
                acc_ref[hd, rs, :] = jnp.exp(m_old - m_new) * acc_ref[hd, rs, :] + pv
            m_ref[hd, rs, :] = m_new

    def body(j, _):
        first = pl.multiple_of(j * tq, tq)
        run_chains([(hd, rc, first + kt * tk, tk, None)
                    for kt in range(tq // tk) for hd in range(FOX_HEADS) for rc in range(row_chunks)])
        return 0

    def diag_body(t, _, masked):
        first = pl.multiple_of(t * tq, tq)
        chains = []
        for kt in range(tq // tk):
            koff = first + kt * tk
            for hd in range(FOX_HEADS):
                if masked:
                    chains += [(hd, rc, koff, tk, bias_lo if rc % 2 == 0 else bias_hi)
                               for rc in (2 * kt, 2 * kt + 1)]
                else:
                    chains += [(hd, rc, koff, tk, None) for rc in range(2 * kt + 2, row_chunks)]
        run_chains(chains, first_visit=masked)
        return 0

    lax.fori_loop(i, i + 1, functools.partial(diag_body, masked=True), 0)
    lax.fori_loop(i, i + 1, functools.partial(diag_body, masked=False), 0)
    lax.fori_loop(0, i, body, 0)

    for hp in range(HEAD_PAIRS):
        ls = slice(LANES * hp, LANES * (hp + 1))
        for rc in range(row_chunks):
            rs = slice(FOX_ROWS * rc, FOX_ROWS * (rc + 1))
            a0 = acc_ref[2 * hp, rs, :]
            a1 = acc_ref[2 * hp + 1, rs, :]
            dens = pltpu.roll(jnp.where(lo, a1, a0), HEAD_DIM, 1)
            o2 = jnp.where(lo, a0, a1) / dens
            o_ref[0, rs, ls] = (o2 * g_ref[0, rs, ls].astype(F32)).astype(BF16)


def _fox(of3, *, tq=512):
    batch, seq, _ = of3.shape
    kernel = functools.partial(_fox_kernel, tq=tq)
    gate_blk = 3 * FOX_AUG_W // FOX_W
    return pl.pallas_call(
        kernel,
        grid=(batch, seq // tq),
        in_specs=[
            pl.BlockSpec((1, tq, FOX_AUG_W), lambda b, i: (b, i, 0)),
            pl.BlockSpec((1, seq, FOX_AUG_W), lambda b, i: (b, 0, 1)),
            pl.BlockSpec((1, seq, FOX_AUG_W), lambda b, i: (b, 0, 2)),
            pl.BlockSpec((1, tq, FOX_W), lambda b, i: (b, i, gate_blk)),
        ],
        out_specs=pl.BlockSpec((1, tq, FOX_W), lambda b, i: (b, i, 0)),
        out_shape=jax.ShapeDtypeStruct((batch, seq, FOX_W), BF16),
        scratch_shapes=[
            pltpu.VMEM((FOX_HEADS, tq, LANES), F32),
            pltpu.VMEM((FOX_HEADS, tq, LANES), F32),
        ],
        compiler_params=pltpu.CompilerParams(
            dimension_semantics=("arbitrary", "arbitrary"), vmem_limit_bytes=VMEM_LIMIT),
        name="fox_attn",
    )(of3, of3, of3, of3)


def _dil_kernel(q_ref, k_ref, v_ref, g_ref, o_ref, q0, q1, kd, v0, v1, nd_ref, m_ref,
                tq_ref, tk_ref, tv_ref, und_ref, um_ref, *, seq):
    n_blocks = seq // BLOCK
    lane = lax.broadcasted_iota(jnp.int32, (BLOCK, LANES), 1)
    lo = lane < HEAD_DIM
    row2 = lax.broadcasted_iota(jnp.int32, (BLOCK, 2 * BLOCK), 0)
    col2 = lax.broadcasted_iota(jnp.int32, (BLOCK, 2 * BLOCK), 1)
    band = jnp.logical_and(col2 >= row2, col2 <= row2 + BLOCK)
    bias_band = jnp.where(band, 0.0, NEG_INF)
    bias_first = jnp.where(jnp.logical_and(band, col2 >= BLOCK), 0.0, NEG_INF)

    kd[0:BLOCK, :] = jnp.zeros((BLOCK, LANES), BF16)
    v0[0:BLOCK, :] = jnp.zeros((BLOCK, LANES), BF16)
    v1[0:BLOCK, :] = jnp.zeros((BLOCK, LANES), BF16)

    order = sorted(range(len(DILATIONS)), key=lambda p: (DILATIONS[p][1] == 1, DILATIONS[p][1]))
    for slot in order:
        window, d = DILATIONS[slot]
        assert window // d == BLOCK
        length = seq // d
        nb = length // BLOCK
        lo_l = lax.broadcasted_iota(jnp.int32, (length, LANES), 1) < HEAD_DIM
        for r in range(d):
            qdst = slice(r * length, (r + 1) * length)
            kdst = slice(BLOCK + r * length, BLOCK + (r + 1) * length)
            if d == DIL_STAGE * DIL_STAGE:
                r4, j = r % DIL_STAGE, r // DIL_STAGE
                start = r4 * (seq // DIL_STAGE) + j
                qx, kx, vx = (_rows(t, start, length, DIL_STAGE) for t in (tq_ref, tk_ref, tv_ref))
            else:
                qx, kx, vx = (_rows(t.at[0], r, length, d) for t in (q_ref, k_ref, v_ref))
            if d == DIL_STAGE:
                tq_ref[qdst, :] = qx
                tk_ref[qdst, :] = kx
                tv_ref[qdst, :] = vx
            qf = qx.astype(BF16)
            q0[qdst, :] = jnp.where(lo_l, qf, jnp.zeros_like(qf))
            q1[qdst, :] = jnp.where(lo_l, jnp.zeros_like(qf), qf)
            kd[kdst, :] = kx.astype(BF16)
            vf = vx.astype(BF16)
            v0[kdst, :] = jnp.where(lo_l, vf, jnp.ones_like(vf))
            v1[kdst, :] = jnp.where(lo_l, jnp.ones_like(vf), vf)

        def body(g, _, slot=slot, d=d, nb=nb):
            qv = ((q0, v0), (q1, v1))
            for u, hh in [(u, hh) for u in range(DIL_GROUP) for hh in range(2)]:
                t = g * DIL_GROUP + u
                base = pl.multiple_of(t * BLOCK, BLOCK)
                if nb == 1:
                    bias = bias_first
                    nat_start = t
                else:
                    r = t // nb
                    n = t - r * nb
                    bias = jnp.where(n == 0, bias_first, bias_band)
                    nat_start = base if d == 1 else n * (BLOCK * d) + r
                s = lax.dot_general(qv[hh][0][pl.ds(base, BLOCK), :], kd[pl.ds(base, 2 * BLOCK), :],
                                    (((1,), (1,)), ((), ())), preferred_element_type=F32) + bias
                mx = jnp.broadcast_to(jnp.max(s, axis=-1, keepdims=True), (BLOCK, LANES))
                e = jnp.concatenate(
                    [jnp.exp(s[:, LANES * t2:LANES * (t2 + 1)] - mx) for t2 in range(2)], axis=1)
                nd = jnp.dot(e.astype(BF16), qv[hh][1][pl.ds(base, 2 * BLOCK), :],
                             preferred_element_type=F32)
                if d == DIL_STAGE * DIL_STAGE:
                    stage = (t % DIL_STAGE) * (seq // DIL_STAGE) + t // DIL_STAGE
                    _set_rows(und_ref.at[hh], stage, BLOCK, DIL_STAGE, nd)
                    _set_rows(um_ref.at[hh], stage, BLOCK, DIL_STAGE, mx)
                else:
                    _set_rows(nd_ref.at[slot, hh], nat_start, BLOCK, d, nd)
                    _set_rows(m_ref.at[slot, hh], nat_start, BLOCK, d, mx)
            return 0

        lax.fori_loop(0, n_blocks // DIL_GROUP, body, 0)
        if d == DIL_STAGE * DIL_STAGE:
            for hh in range(2):
                for r4 in range(DIL_STAGE):
                    src = slice(r4 * (seq // DIL_STAGE), (r4 + 1) * (seq // DIL_STAGE))
                    _set_rows(nd_ref.at[slot, hh], r4, seq // DIL_STAGE, DIL_STAGE, und_ref[hh, src, :])
                    _set_rows(m_ref.at[slot, hh], r4, seq // DIL_STAGE, DIL_STAGE, um_ref[hh, src, :])

    def merge(c, _):
        rs = pl.ds(pl.multiple_of(c * BLOCK, BLOCK), BLOCK)
        tots = []
        for hh in range(2):
            ms = [m_ref[p, hh, rs, :] for p in range(len(DILATIONS))]
            m_all = jnp.maximum(jnp.maximum(ms[0], ms[1]), ms[2])
            tot = nd_ref[0, hh, rs, :] * jnp.exp(ms[0] - m_all)
            for p in range(1, len(DILATIONS)):
                tot = tot + nd_ref[p, hh, rs, :] * jnp.exp(ms[p] - m_all)
            tots.append(tot)
        dens = pltpu.roll(jnp.where(lo, tots[1], tots[0]), HEAD_DIM, 1)
        o = jnp.where(lo, tots[0], tots[1]) / dens * g_ref[0, rs, :].astype(F32)
        o_ref[0, rs, :] = o.astype(BF16)
        return 0

    lax.fori_loop(0, n_blocks, merge, 0, unroll=4)


def _dilated(od3, odg3):
    batch, seq, _ = od3.shape
    kernel = functools.partial(_dil_kernel, seq=seq)
    n_pat = len(DILATIONS)
    return pl.pallas_call(
        kernel,
        grid=(batch, HEAD_PAIRS),
        in_specs=[
            pl.BlockSpec((1, seq, LANES), lambda b, p: (b, 0, p)),
            pl.BlockSpec((1, seq, LANES), lambda b, p: (b, 0, HEAD_PAIRS + p)),
            pl.BlockSpec((1, seq, LANES), lambda b, p: (b, 0, 2 * HEAD_PAIRS + p)),
            pl.BlockSpec((1, seq, LANES), lambda b, p: (b, 0, p)),
        ],
        out_specs=pl.BlockSpec((1, seq, LANES), lambda b, p: (b, 0, p)),
        out_shape=jax.ShapeDtypeStruct((batch, seq, DIL_W), BF16),
        scratch_shapes=[
            pltpu.VMEM((seq, LANES), BF16),
            pltpu.VMEM((seq, LANES), BF16),
            pltpu.VMEM((seq + BLOCK, LANES), BF16),
            pltpu.VMEM((seq + BLOCK, LANES), BF16),
            pltpu.VMEM((seq + BLOCK, LANES), BF16),
            pltpu.VMEM((n_pat, 2, seq, LANES), F32),
            pltpu.VMEM((n_pat, 2, seq, LANES), F32),
            pltpu.VMEM((seq, LANES), F32),
            pltpu.VMEM((seq, LANES), F32),
            pltpu.VMEM((seq, LANES), F32),
            pltpu.VMEM((2, seq, LANES), F32),
            pltpu.VMEM((2, seq, LANES), F32),
        ],
        compiler_params=pltpu.CompilerParams(
            dimension_semantics=("arbitrary", "arbitrary"), vmem_limit_bytes=VMEM_LIMIT),
        name="dilated_attn",
    )(od3, od3, od3, odg3)


def _mem_kv_kernel(mem_ref, g_ref, w_ref, o_ref):
    x = mem_ref[0]
    ms = jnp.mean(x * x, axis=-1, keepdims=True)
    h = (x * lax.rsqrt(ms + RMS_EPS) * g_ref[...]).astype(BF16)
    o_ref[0] = jnp.dot(h, w_ref[...], preferred_element_type=F32).astype(BF16)


def _mem_kv(mem, g, w):
    batch, mlen, _ = mem.shape
    return pl.pallas_call(
        _mem_kv_kernel,
        grid=(batch,),
        in_specs=[
            pl.BlockSpec((1, mlen, D_MODEL), lambda b: (b, 0, 0)),
            pl.BlockSpec((1, D_MODEL), lambda b: (0, 0)),
            pl.BlockSpec((D_MODEL, 2 * MEM_W), lambda b: (0, 0)),
        ],
        out_specs=pl.BlockSpec((1, mlen, 2 * MEM_W), lambda b: (b, 0, 0)),
        out_shape=jax.ShapeDtypeStruct((batch, mlen, 2 * MEM_W), BF16),
        compiler_params=pltpu.CompilerParams(
            dimension_semantics=("arbitrary",), vmem_limit_bytes=VMEM_LIMIT),
        name="mem_kv",
    )(mem, g, w)


def _mem_attn_kernel(q_ref, g_ref, mk_ref, mv_ref, o_ref):
    tq = q_ref.shape[1]
    chains = [(h, rc) for h in range(MEM_HEADS) for rc in range(tq // BLOCK)]

    def scores(c):
        h, rc = chains[c]
        ls = slice(MEM_HEAD_DIM * h, MEM_HEAD_DIM * (h + 1))
        return lax.dot_general(q_ref[0, BLOCK * rc:BLOCK * (rc + 1), ls], mk_ref[0, :, ls],
                               (((1,), (1,)), ((), ())), preferred_element_type=F32)

    pending = [scores(c) for c in range(LOOKAHEAD)]
    for c, (h, rc) in enumerate(chains):
        if c + LOOKAHEAD < len(chains):
            pending.append(scores(c + LOOKAHEAD))
        ls = slice(MEM_HEAD_DIM * h, MEM_HEAD_DIM * (h + 1))
        rs = slice(BLOCK * rc, BLOCK * (rc + 1))
        s = pending.pop(0) * MEM_SCALE
        mx = jnp.max(s, axis=-1, keepdims=True)
        e = jnp.exp(s - mx)
        den = jnp.sum(e, axis=-1, keepdims=True)
        o = jnp.dot(e.astype(BF16), mv_ref[0, :, ls], preferred_element_type=F32) / den
        o_ref[0, rs, ls] = (o * g_ref[0, rs, ls].astype(F32)).astype(BF16)


def _mem_attn(om3, mkv, *, tq=2048):
    batch, seq, _ = om3.shape
    mlen = mkv.shape[1]
    return pl.pallas_call(
        _mem_attn_kernel,
        grid=(batch, seq // tq),
        in_specs=[
            pl.BlockSpec((1, tq, MEM_W), lambda b, i: (b, i, 0)),
            pl.BlockSpec((1, tq, MEM_W), lambda b, i: (b, i, 1)),
            pl.BlockSpec((1, mlen, MEM_W), lambda b, i: (b, 0, 0)),
            pl.BlockSpec((1, mlen, MEM_W), lambda b, i: (b, 0, 1)),
        ],
        out_specs=pl.BlockSpec((1, tq, MEM_W), lambda b, i: (b, i, 0)),
        out_shape=jax.ShapeDtypeStruct((batch, seq, MEM_W), BF16),
        compiler_params=pltpu.CompilerParams(
            dimension_semantics=("arbitrary", "arbitrary"), vmem_limit_bytes=VMEM_LIMIT),
        name="mem_attn",
    )(om3, om3, mkv, mkv)


def _out_proj_kernel(x_ref, yf_ref, yd_ref, ym_ref, w_ref, g_ref, o_ref, *, final_norm):
    z = x_ref[...]
    z = z + jnp.dot(yf_ref[...], w_ref[0:FOX_W, :], preferred_element_type=F32)
    z = z + jnp.dot(yd_ref[...], w_ref[FOX_W:FOX_W + DIL_W, :], preferred_element_type=F32)
    z = z + jnp.dot(ym_ref[...], w_ref[FOX_W + DIL_W:MIX_W, :], preferred_element_type=F32)
    if final_norm:
        ms = jnp.mean(z * z, axis=-1, keepdims=True)
        z = z * lax.rsqrt(ms + RMS_EPS) * g_ref[...]
    o_ref[...] = z


def _out_proj(x2, yf, yd, ym, w, g, *, final_norm, tm=1024):
    n = x2.shape[0]
    row = lambda i: (i, 0)
    const = lambda i: (0, 0)
    kernel = functools.partial(_out_proj_kernel, final_norm=final_norm)
    return pl.pallas_call(
        kernel,
        grid=(n // tm,),
        in_specs=[
            pl.BlockSpec((tm, D_MODEL), row),
            pl.BlockSpec((tm, FOX_W), row),
            pl.BlockSpec((tm, DIL_W), row),
            pl.BlockSpec((tm, MEM_W), row),
            pl.BlockSpec((MIX_W, D_MODEL), const),
            pl.BlockSpec((1, D_MODEL), const),
        ],
        out_specs=pl.BlockSpec((tm, D_MODEL), row),
        out_shape=jax.ShapeDtypeStruct((n, D_MODEL), F32),
        compiler_params=pltpu.CompilerParams(
            dimension_semantics=("arbitrary",), vmem_limit_bytes=VMEM_LIMIT),
        name="out_proj",
    )(x2, yf, yd, ym, w, g)


def _rope_tables(seq):
    half = ROPE_DIM // 2
    pos = np.arange(seq, dtype=np.float64)
    inv_freq = 1.0 / (ROPE_THETA ** (np.arange(0, ROPE_DIM, 2, dtype=np.float64) / ROPE_DIM))
    ang = pos[:, None] * inv_freq[None, :]
    cos = np.cos(ang)
    sin = np.sin(ang)
    ones = np.ones((seq, HEAD_DIM - ROPE_DIM))
    zeros = np.zeros((seq, HEAD_DIM - ROPE_DIM))
    zh = np.zeros((seq, half))
    cos_h = np.concatenate([cos, cos, ones], axis=1)
    sina_h = np.concatenate([zh, sin, zeros], axis=1)
    sinb_h = np.concatenate([-sin, zh, zeros], axis=1)
    tile = lambda t: jnp.asarray(np.concatenate([t, t], axis=1), F32)
    return tile(cos_h), tile(sina_h), tile(sinb_h)


def _split_w_in(w):
    wt = jnp.swapaxes(w, 0, 1)
    w_fox = wt[0:FOX_SEG_W].astype(BF16)
    w_fl = jnp.pad(wt[FOX_SEG_W:FOX_SEG_W + FOX_HEADS], ((0, LANES - FOX_HEADS), (0, 0))).astype(BF16)
    w_rest = wt[FOX_SEG_W + FOX_HEADS:].astype(BF16)
    return w_fox, w_fl, w_rest


def kernel(x, mem, norm_g, w_in, b_forget, mem_norm_g, w_mem_kv, w_out, final_norm_g):
    batch, seq, _ = x.shape
    depth = norm_g.shape[0]
    cos_t, sina_t, sinb_t = _rope_tables(seq)
    x2 = x.reshape(batch * seq, D_MODEL)
    for l in range(depth):
        w_fox, w_fl, w_rest = _split_w_in(w_in[l])
        b_fl = jnp.pad(b_forget[l], (0, LANES - FOX_HEADS)).reshape(1, LANES)
        of, od, odg, om = _in_proj(x2, norm_g[l].reshape(1, D_MODEL), w_fox, w_fl, w_rest, b_fl,
                                   cos_t, sina_t, sinb_t, seq=seq)
        yf = _fox(of.reshape(batch, seq, FOX_OUT_W))
        yd = _dilated(od.reshape(batch, seq, DIL_QKV_W), odg.reshape(batch, seq, DIL_W))
        mkv = _mem_kv(mem, mem_norm_g[l].reshape(1, D_MODEL), w_mem_kv[l].astype(BF16))
        ym = _mem_attn(om.reshape(batch, seq, MEM_QG_W), mkv)
        last = l == depth - 1
        x2 = _out_proj(x2, yf.reshape(batch * seq, FOX_W), yd.reshape(batch * seq, DIL_W),
                       ym.reshape(batch * seq, MEM_W), w_out[l].astype(BF16),
                       final_norm_g.reshape(1, D_MODEL), final_norm=last)
    return x2.reshape(batch, seq, D_MODEL)
```

```python
import functools
import itertools
import math

import jax
import jax.numpy as jnp
import numpy as np
from jax import lax
from jax.experimental import pallas as pl
from jax.experimental.pallas import tpu as pltpu

F32 = jnp.float32
BF16 = jnp.bfloat16

D_MODEL = 1024
HEAD_DIM = 64
FOX_HEADS = 12
DIL_HEADS = 12
MEM_HEADS = 4
MEM_HEAD_DIM = 128
FOX_W = FOX_HEADS * HEAD_DIM
DIL_W = DIL_HEADS * HEAD_DIM
MEM_W = MEM_HEADS * MEM_HEAD_DIM
MIX_W = FOX_W + DIL_W + MEM_W
DILATIONS = ((128, 1), (512, 4), (2048, 16))
BLOCK = 128
ROPE_THETA = 500000.0
ROPE_DIM = HEAD_DIM // 4
RMS_EPS = 1e-6
NEG_INF = -1e30
QK_SCALE = 1.0 / math.sqrt(HEAD_DIM)
MEM_SCALE = 1.0 / math.sqrt(MEM_HEAD_DIM)

LANES = 128
HEAD_PAIRS = FOX_HEADS // 2
FOX_AUG_W = FOX_HEADS * LANES
FOX_OUT_W = 3 * FOX_AUG_W + FOX_W
FOX_ROWS = 128
FOX_KEYS = 256
LOOKAHEAD = 2
DIL_GROUP = 16
DIL_STAGE = 4
DIL_QKV_W = 3 * DIL_W
MEM_QG_W = 2 * MEM_W

_SEG_W = dict(fq=FOX_W, fk=FOX_W, fv=FOX_W, fg=FOX_W, dq=DIL_W, dk=DIL_W, dv=DIL_W, dg=DIL_W,
              mq=MEM_W, mg=MEM_W)
_SEG = dict(zip(_SEG_W, itertools.accumulate(_SEG_W.values(), initial=0)))
MAIN_W = sum(_SEG_W.values())
FOX_SEG_W = 4 * FOX_W

V7X_VMEM_BYTES = 64 * 1024 * 1024
VMEM_LIMIT = V7X_VMEM_BYTES - 8 * 1024 * 1024


def _silu(r):
    return r * (1.0 / (1.0 + jnp.exp(-r)))


def _rows(ref, start, size, stride):
    if stride == 1:
        return ref[pl.ds(start, size), :]
    return ref[pl.ds(start, size, stride=stride), :]


def _set_rows(ref, start, size, stride, val):
    if stride == 1:
        ref[pl.ds(start, size), :] = val
    else:
        ref[pl.ds(start, size, stride=stride), :] = val


def _in_proj_kernel(x_ref, g_ref, wfox_ref, wfl_ref, wrest_ref, bf_ref, cos_ref, sina_ref, sinb_ref,
                    of_ref, od_ref, odg_ref, om_ref, carry_ref,
                    *, tm, tiles_per_seq):
    i = pl.program_id(0)
    x = x_ref[...]
    ms = jnp.mean(x * x, axis=-1, keepdims=True)
    h = (x * lax.rsqrt(ms + RMS_EPS) * g_ref[...]).astype(BF16)

    def seg(name):
        off = _SEG[name]
        w_ref = wfox_ref if off < FOX_SEG_W else wrest_ref
        off = off if off < FOX_SEG_W else off - FOX_SEG_W
        return lax.dot_general(h, w_ref[off:off + _SEG_W[name], :], (((1,), (1,)), ((), ())),
                               preferred_element_type=F32)

    fl = lax.dot_general(h, wfl_ref[...], (((1,), (1,)), ((), ())),
                         preferred_element_type=F32) + bf_ref[...]
    logf = jnp.minimum(fl, 0.0) - jnp.log1p(jnp.exp(-jnp.abs(fl)))
    a1 = logf.astype(BF16)
    r1 = logf - a1.astype(F32)
    a2 = r1.astype(BF16)
    a3 = (r1 - a2.astype(F32)).astype(BF16)
    row = lax.broadcasted_iota(jnp.int32, (tm, tm), 0)
    col = lax.broadcasted_iota(jnp.int32, (tm, tm), 1)
    tri = jnp.where(row >= col, 1.0, 0.0).astype(BF16)

    @pl.when(i % tiles_per_seq == 0)
    def _():
        carry_ref[...] = jnp.zeros_like(carry_ref)

    c = (jnp.dot(tri, a1, preferred_element_type=F32)
         + jnp.dot(tri, a2, preferred_element_type=F32)
         + jnp.dot(tri, a3, preferred_element_type=F32)) + carry_ref[...]
    carry_ref[...] = c[tm - 1:tm, :]

    c1 = c.astype(BF16).astype(F32)
    c2 = (c - c1).astype(BF16).astype(F32)
    c3 = (c - c1 - c2).astype(BF16).astype(F32)
    fq = seg("fq") * QK_SCALE
    fk = seg("fk")
    fv = seg("fv")
    lane = lax.broadcasted_iota(jnp.int32, (tm, LANES), 1)
    for hd in range(FOX_HEADS):
        pair = slice(LANES * (hd // 2), LANES * (hd // 2 + 1))
        own = (lane < HEAD_DIM) if hd % 2 == 0 else (lane >= HEAD_DIM)
        base = HEAD_DIM if hd % 2 == 0 else 0
        b1, b2, b3 = (jnp.broadcast_to(t[:, hd:hd + 1], (tm, LANES)) for t in (c1, c2, c3))
        ones_q = jnp.where(jnp.logical_and(lane >= base, lane < base + 3), 1.0, 0.0)
        ones_k = jnp.where(jnp.logical_and(lane >= base + 3, lane < base + 6), 1.0, 0.0)
        q_bias = jnp.where(lane == base + 3, b1, jnp.where(lane == base + 4, b2,
                           jnp.where(lane == base + 5, b3, ones_q)))
        k_bias = jnp.where(lane == base, -b1, jnp.where(lane == base + 1, -b2,
                           jnp.where(lane == base + 2, -b3, ones_k)))
        dst = LANES * hd
        of_ref[:, dst:dst + LANES] = jnp.where(own, fq[:, pair], q_bias).astype(BF16)
        of_ref[:, FOX_AUG_W + dst:FOX_AUG_W + dst + LANES] = \
            jnp.where(own, fk[:, pair], k_bias).astype(BF16)
        of_ref[:, 2 * FOX_AUG_W + dst:2 * FOX_AUG_W + dst + LANES] = \
            jnp.where(own, fv[:, pair], 1.0).astype(BF16)
    of_ref[:, 3 * FOX_AUG_W:3 * FOX_AUG_W + FOX_W] = _silu(seg("fg")).astype(BF16)

    cos = cos_ref[...]
    sina = sina_ref[...]
    sinb = sinb_ref[...]

    def rope_store(r, dst_off, scale):
        for j in range(DIL_W // LANES):
            c = r[:, LANES * j:LANES * (j + 1)]
            rot = c * cos + pltpu.roll(c, ROPE_DIM // 2, 1) * sina \
                + pltpu.roll(c, LANES - ROPE_DIM // 2, 1) * sinb
            if scale != 1.0:
                rot = rot * scale
            od_ref[:, dst_off + LANES * j:dst_off + LANES * (j + 1)] = rot

    rope_store(seg("dq"), 0, QK_SCALE)
    rope_store(seg("dk"), DIL_W, 1.0)
    od_ref[:, 2 * DIL_W:DIL_QKV_W] = seg("dv")
    odg_ref[...] = _silu(seg("dg")).astype(BF16)

    om_ref[:, 0:MEM_W] = seg("mq").astype(BF16)
    om_ref[:, MEM_W:MEM_QG_W] = _silu(seg("mg")).astype(BF16)


def _in_proj(x2, g, w_fox, w_fl, w_rest, b_fl, cos_t, sina_t, sinb_t, *, seq, tm=512):
    n = x2.shape[0]
    tiles_per_seq = seq // tm
    const = lambda i: (0, 0)
    row = lambda i: (i, 0)
    pos = lambda i: (i % tiles_per_seq, 0)
    kernel = functools.partial(_in_proj_kernel, tm=tm, tiles_per_seq=tiles_per_seq)
    return pl.pallas_call(
        kernel,
        grid=(n // tm,),
        in_specs=[
            pl.BlockSpec((tm, D_MODEL), row),
            pl.BlockSpec((1, D_MODEL), const),
            pl.BlockSpec((FOX_SEG_W, D_MODEL), const),
            pl.BlockSpec((LANES, D_MODEL), const),
            pl.BlockSpec((MAIN_W - FOX_SEG_W, D_MODEL), const),
            pl.BlockSpec((1, LANES), const),
            pl.BlockSpec((tm, LANES), pos),
            pl.BlockSpec((tm, LANES), pos),
            pl.BlockSpec((tm, LANES), pos),
        ],
        out_specs=[
            pl.BlockSpec((tm, FOX_OUT_W), row),
            pl.BlockSpec((tm, DIL_QKV_W), row),
            pl.BlockSpec((tm, DIL_W), row),
            pl.BlockSpec((tm, MEM_QG_W), row),
        ],
        out_shape=[
            jax.ShapeDtypeStruct((n, FOX_OUT_W), BF16),
            jax.ShapeDtypeStruct((n, DIL_QKV_W), F32),
            jax.ShapeDtypeStruct((n, DIL_W), BF16),
            jax.ShapeDtypeStruct((n, MEM_QG_W), BF16),
        ],
        scratch_shapes=[pltpu.VMEM((1, LANES), F32)],
        compiler_params=pltpu.CompilerParams(
            dimension_semantics=("arbitrary",), vmem_limit_bytes=VMEM_LIMIT),
        name="in_proj",
    )(x2, g, w_fox, w_fl, w_rest, b_fl, cos_t, sina_t, sinb_t)


def _fox_kernel(q_ref, k_ref, v_ref, g_ref, o_ref, acc_ref, m_ref, *, tq):
    i = pl.program_id(1)
    tk = FOX_KEYS
    row_chunks = tq // FOX_ROWS
    lane = lax.broadcasted_iota(jnp.int32, (FOX_ROWS, LANES), 1)
    lo = lane < HEAD_DIM
    row = lax.broadcasted_iota(jnp.int32, (FOX_ROWS, tk), 0)
    col = lax.broadcasted_iota(jnp.int32, (FOX_ROWS, tk), 1)
    bias_lo = jnp.where(row >= col, 0.0, NEG_INF)
    bias_hi = jnp.where(row + FOX_ROWS >= col, 0.0, NEG_INF)

    def run_chains(chains, first_visit=False):
        def scores(c):
            hd, rc, koff, klen, _ = chains[c]
            ls = slice(LANES * hd, LANES * (hd + 1))
            rs = slice(FOX_ROWS * rc, FOX_ROWS * (rc + 1))
            return lax.dot_general(q_ref[0, rs, ls], k_ref[0, pl.ds(koff, klen), ls],
                                   (((1,), (1,)), ((), ())), preferred_element_type=F32)

        pending = [scores(c) for c in range(LOOKAHEAD)]
        for c, (hd, rc, koff, klen, bias) in enumerate(chains):
            if c + LOOKAHEAD < len(chains):
                pending.append(scores(c + LOOKAHEAD))
            s = pending.pop(0)
            ls = slice(LANES * hd, LANES * (hd + 1))
            rs = slice(FOX_ROWS * rc, FOX_ROWS * (rc + 1))
            if bias is not None:
                s = s + bias
            m_new = jnp.broadcast_to(jnp.max(s, axis=-1, keepdims=True), (FOX_ROWS, LANES))
            if not first_visit:
                m_old = m_ref[hd, rs, :]
                m_new = jnp.maximum(m_old, m_new)
            p = jnp.concatenate(
                [jnp.exp(s[:, LANES * t:LANES * (t + 1)] - m_new) for t in range(klen // LANES)],
                axis=1)
            pv = jnp.dot(p.astype(BF16), v_ref[0, pl.ds(koff, klen), ls],
                         preferred_element_type=F32)
            if first_visit:
                acc_ref[hd, rs, :] = pv
            else:
                acc_ref[hd, rs, :] = jnp.exp(m_old - m_new) * acc_ref[hd, rs, :] + pv
            m_ref[hd, rs, :] = m_new

    def body(j, _):
        first = pl.multiple_of(j * tq, tq)
        run_chains([(hd, rc, first + kt * tk, tk, None)
                    for kt in range(tq // tk) for hd in range(FOX_HEADS) for rc in range(row_chunks)])
        return 0

    def diag_body(t, _, masked):
        first = pl.multiple_of(t * tq, tq)
        chains = []
        for kt in range(tq // tk):
            koff = first + kt * tk
            for hd in range(FOX_HEADS):
                if masked:
                    chains += [(hd, rc, koff, tk, bias_lo if rc % 2 == 0 else bias_hi)
                               for rc in (2 * kt, 2 * kt + 1)]
                else:
                    chains += [(hd, rc, koff, tk, None) for rc in range(2 * kt + 2, row_chunks)]
        run_chains(chains, first_visit=masked)
        return 0

    lax.fori_loop(i, i + 1, functools.partial(diag_body, masked=True), 0)
    lax.fori_loop(i, i + 1, functools.partial(diag_body, masked=False), 0)
    lax.fori_loop(0, i, body, 0)

    for hp in range(HEAD_PAIRS):
        ls = slice(LANES * hp, LANES * (hp + 1))
        for rc in range(row_chunks):
            rs = slice(FOX_ROWS * rc, FOX_ROWS * (rc + 1))
            a0 = acc_ref[2 * hp, rs, :]
            a1 = acc_ref[2 * hp + 1, rs, :]
            dens = pltpu.roll(jnp.where(lo, a1, a0), HEAD_DIM, 1)
            o2 = jnp.where(lo, a0, a1) / dens
            o_ref[0, rs, ls] = (o2 * g_ref[0, rs, ls].astype(F32)).astype(BF16)


def _fox(of3, *, tq=512):
    batch, seq, _ = of3.shape
    kernel = functools.partial(_fox_kernel, tq=tq)
    gate_blk = 3 * FOX_AUG_W // FOX_W
    return pl.pallas_call(
        kernel,
        grid=(batch, seq // tq),
        in_specs=[
            pl.BlockSpec((1, tq, FOX_AUG_W), lambda b, i: (b, i, 0)),
            pl.BlockSpec((1, seq, FOX_AUG_W), lambda b, i: (b, 0, 1)),
            pl.BlockSpec((1, seq, FOX_AUG_W), lambda b, i: (b, 0, 2)),
            pl.BlockSpec((1, tq, FOX_W), lambda b, i: (b, i, gate_blk)),
        ],
        out_specs=pl.BlockSpec((1, tq, FOX_W), lambda b, i: (b, i, 0)),
        out_shape=jax.ShapeDtypeStruct((batch, seq, FOX_W), BF16),
        scratch_shapes=[
            pltpu.VMEM((FOX_HEADS, tq, LANES), F32),
            pltpu.VMEM((FOX_HEADS, tq, LANES), F32),
        ],
        compiler_params=pltpu.CompilerParams(
            dimension_semantics=("arbitrary", "arbitrary"), vmem_limit_bytes=VMEM_LIMIT),
        name="fox_attn",
    )(of3, of3, of3, of3)


def _dil_kernel(q_ref, k_ref, v_ref, g_ref, o_ref, q0, q1, kd, v0, v1, nd_ref, m_ref,
                tq_ref, tk_ref, tv_ref, und_ref, um_ref, *, seq):
    n_blocks = seq // BLOCK
    lane = lax.broadcasted_iota(jnp.int32, (BLOCK, LANES), 1)
    lo = lane < HEAD_DIM
    row2 = lax.broadcasted_iota(jnp.int32, (BLOCK, 2 * BLOCK), 0)
    col2 = lax.broadcasted_iota(jnp.int32, (BLOCK, 2 * BLOCK), 1)
    band = jnp.logical_and(col2 >= row2, col2 <= row2 + BLOCK)
    bias_band = jnp.where(band, 0.0, NEG_INF)
    bias_first = jnp.where(jnp.logical_and(band, col2 >= BLOCK), 0.0, NEG_INF)

    kd[0:BLOCK, :] = jnp.zeros((BLOCK, LANES), BF16)
    v0[0:BLOCK, :] = jnp.zeros((BLOCK, LANES), BF16)
    v1[0:BLOCK, :] = jnp.zeros((BLOCK, LANES), BF16)

    order = sorted(range(len(DILATIONS)), key=lambda p: (DILATIONS[p][1] == 1, DILATIONS[p][1]))
    for slot in order:
        window, d = DILATIONS[slot]
        assert window // d == BLOCK
        length = seq // d
        nb = length // BLOCK
        lo_l = lax.broadcasted_iota(jnp.int32, (length, LANES), 1) < HEAD_DIM
        for r in range(d):
            qdst = slice(r * length, (r + 1) * length)
            kdst = slice(BLOCK + r * length, BLOCK + (r + 1) * length)
            if d == DIL_STAGE * DIL_STAGE:
                r4, j = r % DIL_STAGE, r // DIL_STAGE
                start = r4 * (seq // DIL_STAGE) + j
                qx, kx, vx = (_rows(t, start, length, DIL_STAGE) for t in (tq_ref, tk_ref, tv_ref))
            else:
                qx, kx, vx = (_rows(t.at[0], r, length, d) for t in (q_ref, k_ref, v_ref))
            if d == DIL_STAGE:
                tq_ref[qdst, :] = qx
                tk_ref[qdst, :] = kx
                tv_ref[qdst, :] = vx
            qf = qx.astype(BF16)
            q0[qdst, :] = jnp.where(lo_l, qf, jnp.zeros_like(qf))
            q1[qdst, :] = jnp.where(lo_l, jnp.zeros_like(qf), qf)
            kd[kdst, :] = kx.astype(BF16)
            vf = vx.astype(BF16)
            v0[kdst, :] = jnp.where(lo_l, vf, jnp.ones_like(vf))
            v1[kdst, :] = jnp.where(lo_l, jnp.ones_like(vf), vf)

        def body(g, _, slot=slot, d=d, nb=nb):
            qv = ((q0, v0), (q1, v1))
            for u, hh in [(u, hh) for u in range(DIL_GROUP) for hh in range(2)]:
                t = g * DIL_GROUP + u
                base = pl.multiple_of(t * BLOCK, BLOCK)
                if nb == 1:
                    bias = bias_first
                    nat_start = t
                else:
                    r = t // nb
                    n = t - r * nb
                    bias = jnp.where(n == 0, bias_first, bias_band)
                    nat_start = base if d == 1 else n * (BLOCK * d) + r
                s = lax.dot_general(qv[hh][0][pl.ds(base, BLOCK), :], kd[pl.ds(base, 2 * BLOCK), :],
                                    (((1,), (1,)), ((), ())), preferred_element_type=F32) + bias
                mx = jnp.broadcast_to(jnp.max(s, axis=-1, keepdims=True), (BLOCK, LANES))
                e = jnp.concatenate(
                    [jnp.exp(s[:, LANES * t2:LANES * (t2 + 1)] - mx) for t2 in range(2)], axis=1)
                nd = jnp.dot(e.astype(BF16), qv[hh][1][pl.ds(base, 2 * BLOCK), :],
                             preferred_element_type=F32)
                if d == DIL_STAGE * DIL_STAGE:
                    stage = (t % DIL_STAGE) * (seq // DIL_STAGE) + t // DIL_STAGE
                    _set_rows(und_ref.at[hh], stage, BLOCK, DIL_STAGE, nd)
                    _set_rows(um_ref.at[hh], stage, BLOCK, DIL_STAGE, mx)
                else:
                    _set_rows(nd_ref.at[slot, hh], nat_start, BLOCK, d, nd)
                    _set_rows(m_ref.at[slot, hh], nat_start, BLOCK, d, mx)
            return 0

        lax.fori_loop(0, n_blocks // DIL_GROUP, body, 0)
        if d == DIL_STAGE * DIL_STAGE:
            for hh in range(2):
                for r4 in range(DIL_STAGE):
                    src = slice(r4 * (seq // DIL_STAGE), (r4 + 1) * (seq // DIL_STAGE))
                    _set_rows(nd_ref.at[slot, hh], r4, seq // DIL_STAGE, DIL_STAGE, und_ref[hh, src, :])
                    _set_rows(m_ref.at[slot, hh], r4, seq // DIL_STAGE, DIL_STAGE, um_ref[hh, src, :])

    def merge(c, _):
        rs = pl.ds(pl.multiple_of(c * BLOCK, BLOCK), BLOCK)
        tots = []
        for hh in range(2):
            ms = [m_ref[p, hh, rs, :] for p in range(len(DILATIONS))]
            m_all = jnp.maximum(jnp.maximum(ms[0], ms[1]), ms[2])
            tot = nd_ref[0, hh, rs, :] * jnp.exp(ms[0] - m_all)
            for p in range(1, len(DILATIONS)):
                tot = tot + nd_ref[p, hh, rs, :] * jnp.exp(ms[p] - m_all)
            tots.append(tot)
        dens = pltpu.roll(jnp.where(lo, tots[1], tots[0]), HEAD_DIM, 1)
        o = jnp.where(lo, tots[0], tots[1]) / dens * g_ref[0, rs, :].astype(F32)
        o_ref[0, rs, :] = o.astype(BF16)
        return 0

    lax.fori_loop(0, n_blocks, merge, 0, unroll=4)


def _dilated(od3, odg3):
    batch, seq, _ = od3.shape
    kernel = functools.partial(_dil_kernel, seq=seq)
    n_pat = len(DILATIONS)
    return pl.pallas_call(
        kernel,
        grid=(batch, HEAD_PAIRS),
        in_specs=[
            pl.BlockSpec((1, seq, LANES), lambda b, p: (b, 0, p)),
            pl.BlockSpec((1, seq, LANES), lambda b, p: (b, 0, HEAD_PAIRS + p)),
            pl.BlockSpec((1, seq, LANES), lambda b, p: (b, 0, 2 * HEAD_PAIRS + p)),
            pl.BlockSpec((1, seq, LANES), lambda b, p: (b, 0, p)),
        ],
        out_specs=pl.BlockSpec((1, seq, LANES), lambda b, p: (b, 0, p)),
        out_shape=jax.ShapeDtypeStruct((batch, seq, DIL_W), BF16),
        scratch_shapes=[
            pltpu.VMEM((seq, LANES), BF16),
            pltpu.VMEM((seq, LANES), BF16),
            pltpu.VMEM((seq + BLOCK, LANES), BF16),
            pltpu.VMEM((seq + BLOCK, LANES), BF16),
            pltpu.VMEM((seq + BLOCK, LANES), BF16),
            pltpu.VMEM((n_pat, 2, seq, LANES), F32),
            pltpu.VMEM((n_pat, 2, seq, LANES), F32),
            pltpu.VMEM((seq, LANES), F32),
            pltpu.VMEM((seq, LANES), F32),
            pltpu.VMEM((seq, LANES), F32),
            pltpu.VMEM((2, seq, LANES), F32),
            pltpu.VMEM((2, seq, LANES), F32),
        ],
        compiler_params=pltpu.CompilerParams(
            dimension_semantics=("arbitrary", "arbitrary"), vmem_limit_bytes=VMEM_LIMIT),
        name="dilated_attn",
    )(od3, od3, od3, odg3)


def _mem_attn_kernel(q_ref, g_ref, mem_ref, mg_ref, w_ref, o_ref, mkv_ref):
    def project(t, _):
        x = mem_ref[0]
        ms = jnp.mean(x * x, axis=-1, keepdims=True)
        h = (x * lax.rsqrt(ms + RMS_EPS) * mg_ref[...]).astype(BF16)
        mkv_ref[...] = jnp.dot(h, w_ref[...], preferred_element_type=F32).astype(BF16)
        return 0

    b = pl.program_id(0)
    lax.fori_loop(b, b + 1, project, 0)

    tq = q_ref.shape[1]
    chains = [(h, rc) for h in range(MEM_HEADS) for rc in range(tq // BLOCK)]

    def scores(c):
        h, rc = chains[c]
        ls = slice(MEM_HEAD_DIM * h, MEM_HEAD_DIM * (h + 1))
        return lax.dot_general(q_ref[0, BLOCK * rc:BLOCK * (rc + 1), ls], mkv_ref[:, ls],
                               (((1,), (1,)), ((), ())), preferred_element_type=F32)

    pending = [scores(c) for c in range(LOOKAHEAD)]
    for c, (h, rc) in enumerate(chains):
        if c + LOOKAHEAD < len(chains):
            pending.append(scores(c + LOOKAHEAD))
        ls = slice(MEM_HEAD_DIM * h, MEM_HEAD_DIM * (h + 1))
        rs = slice(BLOCK * rc, BLOCK * (rc + 1))
        s = pending.pop(0) * MEM_SCALE
        mx = jnp.max(s, axis=-1, keepdims=True)
        e = jnp.exp(s - mx)
        den = jnp.sum(e, axis=-1, keepdims=True)
        o = jnp.dot(e.astype(BF16), mkv_ref[:, MEM_W + MEM_HEAD_DIM * h:MEM_W + MEM_HEAD_DIM * (h + 1)],
                    preferred_element_type=F32) / den
        o_ref[0, rs, ls] = (o * g_ref[0, rs, ls].astype(F32)).astype(BF16)


def _mem_attn(om3, mem, mem_g, w_kv):
    batch, seq, _ = om3.shape
    mlen = mem.shape[1]
    return pl.pallas_call(
        _mem_attn_kernel,
        grid=(batch,),
        in_specs=[
            pl.BlockSpec((1, seq, MEM_W), lambda b: (b, 0, 0)),
            pl.BlockSpec((1, seq, MEM_W), lambda b: (b, 0, 1)),
            pl.BlockSpec((1, mlen, D_MODEL), lambda b: (b, 0, 0)),
            pl.BlockSpec((1, D_MODEL), lambda b: (0, 0)),
            pl.BlockSpec((D_MODEL, 2 * MEM_W), lambda b: (0, 0)),
        ],
        out_specs=pl.BlockSpec((1, seq, MEM_W), lambda b: (b, 0, 0)),
        out_shape=jax.ShapeDtypeStruct((batch, seq, MEM_W), BF16),
        scratch_shapes=[pltpu.VMEM((mlen, 2 * MEM_W), BF16)],
        compiler_params=pltpu.CompilerParams(
            dimension_semantics=("arbitrary",), vmem_limit_bytes=VMEM_LIMIT),
        name="mem_attn",
    )(om3, om3, mem, mem_g, w_kv)


def _out_proj_kernel(x_ref, yf_ref, yd_ref, ym_ref, w_ref, g_ref, o_ref, *, final_norm):
    z = x_ref[...]
    z = z + jnp.dot(yf_ref[...], w_ref[0:FOX_W, :], preferred_element_type=F32)
    z = z + jnp.dot(yd_ref[...], w_ref[FOX_W:FOX_W + DIL_W, :], preferred_element_type=F32)
    z = z + jnp.dot(ym_ref[...], w_ref[FOX_W + DIL_W:MIX_W, :], preferred_element_type=F32)
    if final_norm:
        ms = jnp.mean(z * z, axis=-1, keepdims=True)
        z = z * lax.rsqrt(ms + RMS_EPS) * g_ref[...]
    o_ref[...] = z


def _out_proj(x2, yf, yd, ym, w, g, *, final_norm, tm=1024):
    n = x2.shape[0]
    row = lambda i: (i, 0)
    const = lambda i: (0, 0)
    kernel = functools.partial(_out_proj_kernel, final_norm=final_norm)
    return pl.pallas_call(
        kernel,
        grid=(n // tm,),
        in_specs=[
            pl.BlockSpec((tm, D_MODEL), row),
            pl.BlockSpec((tm, FOX_W), row),
            pl.BlockSpec((tm, DIL_W), row),
            pl.BlockSpec((tm, MEM_W), row),
            pl.BlockSpec((MIX_W, D_MODEL), const),
            pl.BlockSpec((1, D_MODEL), const),
        ],
        out_specs=pl.BlockSpec((tm, D_MODEL), row),
        out_shape=jax.ShapeDtypeStruct((n, D_MODEL), F32),
        compiler_params=pltpu.CompilerParams(
            dimension_semantics=("arbitrary",), vmem_limit_bytes=VMEM_LIMIT),
        name="out_proj",
    )(x2, yf, yd, ym, w, g)


def _rope_tables(seq):
    half = ROPE_DIM // 2
    pos = np.arange(seq, dtype=np.float64)
    inv_freq = 1.0 / (ROPE_THETA ** (np.arange(0, ROPE_DIM, 2, dtype=np.float64) / ROPE_DIM))
    ang = pos[:, None] * inv_freq[None, :]
    cos = np.cos(ang)
    sin = np.sin(ang)
    ones = np.ones((seq, HEAD_DIM - ROPE_DIM))
    zeros = np.zeros((seq, HEAD_DIM - ROPE_DIM))
    zh = np.zeros((seq, half))
    cos_h = np.concatenate([cos, cos, ones], axis=1)
    sina_h = np.concatenate([zh, sin, zeros], axis=1)
    sinb_h = np.concatenate([-sin, zh, zeros], axis=1)
    tile = lambda t: jnp.asarray(np.concatenate([t, t], axis=1), F32)
    return tile(cos_h), tile(sina_h), tile(sinb_h)


def _split_w_in(w):
    wt = jnp.swapaxes(w, 0, 1)
    w_fox = wt[0:FOX_SEG_W].astype(BF16)
    w_fl = jnp.pad(wt[FOX_SEG_W:FOX_SEG_W + FOX_HEADS], ((0, LANES - FOX_HEADS), (0, 0))).astype(BF16)
    w_rest = wt[FOX_SEG_W + FOX_HEADS:].astype(BF16)
    return w_fox, w_fl, w_rest


def kernel(x, mem, norm_g, w_in, b_forget, mem_norm_g, w_mem_kv, w_out, final_norm_g):
    batch, seq, _ = x.shape
    depth = norm_g.shape[0]
    cos_t, sina_t, sinb_t = _rope_tables(seq)
    x2 = x.reshape(batch * seq, D_MODEL)
    for l in range(depth):
        w_fox, w_fl, w_rest = _split_w_in(w_in[l])
        b_fl = jnp.pad(b_forget[l], (0, LANES - FOX_HEADS)).reshape(1, LANES)
        of, od, odg, om = _in_proj(x2, norm_g[l].reshape(1, D_MODEL), w_fox, w_fl, w_rest, b_fl,
                                   cos_t, sina_t, sinb_t, seq=seq)
        yf = _fox(of.reshape(batch, seq, FOX_OUT_W))
        yd = _dilated(od.reshape(batch, seq, DIL_QKV_W), odg.reshape(batch, seq, DIL_W))
        ym = _mem_attn(om.reshape(batch, seq, MEM_QG_W), mem, mem_norm_g[l].reshape(1, D_MODEL),
                       w_mem_kv[l].astype(BF16))
        last = l == depth - 1
        x2 = _out_proj(x2, yf.reshape(batch * seq, FOX_W), yd.reshape(batch * seq, DIL_W),
                       ym.reshape(batch * seq, MEM_W), w_out[l].astype(BF16),
                       final_norm_g.reshape(1, D_MODEL), final_norm=last)
    return x2.reshape(batch, seq, D_MODEL)
```

```python
import functools
import itertools
import math

import jax
import jax.numpy as jnp
import numpy as np
from jax import lax
from jax.experimental import pallas as pl
from jax.experimental.pallas import tpu as pltpu

F32 = jnp.float32
BF16 = jnp.bfloat16

D_MODEL = 1024
HEAD_DIM = 64
FOX_HEADS = 12
DIL_HEADS = 12
MEM_HEADS = 4
MEM_HEAD_DIM = 128
FOX_W = FOX_HEADS * HEAD_DIM
DIL_W = DIL_HEADS * HEAD_DIM
MEM_W = MEM_HEADS * MEM_HEAD_DIM
MIX_W = FOX_W + DIL_W + MEM_W
DILATIONS = ((128, 1), (512, 4), (2048, 16))
BLOCK = 128
ROPE_THETA = 500000.0
ROPE_DIM = HEAD_DIM // 4
RMS_EPS = 1e-6
NEG_INF = -1e30
QK_SCALE = 1.0 / math.sqrt(HEAD_DIM)
MEM_SCALE = 1.0 / math.sqrt(MEM_HEAD_DIM)

LANES = 128
HEAD_PAIRS = FOX_HEADS // 2
FOX_AUG_W = FOX_HEADS * LANES
FOX_OUT_W = 3 * FOX_AUG_W + FOX_W
FOX_ROWS = 128
FOX_KEYS = 256
LOOKAHEAD = 2
DIL_GROUP = 16
DIL_STAGE = 4
DIL_QKV_W = 3 * DIL_W
MEM_QG_W = 2 * MEM_W

_SEG_W = dict(fq=FOX_W, fk=FOX_W, fv=FOX_W, fg=FOX_W, dq=DIL_W, dk=DIL_W, dv=DIL_W, dg=DIL_W,
              mq=MEM_W, mg=MEM_W)
_SEG = dict(zip(_SEG_W, itertools.accumulate(_SEG_W.values(), initial=0)))
MAIN_W = sum(_SEG_W.values())
FOX_SEG_W = 4 * FOX_W

V7X_VMEM_BYTES = 64 * 1024 * 1024
VMEM_LIMIT = V7X_VMEM_BYTES - 8 * 1024 * 1024


def _silu(r):
    return r * (1.0 / (1.0 + jnp.exp(-r)))


def _rows(ref, start, size, stride):
    if stride == 1:
        return ref[pl.ds(start, size), :]
    return ref[pl.ds(start, size, stride=stride), :]


def _set_rows(ref, start, size, stride, val):
    if stride == 1:
        ref[pl.ds(start, size), :] = val
    else:
        ref[pl.ds(start, size, stride=stride), :] = val


def _in_proj_kernel(x_ref, g_ref, wfox_ref, wfl_ref, wrest_ref, bf_ref, cos_ref, sina_ref, sinb_ref,
                    of_ref, od_ref, odg_ref, om_ref, carry_ref,
                    *, tm, tiles_per_seq):
    i = pl.program_id(0)
    x = x_ref[...]
    ms = jnp.mean(x * x, axis=-1, keepdims=True)
    h = (x * lax.rsqrt(ms + RMS_EPS) * g_ref[...]).astype(BF16)

    def seg(name):
        off = _SEG[name]
        w_ref = wfox_ref if off < FOX_SEG_W else wrest_ref
        off = off if off < FOX_SEG_W else off - FOX_SEG_W
        return lax.dot_general(h, w_ref[off:off + _SEG_W[name], :], (((1,), (1,)), ((), ())),
                               preferred_element_type=F32)

    fl = lax.dot_general(h, wfl_ref[...], (((1,), (1,)), ((), ())),
                         preferred_element_type=F32) + bf_ref[...]
    logf = jnp.minimum(fl, 0.0) - jnp.log1p(jnp.exp(-jnp.abs(fl)))
    a1 = logf.astype(BF16)
    r1 = logf - a1.astype(F32)
    a2 = r1.astype(BF16)
    a3 = (r1 - a2.astype(F32)).astype(BF16)
    row = lax.broadcasted_iota(jnp.int32, (tm, tm), 0)
    col = lax.broadcasted_iota(jnp.int32, (tm, tm), 1)
    tri = jnp.where(row >= col, 1.0, 0.0).astype(BF16)

    @pl.when(i % tiles_per_seq == 0)
    def _():
        carry_ref[...] = jnp.zeros_like(carry_ref)

    c = (jnp.dot(tri, a1, preferred_element_type=F32)
         + jnp.dot(tri, a2, preferred_element_type=F32)
         + jnp.dot(tri, a3, preferred_element_type=F32)) + carry_ref[...]
    carry_ref[...] = c[tm - 1:tm, :]

    c1 = c.astype(BF16).astype(F32)
    c2 = (c - c1).astype(BF16).astype(F32)
    c3 = (c - c1 - c2).astype(BF16).astype(F32)
    fq = seg("fq") * QK_SCALE
    fk = seg("fk")
    fv = seg("fv")
    lane = lax.broadcasted_iota(jnp.int32, (tm, LANES), 1)
    for hd in range(FOX_HEADS):
        pair = slice(LANES * (hd // 2), LANES * (hd // 2 + 1))
        own = (lane < HEAD_DIM) if hd % 2 == 0 else (lane >= HEAD_DIM)
        base = HEAD_DIM if hd % 2 == 0 else 0
        b1, b2, b3 = (jnp.broadcast_to(t[:, hd:hd + 1], (tm, LANES)) for t in (c1, c2, c3))
        ones_q = jnp.where(jnp.logical_and(lane >= base, lane < base + 3), 1.0, 0.0)
        ones_k = jnp.where(jnp.logical_and(lane >= base + 3, lane < base + 6), 1.0, 0.0)
        q_bias = jnp.where(lane == base + 3, b1, jnp.where(lane == base + 4, b2,
                           jnp.where(lane == base + 5, b3, ones_q)))
        k_bias = jnp.where(lane == base, -b1, jnp.where(lane == base + 1, -b2,
                           jnp.where(lane == base + 2, -b3, ones_k)))
        dst = LANES * hd
        of_ref[:, dst:dst + LANES] = jnp.where(own, fq[:, pair], q_bias).astype(BF16)
        of_ref[:, FOX_AUG_W + dst:FOX_AUG_W + dst + LANES] = \
            jnp.where(own, fk[:, pair], k_bias).astype(BF16)
        of_ref[:, 2 * FOX_AUG_W + dst:2 * FOX_AUG_W + dst + LANES] = \
            jnp.where(own, fv[:, pair], 1.0).astype(BF16)
    of_ref[:, 3 * FOX_AUG_W:3 * FOX_AUG_W + FOX_W] = _silu(seg("fg")).astype(BF16)

    cos = cos_ref[...]
    sina = sina_ref[...]
    sinb = sinb_ref[...]

    def rope_store(r, dst_off, scale):
        for j in range(DIL_W // LANES):
            c = r[:, LANES * j:LANES * (j + 1)]
            rot = c * cos + pltpu.roll(c, ROPE_DIM // 2, 1) * sina \
                + pltpu.roll(c, LANES - ROPE_DIM // 2, 1) * sinb
            if scale != 1.0:
                rot = rot * scale
            od_ref[:, dst_off + LANES * j:dst_off + LANES * (j + 1)] = rot

    rope_store(seg("dq"), 0, QK_SCALE)
    rope_store(seg("dk"), DIL_W, 1.0)
    od_ref[:, 2 * DIL_W:DIL_QKV_W] = seg("dv")
    odg_ref[...] = _silu(seg("dg")).astype(BF16)

    om_ref[:, 0:MEM_W] = seg("mq").astype(BF16)
    om_ref[:, MEM_W:MEM_QG_W] = _silu(seg("mg")).astype(BF16)


def _in_proj(x2, g, w_fox, w_fl, w_rest, b_fl, cos_t, sina_t, sinb_t, *, seq, tm=512):
    n = x2.shape[0]
    tiles_per_seq = seq // tm
    const = lambda i: (0, 0)
    row = lambda i: (i, 0)
    pos = lambda i: (i % tiles_per_seq, 0)
    kernel = functools.partial(_in_proj_kernel, tm=tm, tiles_per_seq=tiles_per_seq)
    return pl.pallas_call(
        kernel,
        grid=(n // tm,),
        in_specs=[
            pl.BlockSpec((tm, D_MODEL), row),
            pl.BlockSpec((1, D_MODEL), const),
            pl.BlockSpec((FOX_SEG_W, D_MODEL), const, pipeline_mode=pl.Buffered(1)),
            pl.BlockSpec((LANES, D_MODEL), const, pipeline_mode=pl.Buffered(1)),
            pl.BlockSpec((MAIN_W - FOX_SEG_W, D_MODEL), const, pipeline_mode=pl.Buffered(1)),
            pl.BlockSpec((1, LANES), const),
            pl.BlockSpec((tm, LANES), pos),
            pl.BlockSpec((tm, LANES), pos),
            pl.BlockSpec((tm, LANES), pos),
        ],
        out_specs=[
            pl.BlockSpec((tm, FOX_OUT_W), row),
            pl.BlockSpec((tm, DIL_QKV_W), row),
            pl.BlockSpec((tm, DIL_W), row),
            pl.BlockSpec((tm, MEM_QG_W), row),
        ],
        out_shape=[
            jax.ShapeDtypeStruct((n, FOX_OUT_W), BF16),
            jax.ShapeDtypeStruct((n, DIL_QKV_W), F32),
            jax.ShapeDtypeStruct((n, DIL_W), BF16),
            jax.ShapeDtypeStruct((n, MEM_QG_W), BF16),
        ],
        scratch_shapes=[pltpu.VMEM((1, LANES), F32)],
        compiler_params=pltpu.CompilerParams(
            dimension_semantics=("arbitrary",), vmem_limit_bytes=VMEM_LIMIT),
        name="in_proj",
    )(x2, g, w_fox, w_fl, w_rest, b_fl, cos_t, sina_t, sinb_t)


def _fox_kernel(q_ref, k_ref, v_ref, g_ref, o_ref, acc_ref, m_ref, *, tq):
    i = pl.program_id(1)
    tk = FOX_KEYS
    row_chunks = tq // FOX_ROWS
    lane = lax.broadcasted_iota(jnp.int32, (FOX_ROWS, LANES), 1)
    lo = lane < HEAD_DIM
    row = lax.broadcasted_iota(jnp.int32, (FOX_ROWS, tk), 0)
    col = lax.broadcasted_iota(jnp.int32, (FOX_ROWS, tk), 1)
    bias_lo = jnp.where(row >= col, 0.0, NEG_INF)
    bias_hi = jnp.where(row + FOX_ROWS >= col, 0.0, NEG_INF)

    def run_chains(chains, first_visit=False):
        def scores(c):
            hd, rc, koff, klen, _ = chains[c]
            ls = slice(LANES * hd, LANES * (hd + 1))
            rs = slice(FOX_ROWS * rc, FOX_ROWS * (rc + 1))
            return lax.dot_general(q_ref[0, rs, ls], k_ref[0, pl.ds(koff, klen), ls],
                                   (((1,), (1,)), ((), ())), preferred_element_type=F32)

        pending = [scores(c) for c in range(LOOKAHEAD)]
        for c, (hd, rc, koff, klen, bias) in enumerate(chains):
            if c + LOOKAHEAD < len(chains):
                pending.append(scores(c + LOOKAHEAD))
            s = pending.pop(0)
            ls = slice(LANES * hd, LANES * (hd + 1))
            rs = slice(FOX_ROWS * rc, FOX_ROWS * (rc + 1))
            if bias is not None:
                s = s + bias
            m_new = jnp.broadcast_to(jnp.max(s, axis=-1, keepdims=True), (FOX_ROWS, LANES))
            if not first_visit:
                m_old = m_ref[hd, rs, :]
                m_new = jnp.maximum(m_old, m_new)
            p = jnp.concatenate(
                [jnp.exp(s[:, LANES * t:LANES * (t + 1)] - m_new) for t in range(klen // LANES)],
                axis=1)
            pv = jnp.dot(p.astype(BF16), v_ref[0, pl.ds(koff, klen), ls],
                         preferred_element_type=F32)
            if first_visit:
                acc_ref[hd, rs, :] = pv
            else:
                acc_ref[hd, rs, :] = jnp.exp(m_old - m_new) * acc_ref[hd, rs, :] + pv
            m_ref[hd, rs, :] = m_new

    def body(j, _):
        first = pl.multiple_of(j * tq, tq)
        run_chains([(hd, rc, first + kt * tk, tk, None)
                    for kt in range(tq // tk) for hd in range(FOX_HEADS) for rc in range(row_chunks)])
        return 0

    def diag_body(t, _, masked):
        first = pl.multiple_of(t * tq, tq)
        chains = []
        for kt in range(tq // tk):
            koff = first + kt * tk
            for hd in range(FOX_HEADS):
                if masked:
                    chains += [(hd, rc, koff, tk, bias_lo if rc % 2 == 0 else bias_hi)
                               for rc in (2 * kt, 2 * kt + 1)]
                else:
                    chains += [(hd, rc, koff, tk, None) for rc in range(2 * kt + 2, row_chunks)]
        run_chains(chains, first_visit=masked)
        return 0

    lax.fori_loop(i, i + 1, functools.partial(diag_body, masked=True), 0)
    lax.fori_loop(i, i + 1, functools.partial(diag_body, masked=False), 0)
    lax.fori_loop(0, i, body, 0)

    for hp in range(HEAD_PAIRS):
        ls = slice(LANES * hp, LANES * (hp + 1))
        for rc in range(row_chunks):
            rs = slice(FOX_ROWS * rc, FOX_ROWS * (rc + 1))
            a0 = acc_ref[2 * hp, rs, :]
            a1 = acc_ref[2 * hp + 1, rs, :]
            dens = pltpu.roll(jnp.where(lo, a1, a0), HEAD_DIM, 1)
            o2 = jnp.where(lo, a0, a1) / dens
            o_ref[0, rs, ls] = (o2 * g_ref[0, rs, ls].astype(F32)).astype(BF16)


def _fox(of3, *, tq=512):
    batch, seq, _ = of3.shape
    kernel = functools.partial(_fox_kernel, tq=tq)
    gate_blk = 3 * FOX_AUG_W // FOX_W
    return pl.pallas_call(
        kernel,
        grid=(batch, seq // tq),
        in_specs=[
            pl.BlockSpec((1, tq, FOX_AUG_W), lambda b, i: (b, i, 0)),
            pl.BlockSpec((1, seq, FOX_AUG_W), lambda b, i: (b, 0, 1)),
            pl.BlockSpec((1, seq, FOX_AUG_W), lambda b, i: (b, 0, 2)),
            pl.BlockSpec((1, tq, FOX_W), lambda b, i: (b, i, gate_blk)),
        ],
        out_specs=pl.BlockSpec((1, tq, FOX_W), lambda b, i: (b, i, 0)),
        out_shape=jax.ShapeDtypeStruct((batch, seq, FOX_W), BF16),
        scratch_shapes=[
            pltpu.VMEM((FOX_HEADS, tq, LANES), F32),
            pltpu.VMEM((FOX_HEADS, tq, LANES), F32),
        ],
        compiler_params=pltpu.CompilerParams(
            dimension_semantics=("arbitrary", "arbitrary"), vmem_limit_bytes=VMEM_LIMIT),
        name="fox_attn",
    )(of3, of3, of3, of3)


def _dil_kernel(q_ref, k_ref, v_ref, g_ref, o_ref, q0, q1, kd, v0, v1, nd_ref, m_ref,
                tq_ref, tk_ref, tv_ref, und_ref, um_ref, *, seq):
    n_blocks = seq // BLOCK
    lane = lax.broadcasted_iota(jnp.int32, (BLOCK, LANES), 1)
    lo = lane < HEAD_DIM
    row2 = lax.broadcasted_iota(jnp.int32, (BLOCK, 2 * BLOCK), 0)
    col2 = lax.broadcasted_iota(jnp.int32, (BLOCK, 2 * BLOCK), 1)
    band = jnp.logical_and(col2 >= row2, col2 <= row2 + BLOCK)
    bias_band = jnp.where(band, 0.0, NEG_INF)
    bias_first = jnp.where(jnp.logical_and(band, col2 >= BLOCK), 0.0, NEG_INF)

    kd[0:BLOCK, :] = jnp.zeros((BLOCK, LANES), BF16)
    v0[0:BLOCK, :] = jnp.zeros((BLOCK, LANES), BF16)
    v1[0:BLOCK, :] = jnp.zeros((BLOCK, LANES), BF16)

    order = sorted(range(len(DILATIONS)), key=lambda p: (DILATIONS[p][1] == 1, DILATIONS[p][1]))
    for slot in order:
        window, d = DILATIONS[slot]
        assert window // d == BLOCK
        length = seq // d
        nb = length // BLOCK
        lo_l = lax.broadcasted_iota(jnp.int32, (length, LANES), 1) < HEAD_DIM
        for r in range(d):
            qdst = slice(r * length, (r + 1) * length)
            kdst = slice(BLOCK + r * length, BLOCK + (r + 1) * length)
            if d == DIL_STAGE * DIL_STAGE:
                r4, j = r % DIL_STAGE, r // DIL_STAGE
                start = r4 * (seq // DIL_STAGE) + j
                qx, kx, vx = (_rows(t, start, length, DIL_STAGE) for t in (tq_ref, tk_ref, tv_ref))
            else:
                qx, kx, vx = (_rows(t.at[0], r, length, d) for t in (q_ref, k_ref, v_ref))
            if d == DIL_STAGE:
                tq_ref[qdst, :] = qx
                tk_ref[qdst, :] = kx
                tv_ref[qdst, :] = vx
            qf = qx.astype(BF16)
            q0[qdst, :] = jnp.where(lo_l, qf, jnp.zeros_like(qf))
            q1[qdst, :] = jnp.where(lo_l, jnp.zeros_like(qf), qf)
            kd[kdst, :] = kx.astype(BF16)
            vf = vx.astype(BF16)
            v0[kdst, :] = jnp.where(lo_l, vf, jnp.ones_like(vf))
            v1[kdst, :] = jnp.where(lo_l, jnp.ones_like(vf), vf)

        def body(g, _, slot=slot, d=d, nb=nb):
            qv = ((q0, v0), (q1, v1))
            for u, hh in [(u, hh) for u in range(DIL_GROUP) for hh in range(2)]:
                t = g * DIL_GROUP + u
                base = pl.multiple_of(t * BLOCK, BLOCK)
                if nb == 1:
                    bias = bias_first
                    nat_start = t
                else:
                    r = t // nb
                    n = t - r * nb
                    bias = jnp.where(n == 0, bias_first, bias_band)
                    nat_start = base if d == 1 else n * (BLOCK * d) + r
                s = lax.dot_general(qv[hh][0][pl.ds(base, BLOCK), :], kd[pl.ds(base, 2 * BLOCK), :],
                                    (((1,), (1,)), ((), ())), preferred_element_type=F32) + bias
                mx = jnp.broadcast_to(jnp.max(s, axis=-1, keepdims=True), (BLOCK, LANES))
                e = jnp.concatenate(
                    [jnp.exp(s[:, LANES * t2:LANES * (t2 + 1)] - mx) for t2 in range(2)], axis=1)
                nd = jnp.dot(e.astype(BF16), qv[hh][1][pl.ds(base, 2 * BLOCK), :],
                             preferred_element_type=F32)
                if d == DIL_STAGE * DIL_STAGE:
                    stage = (t % DIL_STAGE) * (seq // DIL_STAGE) + t // DIL_STAGE
                    _set_rows(und_ref.at[hh], stage, BLOCK, DIL_STAGE, nd)
                    _set_rows(um_ref.at[hh], stage, BLOCK, DIL_STAGE, mx)
                else:
                    _set_rows(nd_ref.at[slot, hh], nat_start, BLOCK, d, nd)
                    _set_rows(m_ref.at[slot, hh], nat_start, BLOCK, d, mx)
            return 0

        lax.fori_loop(0, n_blocks // DIL_GROUP, body, 0)
        if d == DIL_STAGE * DIL_STAGE:
            for hh in range(2):
                for r4 in range(DIL_STAGE):
                    src = slice(r4 * (seq // DIL_STAGE), (r4 + 1) * (seq // DIL_STAGE))
                    _set_rows(nd_ref.at[slot, hh], r4, seq // DIL_STAGE, DIL_STAGE, und_ref[hh, src, :])
                    _set_rows(m_ref.at[slot, hh], r4, seq // DIL_STAGE, DIL_STAGE, um_ref[hh, src, :])

    def merge(c, _):
        rs = pl.ds(pl.multiple_of(c * BLOCK, BLOCK), BLOCK)
        tots = []
        for hh in range(2):
            ms = [m_ref[p, hh, rs, :] for p in range(len(DILATIONS))]
            m_all = jnp.maximum(jnp.maximum(ms[0], ms[1]), ms[2])
            tot = nd_ref[0, hh, rs, :] * jnp.exp(ms[0] - m_all)
            for p in range(1, len(DILATIONS)):
                tot = tot + nd_ref[p, hh, rs, :] * jnp.exp(ms[p] - m_all)
            tots.append(tot)
        dens = pltpu.roll(jnp.where(lo, tots[1], tots[0]), HEAD_DIM, 1)
        o = jnp.where(lo, tots[0], tots[1]) / dens * g_ref[0, rs, :].astype(F32)
        o_ref[0, rs, :] = o.astype(BF16)
        return 0

    lax.fori_loop(0, n_blocks, merge, 0, unroll=4)


def _dilated(od3, odg3):
    batch, seq, _ = od3.shape
    kernel = functools.partial(_dil_kernel, seq=seq)
    n_pat = len(DILATIONS)
    return pl.pallas_call(
        kernel,
        grid=(batch, HEAD_PAIRS),
        in_specs=[
            pl.BlockSpec((1, seq, LANES), lambda b, p: (b, 0, p)),
            pl.BlockSpec((1, seq, LANES), lambda b, p: (b, 0, HEAD_PAIRS + p)),
            pl.BlockSpec((1, seq, LANES), lambda b, p: (b, 0, 2 * HEAD_PAIRS + p)),
            pl.BlockSpec((1, seq, LANES), lambda b, p: (b, 0, p)),
        ],
        out_specs=pl.BlockSpec((1, seq, LANES), lambda b, p: (b, 0, p)),
        out_shape=jax.ShapeDtypeStruct((batch, seq, DIL_W), BF16),
        scratch_shapes=[
            pltpu.VMEM((seq, LANES), BF16),
            pltpu.VMEM((seq, LANES), BF16),
            pltpu.VMEM((seq + BLOCK, LANES), BF16),
            pltpu.VMEM((seq + BLOCK, LANES), BF16),
            pltpu.VMEM((seq + BLOCK, LANES), BF16),
            pltpu.VMEM((n_pat, 2, seq, LANES), F32),
            pltpu.VMEM((n_pat, 2, seq, LANES), F32),
            pltpu.VMEM((seq, LANES), F32),
            pltpu.VMEM((seq, LANES), F32),
            pltpu.VMEM((seq, LANES), F32),
            pltpu.VMEM((2, seq, LANES), F32),
            pltpu.VMEM((2, seq, LANES), F32),
        ],
        compiler_params=pltpu.CompilerParams(
            dimension_semantics=("arbitrary", "arbitrary"), vmem_limit_bytes=VMEM_LIMIT),
        name="dilated_attn",
    )(od3, od3, od3, odg3)


def _mem_kv_kernel(mem_ref, g_ref, w_ref, o_ref):
    x = mem_ref[0]
    ms = jnp.mean(x * x, axis=-1, keepdims=True)
    h = (x * lax.rsqrt(ms + RMS_EPS) * g_ref[...]).astype(BF16)
    o_ref[0] = jnp.dot(h, w_ref[...], preferred_element_type=F32).astype(BF16)


def _mem_kv(mem, g, w):
    batch, mlen, _ = mem.shape
    return pl.pallas_call(
        _mem_kv_kernel,
        grid=(batch,),
        in_specs=[
            pl.BlockSpec((1, mlen, D_MODEL), lambda b: (b, 0, 0)),
            pl.BlockSpec((1, D_MODEL), lambda b: (0, 0)),
            pl.BlockSpec((D_MODEL, 2 * MEM_W), lambda b: (0, 0)),
        ],
        out_specs=pl.BlockSpec((1, mlen, 2 * MEM_W), lambda b: (b, 0, 0)),
        out_shape=jax.ShapeDtypeStruct((batch, mlen, 2 * MEM_W), BF16),
        compiler_params=pltpu.CompilerParams(
            dimension_semantics=("arbitrary",), vmem_limit_bytes=VMEM_LIMIT),
        name="mem_kv",
    )(mem, g, w)


def _mem_attn_kernel(q_ref, g_ref, mk_ref, mv_ref, o_ref):
    tq = q_ref.shape[1]
    chains = [(h, rc) for h in range(MEM_HEADS) for rc in range(tq // BLOCK)]

    def scores(c):
        h, rc = chains[c]
        ls = slice(MEM_HEAD_DIM * h, MEM_HEAD_DIM * (h + 1))
        return lax.dot_general(q_ref[0, BLOCK * rc:BLOCK * (rc + 1), ls], mk_ref[0, :, ls],
                               (((1,), (1,)), ((), ())), preferred_element_type=F32)

    pending = [scores(c) for c in range(LOOKAHEAD)]
    for c, (h, rc) in enumerate(chains):
        if c + LOOKAHEAD < len(chains):
            pending.append(scores(c + LOOKAHEAD))
        ls = slice(MEM_HEAD_DIM * h, MEM_HEAD_DIM * (h + 1))
        rs = slice(BLOCK * rc, BLOCK * (rc + 1))
        s = pending.pop(0) * MEM_SCALE
        mx = jnp.max(s, axis=-1, keepdims=True)
        e = jnp.exp(s - mx)
        den = jnp.sum(e, axis=-1, keepdims=True)
        o = jnp.dot(e.astype(BF16), mv_ref[0, :, ls], preferred_element_type=F32) / den
        o_ref[0, rs, ls] = (o * g_ref[0, rs, ls].astype(F32)).astype(BF16)


def _mem_attn(om3, mkv, *, tq=2048):
    batch, seq, _ = om3.shape
    mlen = mkv.shape[1]
    return pl.pallas_call(
        _mem_attn_kernel,
        grid=(batch, seq // tq),
        in_specs=[
            pl.BlockSpec((1, tq, MEM_W), lambda b, i: (b, i, 0)),
            pl.BlockSpec((1, tq, MEM_W), lambda b, i: (b, i, 1)),
            pl.BlockSpec((1, mlen, MEM_W), lambda b, i: (b, 0, 0)),
            pl.BlockSpec((1, mlen, MEM_W), lambda b, i: (b, 0, 1)),
        ],
        out_specs=pl.BlockSpec((1, tq, MEM_W), lambda b, i: (b, i, 0)),
        out_shape=jax.ShapeDtypeStruct((batch, seq, MEM_W), BF16),
        compiler_params=pltpu.CompilerParams(
            dimension_semantics=("arbitrary", "arbitrary"), vmem_limit_bytes=VMEM_LIMIT),
        name="mem_attn",
    )(om3, om3, mkv, mkv)


def _out_proj_kernel(x_ref, yf_ref, yd_ref, ym_ref, w_ref, g_ref, o_ref, *, final_norm):
    z = x_ref[...]
    z = z + jnp.dot(yf_ref[...], w_ref[0:FOX_W, :], preferred_element_type=F32)
    z = z + jnp.dot(yd_ref[...], w_ref[FOX_W:FOX_W + DIL_W, :], preferred_element_type=F32)
    z = z + jnp.dot(ym_ref[...], w_ref[FOX_W + DIL_W:MIX_W, :], preferred_element_type=F32)
    if final_norm:
        ms = jnp.mean(z * z, axis=-1, keepdims=True)
        z = z * lax.rsqrt(ms + RMS_EPS) * g_ref[...]
    o_ref[...] = z


def _out_proj(x2, yf, yd, ym, w, g, *, final_norm, tm=1024):
    n = x2.shape[0]
    row = lambda i: (i, 0)
    const = lambda i: (0, 0)
    kernel = functools.partial(_out_proj_kernel, final_norm=final_norm)
    return pl.pallas_call(
        kernel,
        grid=(n // tm,),
        in_specs=[
            pl.BlockSpec((tm, D_MODEL), row),
            pl.BlockSpec((tm, FOX_W), row),
            pl.BlockSpec((tm, DIL_W), row),
            pl.BlockSpec((tm, MEM_W), row),
            pl.BlockSpec((MIX_W, D_MODEL), const),
            pl.BlockSpec((1, D_MODEL), const),
        ],
        out_specs=pl.BlockSpec((tm, D_MODEL), row),
        out_shape=jax.ShapeDtypeStruct((n, D_MODEL), F32),
        compiler_params=pltpu.CompilerParams(
            dimension_semantics=("arbitrary",), vmem_limit_bytes=VMEM_LIMIT),
        name="out_proj",
    )(x2, yf, yd, ym, w, g)


def _rope_tables(seq):
    half = ROPE_DIM // 2
    pos = np.arange(seq, dtype=np.float64)
    inv_freq = 1.0 / (ROPE_THETA ** (np.arange(0, ROPE_DIM, 2, dtype=np.float64) / ROPE_DIM))
    ang = pos[:, None] * inv_freq[None, :]
    cos = np.cos(ang)
    sin = np.sin(ang)
    ones = np.ones((seq, HEAD_DIM - ROPE_DIM))
    zeros = np.zeros((seq, HEAD_DIM - ROPE_DIM))
    zh = np.zeros((seq, half))
    cos_h = np.concatenate([cos, cos, ones], axis=1)
    sina_h = np.concatenate([zh, sin, zeros], axis=1)
    sinb_h = np.concatenate([-sin, zh, zeros], axis=1)
    tile = lambda t: jnp.asarray(np.concatenate([t, t], axis=1), F32)
    return tile(cos_h), tile(sina_h), tile(sinb_h)


def _split_w_in(w):
    wt = jnp.swapaxes(w, 0, 1)
    w_fox = wt[0:FOX_SEG_W].astype(BF16)
    w_fl = jnp.pad(wt[FOX_SEG_W:FOX_SEG_W + FOX_HEADS], ((0, LANES - FOX_HEADS), (0, 0))).astype(BF16)
    w_rest = wt[FOX_SEG_W + FOX_HEADS:].astype(BF16)
    return w_fox, w_fl, w_rest


def kernel(x, mem, norm_g, w_in, b_forget, mem_norm_g, w_mem_kv, w_out, final_norm_g):
    batch, seq, _ = x.shape
    depth = norm_g.shape[0]
    cos_t, sina_t, sinb_t = _rope_tables(seq)
    x2 = x.reshape(batch * seq, D_MODEL)
    for l in range(depth):
        w_fox, w_fl, w_rest = _split_w_in(w_in[l])
        b_fl = jnp.pad(b_forget[l], (0, LANES - FOX_HEADS)).reshape(1, LANES)
        of, od, odg, om = _in_proj(x2, norm_g[l].reshape(1, D_MODEL), w_fox, w_fl, w_rest, b_fl,
                                   cos_t, sina_t, sinb_t, seq=seq)
        yf = _fox(of.reshape(batch, seq, FOX_OUT_W))
        yd = _dilated(od.reshape(batch, seq, DIL_QKV_W), odg.reshape(batch, seq, DIL_W))
        mkv = _mem_kv(mem, mem_norm_g[l].reshape(1, D_MODEL), w_mem_kv[l].astype(BF16))
        ym = _mem_attn(om.reshape(batch, seq, MEM_QG_W), mkv)
        last = l == depth - 1
        x2 = _out_proj(x2, yf.reshape(batch * seq, FOX_W), yd.reshape(batch * seq, DIL_W),
                       ym.reshape(batch * seq, MEM_W), w_out[l].astype(BF16),
                       final_norm_g.reshape(1, D_MODEL), final_norm=last)
    return x2.reshape(batch, seq, D_MODEL)
```
